```python
import jax
import jax.numpy as jnp
from jax import lax
import numpy as np

D_MODEL = 1024
BATCH = 16
SEQ = 2048
DEPTH = 2

GRID_W = 64
CTX_LEN = 256
HEAD_DIM = 64
N_BRANCH = 4
BRANCH_W = 256
MLA_HEADS = 4
MLA_NOPE = 64
MLA_ROPE = 32
MLA_V = 64
MLA_Q_RANK = 256
MLA_KV_RANK = 256
ML_HEADS = 4
ML_HEAD_DIM = 64
ML_CHUNK = 128
WG_HEADS = 4
WG_KV_HEADS = 2
WINDOW = 128
Q_BLOCK = 128
FN_GROUPS = 4
FN_GROUP_W = 64
D_FF = 2816
ROPE_BASE = 10000.0
EPS = 1e-6
IN_SPLITS = (MLA_Q_RANK, MLA_KV_RANK, MLA_ROPE,
             ML_HEADS * ML_HEAD_DIM, ML_HEADS * ML_HEAD_DIM, ML_HEADS * ML_HEAD_DIM, ML_HEADS * ML_HEAD_DIM, 4 * ML_HEADS,
             WG_HEADS * HEAD_DIM, WG_KV_HEADS * HEAD_DIM, WG_KV_HEADS * HEAD_DIM,
             FN_GROUPS * FN_GROUP_W)
D_IN = sum(IN_SPLITS)

kernel_name = "hybrid_dit_mla_mlstm_swa_fnet"


def rmsnorm(x, g):
    x32 = x.astype(jnp.float32)
    y = x32 * lax.rsqrt(jnp.mean(x32 * x32, axis=-1, keepdims=True) + EPS)
    return (y * g.astype(jnp.float32)).astype(x.dtype)


def split_cols(u, sizes):
    idx = [int(i) for i in np.cumsum(sizes)[:-1]]
    return jnp.split(u, idx, axis=-1)


def to_heads(u, n_heads):
    b, n, _ = u.shape
    return u.reshape(b, n, n_heads, -1).transpose(0, 2, 1, 3)


def from_heads(y):
    b, h, n, d = y.shape
    return y.transpose(0, 2, 1, 3).reshape(b, n, h * d)


def axial_rope(x, row, col):
    d = x.shape[-1]
    half = d // 2
    nf = half // 2
    inv = ROPE_BASE ** (-jnp.arange(nf, dtype=jnp.float32) / nf)
    ang_r = row.astype(jnp.float32)[:, None] * inv
    ang_c = col.astype(jnp.float32)[:, None] * inv

    def rot(xh, ang):
        x1, x2 = xh[..., :nf], xh[..., nf:]
        cos, sin = jnp.cos(ang), jnp.sin(ang)
        return jnp.concatenate([x1 * cos - x2 * sin, x1 * sin + x2 * cos], axis=-1)

    x32 = x.astype(jnp.float32)
    out = jnp.concatenate([rot(x32[..., :half], ang_r), rot(x32[..., half:], ang_c)], axis=-1)
    return out.astype(x.dtype)


def dwconv3(x, w):
    xp = jnp.pad(x, ((0, 0), (1, 1), (0, 0)))
    return xp[:, :-2] * w[0] + xp[:, 1:-1] * w[1] + xp[:, 2:] * w[2]


def softmax_attend(q, k, v, scale):
    s = jnp.einsum('bhqd,bhkd->bhqk', q, k).astype(jnp.float32) * scale
    p = jax.nn.softmax(s, axis=-1).astype(v.dtype)
    return jnp.einsum('bhqk,bhkd->bhqd', p, v)


def blocked_dense_attend(q, k, v, scale):
    b, h, n, d = q.shape
    nb = n // Q_BLOCK
    qb = q.reshape(b, h, nb, Q_BLOCK, d).transpose(2, 0, 1, 3, 4)
    out = lax.map(lambda qi: softmax_attend(qi, k, v, scale), qb)
    return out.transpose(1, 2, 0, 3, 4).reshape(b, h, n, -1)


def window_attend(q, k, v, kc, vc, sink):
    b, hq, n, d = q.shape
    hkv = k.shape[1]
    g = hq // hkv
    nb = n // Q_BLOCK
    scale = d ** -0.5
    qb = q.reshape(b, hkv, g, nb, Q_BLOCK, d)

    def band(t):
        tp = jnp.pad(t, ((0, 0), (0, 0), (WINDOW, WINDOW), (0, 0))).reshape(b, hkv, nb + 2, Q_BLOCK, d)
        return jnp.concatenate([tp[:, :, :-2], tp[:, :, 1:-1], tp[:, :, 2:]], axis=3)

    kb, vb = band(k), band(v)
    blk = jnp.arange(nb)[:, None, None]
    qpos = blk * Q_BLOCK + jnp.arange(Q_BLOCK)[None, :, None]
    kpos = blk * Q_BLOCK - WINDOW + jnp.arange(3 * Q_BLOCK)[None, None, :]
    valid = (jnp.abs(kpos - qpos) <= WINDOW) & (kpos >= 0) & (kpos < n)
    s_loc = jnp.einsum('bhgnqd,bhnkd->bhgnqk', qb, kb).astype(jnp.float32) * scale
    s_loc = jnp.where(valid, s_loc, -jnp.inf)
    s_ctx = jnp.einsum('bhgnqd,bhcd->bhgnqc', qb, kc).astype(jnp.float32) * scale
    c_len = kc.shape[2]
    s_sink = jnp.broadcast_to(sink.astype(jnp.float32).reshape(1, hkv, g, 1, 1, 1), (b, hkv, g, nb, Q_BLOCK, 1))
    p = jax.nn.softmax(jnp.concatenate([s_sink, s_ctx, s_loc], axis=-1), axis=-1).astype(v.dtype)
    out = (jnp.einsum('bhgnqc,bhcd->bhgnqd', p[..., 1:1 + c_len], vc)
           + jnp.einsum('bhgnqk,bhnkd->bhgnqd', p[..., 1 + c_len:], vb))
    return out.reshape(b, hq, n, d)


def ctx_sink_attend(q, kc, vc, sink):
    b, hq, c_len, d = q.shape
    hkv = kc.shape[1]
    g = hq // hkv
    qg = q.reshape(b, hkv, g, c_len, d)
    s = jnp.einsum('bhgqd,bhkd->bhgqk', qg, kc).astype(jnp.float32) * d ** -0.5
    s_sink = jnp.broadcast_to(sink.astype(jnp.float32).reshape(1, hkv, g, 1, 1), (b, hkv, g, c_len, 1))
    p = jax.nn.softmax(jnp.concatenate([s_sink, s], axis=-1), axis=-1)[..., 1:].astype(vc.dtype)
    return jnp.einsum('bhgqk,bhkd->bhgqd', p, vc).reshape(b, hq, c_len, d)


def mla_q(cq, g_qa, w_uq, pos):
    q = to_heads(rmsnorm(cq, g_qa) @ w_uq, MLA_HEADS)
    if pos is not None:
        q = jnp.concatenate([q[..., :MLA_NOPE], axial_rope(q[..., MLA_NOPE:], *pos)], axis=-1)
    return q


def mla_kv(ckv, kr, g_kva, w_ukv, pos):
    kv = to_heads(rmsnorm(ckv, g_kva) @ w_ukv, MLA_HEADS)
    k_nope, v = kv[..., :MLA_NOPE], kv[..., MLA_NOPE:]
    kr = kr[:, None]
    if pos is not None:
        kr = axial_rope(kr, *pos)
    k = jnp.concatenate([k_nope, jnp.broadcast_to(kr, k_nope.shape[:3] + (MLA_ROPE,))], axis=-1)
    return k, v


def mlstm_scan(q, k, v, log_i, log_f, state):
    bsz, nh, n, d = q.shape
    nc = n // ML_CHUNK

    def chunks(t):
        t = t.astype(jnp.float32)
        return jnp.moveaxis(t.reshape(t.shape[:2] + (nc, ML_CHUNK) + t.shape[3:]), 2, 0)

    tril = jnp.tril(jnp.ones((ML_CHUNK, ML_CHUNK), dtype=bool))

    def step(carry, xs):
        c_st, n_st, m_st = carry
        qc, kc, vc, li, lf = xs
        cum = jnp.cumsum(lf, axis=-1)
        log_d = jnp.where(tril, cum[..., :, None] - cum[..., None, :] + li[..., None, :], -jnp.inf)
        log_inter = cum + m_st[..., None]
        m_t = jnp.maximum(log_inter, jnp.max(log_d, axis=-1))
        w_intra = jnp.exp(log_d - m_t[..., None])
        w_inter = jnp.exp(log_inter - m_t)
        s = jnp.einsum('bhtd,bhsd->bhts', qc, kc) * w_intra
        num = jnp.einsum('bhts,bhse->bhte', s, vc) + w_inter[..., None] * jnp.einsum('bhtd,bhde->bhte', qc, c_st)
        den = jnp.sum(s, axis=-1) + w_inter * jnp.einsum('bhtd,bhd->bht', qc, n_st)
        h = num / jnp.maximum(jnp.abs(den), jnp.exp(-m_t))[..., None]
        c_last = cum[..., -1]
        log_w = c_last[..., None] - cum + li
        m_new = jnp.maximum(c_last + m_st, jnp.max(log_w, axis=-1))
        w = jnp.exp(log_w - m_new[..., None])
        decay = jnp.exp(c_last + m_st - m_new)
        c_new = decay[..., None, None] * c_st + jnp.einsum('bhs,bhsd,bhse->bhde', w, kc, vc)
        n_new = decay[..., None] * n_st + jnp.einsum('bhs,bhsd->bhd', w, kc)
        return (c_new, n_new, m_new), h

    xs = (chunks(q), chunks(k), chunks(v), chunks(log_i), chunks(log_f))
    state, h = lax.scan(step, state, xs)
    return jnp.moveaxis(h, 0, 2).reshape(bsz, nh, n, d).astype(q.dtype), state


def mlstm_prep(uq, uk, uv, ug, w_conv, b_gates):
    qk = jax.nn.silu(dwconv3(jnp.concatenate([uq, uk], axis=-1), w_conv))
    q, k = jnp.split(qk, 2, axis=-1)
    q = to_heads(q, ML_HEADS)
    k = to_heads(k, ML_HEADS) * ML_HEAD_DIM ** -0.5
    v = to_heads(uv, ML_HEADS)
    bsz, n, _ = ug.shape
    gt = (ug.astype(jnp.float32) + b_gates.astype(jnp.float32)).reshape(bsz, n, 4, ML_HEADS).transpose(2, 0, 3, 1)
    return q, k, v, gt[0], jax.nn.log_sigmoid(gt[1]), gt[2], jax.nn.log_sigmoid(gt[3])


def mlstm_branch(lat, ctx, w_conv, b_gates, ctx_out):
    uq, uk, uv, uo, ug = lat
    cq, ck, cv, co, cg = ctx
    q, k, v, i_f, f_f, i_b, f_b = mlstm_prep(uq, uk, uv, ug, w_conv, b_gates)
    qc, kc, vc, ic_f, fc_f, ic_b, fc_b = mlstm_prep(cq, ck, cv, cg, w_conv, b_gates)
    bsz = q.shape[0]
    zero = (jnp.zeros((bsz, ML_HEADS, ML_HEAD_DIM, ML_HEAD_DIM), jnp.float32),
            jnp.zeros((bsz, ML_HEADS, ML_HEAD_DIM), jnp.float32),
            jnp.zeros((bsz, ML_HEADS), jnp.float32))
    flip = lambda t: jnp.flip(t, axis=2)
    hc_f, st_f = mlstm_scan(qc, kc, vc, ic_f, fc_f, zero)
    h_f, _ = mlstm_scan(q, k, v, i_f, f_f, st_f)
    hc_b, st_b = mlstm_scan(flip(qc), flip(kc), flip(vc), flip(ic_b), flip(fc_b), zero)
    h_b, _ = mlstm_scan(flip(q), flip(k), flip(v), flip(i_b), flip(f_b), st_b)
    y = jax.nn.sigmoid(uo) * from_heads(h_f + flip(h_b))
    if not ctx_out:
        return y, None
    yc = jax.nn.sigmoid(co) * from_heads(hc_f + flip(hc_b))
    return y, yc


def fourier_mix(u):
    bsz, n, _ = u.shape
    z = jnp.fft.fftn(u.astype(jnp.float32).reshape(bsz, n, FN_GROUPS, FN_GROUP_W), axes=(1, 3), norm="ortho")
    return jnp.real(z).reshape(bsz, n, FN_GROUPS * FN_GROUP_W).astype(u.dtype)


def merge(h, ys, w_gate, b_gate, w_branch, w_out):
    acc = jax.nn.sigmoid(h @ w_gate[0] + b_gate[0]) * (ys[0] @ w_branch[0])
    for s in range(1, N_BRANCH):
        acc = acc + jax.nn.sigmoid(h @ w_gate[s] + b_gate[s]) * (ys[s] @ w_branch[s])
    return acc @ w_out


def token_mixer(h, hc, lp, row, col, ctx_out):
    u = split_cols(h @ lp["w_in"], IN_SPLITS)
    uc = split_cols(hc @ lp["w_in"], IN_SPLITS)
    pos = (row, col)
    scale_a = (MLA_NOPE + MLA_ROPE) ** -0.5
    qa = mla_q(u[0], lp["g_qa"], lp["w_uq"], pos)
    ka, va = mla_kv(u[1], u[2], lp["g_kva"], lp["w_ukv"], pos)
    kca, vca = mla_kv(uc[1], uc[2], lp["g_kva"], lp["w_ukv"], None)
    ya = from_heads(blocked_dense_attend(qa, jnp.concatenate([kca, ka], axis=2),
                                         jnp.concatenate([vca, va], axis=2), scale_a))
    yb, ybc = mlstm_branch(tuple(u[3:8]), tuple(uc[3:8]), lp["w_ml_conv"], lp["b_ml_gates"], ctx_out)
    qw = axial_rope(to_heads(u[8], WG_HEADS), row, col)
    kw = axial_rope(to_heads(u[9], WG_KV_HEADS), row, col)
    vw = to_heads(u[10], WG_KV_HEADS)
    kcw, vcw = to_heads(uc[9], WG_KV_HEADS), to_heads(uc[10], WG_KV_HEADS)
    yc = from_heads(window_attend(qw, kw, vw, kcw, vcw, lp["wg_sink"]))
    yd = fourier_mix(u[11])
    y = merge(h, [ya, yb, yc, yd], lp["w_gate"], lp["b_gate"], lp["w_branch"], lp["w_out"])
    if not ctx_out:
        return y, None
    yca = from_heads(softmax_attend(mla_q(uc[0], lp["g_qa"], lp["w_uq"], None), kca, vca, scale_a))
    ycc = from_heads(ctx_sink_attend(to_heads(uc[8], WG_HEADS), kcw, vcw, lp["wg_sink"]))
    ycd = fourier_mix(uc[11])
    y_ctx = merge(hc, [yca, ybc, ycc, ycd], lp["w_gate"], lp["b_gate"], lp["w_branch"], lp["w_out"])
    return y, y_ctx


def conv_ffn(h, w_up, w_conv, b_conv, w_down):
    a, v = jnp.split(h @ w_up, 2, axis=-1)
    a = dwconv3(a, w_conv) + b_conv
    return (jax.nn.silu(a) * v) @ w_down


def setup_inputs(seed: int = 0) -> dict:
    key = jax.random.key(seed)
    ks = jax.random.split(key, 32)
    f32 = jnp.float32
    L, D = DEPTH, D_MODEL

    def nrm(k, shape, scale=1.0):
        return jax.random.normal(k, shape, f32) * scale

    def gain(k, shape):
        return 1.0 + nrm(k, shape, 0.05)

    f_bias = jnp.linspace(3.0, 6.0, ML_HEADS, dtype=f32)
    b_ml_gates = jnp.concatenate([
        nrm(ks[14], (L, ML_HEADS), 0.1),
        f_bias + nrm(ks[15], (L, ML_HEADS), 0.1),
        nrm(ks[16], (L, ML_HEADS), 0.1),
        f_bias + nrm(ks[17], (L, ML_HEADS), 0.1)], axis=-1)
    return {
        "x": nrm(ks[0], (BATCH, SEQ, D)),
        "c": nrm(ks[1], (BATCH, D)),
        "ctx": nrm(ks[2], (BATCH, CTX_LEN, D)),
        "c_ctx": nrm(ks[3], (D,)),
        "w_mod": nrm(ks[4], (L, D, 6 * D), 0.5 * D ** -0.5),
        "b_mod": nrm(ks[5], (L, 6 * D), 0.02),
        "g_pre_mix": gain(ks[6], (L, D)),
        "g_post_mix": gain(ks[7], (L, D)),
        "g_pre_ffn": gain(ks[8], (L, D)),
        "g_post_ffn": gain(ks[9], (L, D)),
        "w_in": nrm(ks[10], (L, D, D_IN), D ** -0.5),
        "g_qa": gain(ks[11], (L, MLA_Q_RANK)),
        "w_uq": nrm(ks[12], (L, MLA_Q_RANK, MLA_HEADS * (MLA_NOPE + MLA_ROPE)), MLA_Q_RANK ** -0.5),
        "g_kva": gain(ks[13], (L, MLA_KV_RANK)),
        "w_ukv": nrm(ks[18], (L, MLA_KV_RANK, MLA_HEADS * (MLA_NOPE + MLA_V)), MLA_KV_RANK ** -0.5),
        "w_ml_conv": nrm(ks[19], (L, 3, 2 * ML_HEADS * ML_HEAD_DIM), 3 ** -0.5),
        "b_ml_gates": b_ml_gates,
        "wg_sink": nrm(ks[20], (L, WG_HEADS), 0.5),
        "w_gate": nrm(ks[21], (L, N_BRANCH, D, D), D ** -0.5),
        "b_gate": nrm(ks[22], (L, N_BRANCH, D), 0.02),
        "w_branch": nrm(ks[23], (L, N_BRANCH, BRANCH_W, D), BRANCH_W ** -0.5),
        "w_out": nrm(ks[24], (L, D, D), D ** -0.5),
        "w_up": nrm(ks[25], (L, D, 2 * D_FF), D ** -0.5),
        "w_ffn_conv": nrm(ks[26], (L, 3, D_FF), 3 ** -0.5),
        "b_ffn_conv": nrm(ks[27], (L, D_FF), 0.02),
        "w_down": nrm(ks[28], (L, D_FF, D), D_FF ** -0.5),
    }


def reference(x, c, ctx, c_ctx, w_mod, b_mod, g_pre_mix, g_post_mix, g_pre_ffn, g_post_ffn,
              w_in, g_qa, w_uq, g_kva, w_ukv, w_ml_conv, b_ml_gates, wg_sink,
              w_gate, b_gate, w_branch, w_out, w_up, w_ffn_conv, b_ffn_conv, w_down):
    n = x.shape[1]
    rows = n // GRID_W
    row = jnp.repeat(jnp.arange(rows, dtype=jnp.int32), GRID_W)
    col = jnp.tile(jnp.arange(GRID_W, dtype=jnp.int32), rows)
    xc = ctx
    for l in range(DEPTH):
        ctx_out = l < DEPTH - 1
        mod = jax.nn.silu(c) @ w_mod[l] + b_mod[l]
        mod_c = jax.nn.silu(c_ctx) @ w_mod[l] + b_mod[l]
        sh1, sc1, g1, sh2, sc2, g2 = jnp.split(mod[:, None, :], 6, axis=-1)
        shc1, scc1, gc1, shc2, scc2, gc2 = jnp.split(mod_c, 6, axis=-1)
        lp = {"w_in": w_in[l], "g_qa": g_qa[l], "w_uq": w_uq[l], "g_kva": g_kva[l], "w_ukv": w_ukv[l],
              "w_ml_conv": w_ml_conv[l], "b_ml_gates": b_ml_gates[l], "wg_sink": wg_sink[l],
              "w_gate": w_gate[l], "b_gate": b_gate[l], "w_branch": w_branch[l], "w_out": w_out[l]}
        h = rmsnorm(x, g_pre_mix[l]) * (1 + sc1) + sh1
        hc = rmsnorm(xc, g_pre_mix[l]) * (1 + scc1) + shc1
        y, y_ctx = token_mixer(h, hc, lp, row, col, ctx_out)
        x = x + g1 * rmsnorm(y, g_post_mix[l])
        h2 = rmsnorm(x, g_pre_ffn[l]) * (1 + sc2) + sh2
        x = x + g2 * rmsnorm(conv_ffn(h2, w_up[l], w_ffn_conv[l], b_ffn_conv[l], w_down[l]), g_post_ffn[l])
        if ctx_out:
            xc = xc + gc1 * rmsnorm(y_ctx, g_post_mix[l])
            hc2 = rmsnorm(xc, g_pre_ffn[l]) * (1 + scc2) + shc2
            xc = xc + gc2 * rmsnorm(conv_ffn(hc2, w_up[l], w_ffn_conv[l], b_ffn_conv[l], w_down[l]), g_post_ffn[l])
    return x
```

```python
import functools
import math

import numpy as np
import jax
import jax.numpy as jnp
from jax import lax
from jax.experimental import pallas as pl
from jax.experimental.pallas import tpu as pltpu

F32 = jnp.float32
BF16 = jnp.bfloat16

D_MODEL = 1024
GRID_W = 64
N_BRANCH = 4
BRANCH_W = 256
MLA_HEADS = 4
MLA_NOPE = 64
MLA_ROPE = 32
MLA_V = 64
MLA_Q_RANK = 256
MLA_KV_RANK = 256
ML_HEADS = 4
ML_HEAD_DIM = 64
ML_CHUNK = 128
WG_HEADS = 4
WG_KV_HEADS = 2
HEAD_DIM = 64
WINDOW = 128
FN_GROUPS = 4
FN_GROUP_W = 64
D_FF = 2816
ROPE_BASE = 10000.0
EPS = 1e-6

LANES = 128
NEG = -1e30
VMEM_LIMIT = 56 * 1024 * 1024

C_CQ, C_CKV, C_KRA, C_KRB, C_UQ, C_UK, C_UV, C_UO, C_GT, C_WQA, C_WQB, C_WKA, C_WKB, C_WV, C_UF, C_END = (
    0, 256, 512, 640, 768, 1024, 1280, 1536, 1792, 1920, 2176, 2432, 2688, 2944, 3200, 3456)
T_QA_COS, T_QA_SIN, T_KA_COS, T_KA_SIN, T_QW_COS, T_QW_SIN, T_KW_COS, T_KW_SIN = range(8)


def _cparams(sem):
    return pltpu.CompilerParams(dimension_semantics=sem, vmem_limit_bytes=VMEM_LIMIT)


def _const_spec(shape):
    nd = len(shape)
    return pl.BlockSpec(shape, lambda *_: (0,) * nd, pipeline_mode=pl.Buffered(1))


def _rms(x, g):
    ms = jnp.mean(x * x, axis=-1, keepdims=True)
    return x * lax.rsqrt(ms + EPS) * g


def _dot(a, b):
    return jnp.dot(a, b, preferred_element_type=F32)


def _dot_nt(a, b):
    return lax.dot_general(a, b, (((1,), (1,)), ((), ())), preferred_element_type=F32)


def _dot_tn(a, b):
    return lax.dot_general(a, b, (((0,), (0,)), ((), ())), preferred_element_type=F32)


def _sigmoid(x):
    return 1.0 / (1.0 + jnp.exp(-x))


def _mod_kernel(c_ref, w_ref, b_ref, o_ref):
    c = c_ref[...]
    s = (c * _sigmoid(c)).astype(BF16)
    o_ref[0] = _dot(s, w_ref[0].astype(BF16)) + b_ref[0]


def _modulation(c_all, w_mod, b_mod):
    depth, d, n = w_mod.shape
    rows = c_all.shape[0]
    tn = 1536
    return pl.pallas_call(
        _mod_kernel,
        grid=(depth, n // tn),
        in_specs=[pl.BlockSpec((rows, d), lambda l, j: (0, 0)),
                  pl.BlockSpec((1, d, tn), lambda l, j: (l, 0, j)),
                  pl.BlockSpec((1, 1, tn), lambda l, j: (l, 0, j))],
        out_specs=pl.BlockSpec((1, rows, tn), lambda l, j: (l, 0, j)),
        out_shape=jax.ShapeDtypeStruct((depth, rows, n), F32),
        compiler_params=_cparams(("parallel", "parallel")),
        name="modulation",
    )(c_all, w_mod, b_mod.reshape(depth, 1, n))


def _inproj_kernel(x_ref, mod_ref, gpre_ref, w1_ref, gqa_ref, wq_ref, gkva_ref, wkv_ref, tab_ref,
                   qa_ref, ka_ref, va_ref, uq_ref, uk_ref, uv_ref, uo_ref, gt_ref,
                   qw_ref, kw_ref, vw_ref, uf_ref):
    x = x_ref[0]
    m = mod_ref[0]
    h = _rms(x, gpre_ref[...]) * (1.0 + m[1:2]) + m[0:1]
    hb = h.astype(BF16)

    def proj(lo, hi):
        return _dot(hb, w1_ref[:, lo:hi])

    def tab(j):
        return tab_ref[:, LANES * j:LANES * (j + 1)]

    def rep(t, n):
        return jnp.concatenate([t] * n, axis=-1)

    cq = _rms(proj(C_CQ, C_CKV), gqa_ref[...]).astype(BF16)
    q2 = _dot(cq, wq_ref[...])
    q = q2[:, :512] * rep(tab(T_QA_COS), 4) + q2[:, 512:] * rep(tab(T_QA_SIN), 4)
    qa_ref[0] = q.astype(BF16)
    ckv = _rms(proj(C_CKV, C_KRA), gkva_ref[...]).astype(BF16)
    kv = _dot(ckv, wkv_ref[...])
    kr = proj(C_KRA, C_KRB) * tab(T_KA_COS) + proj(C_KRB, C_UQ) * tab(T_KA_SIN)
    ka_ref[0] = (kv[:, :512] + rep(kr, 4)).astype(BF16)
    va_ref[0] = kv[:, 512:].astype(BF16)
    uq_ref[0] = proj(C_UQ, C_UK)
    uk_ref[0] = proj(C_UK, C_UV)
    uv_ref[0] = proj(C_UV, C_UO).astype(BF16)
    uo_ref[0] = proj(C_UO, C_GT)
    gt_ref[0] = proj(C_GT, C_WQA)
    qw = proj(C_WQA, C_WQB) * rep(tab(T_QW_COS), 2) + proj(C_WQB, C_WKA) * rep(tab(T_QW_SIN), 2)
    qw_ref[0] = qw.astype(BF16)
    kw = proj(C_WKA, C_WKB) * rep(tab(T_KW_COS), 2) + proj(C_WKB, C_WV) * rep(tab(T_KW_SIN), 2)
    kw_ref[0] = kw.astype(BF16)
    vw_ref[0] = proj(C_WV, C_UF).astype(BF16)
    uf_ref[0] = proj(C_UF, C_END).astype(BF16)


def _inproj(x, mod, lw, tab, tm):
    b, n, d = x.shape
    tm = min(tm, n)
    row = lambda bi, i: (bi, i, 0)

    def out(w, dt):
        return jax.ShapeDtypeStruct((b, n, w), dt), pl.BlockSpec((1, tm, w), row)

    outs = [out(512, BF16), out(512, BF16), out(256, BF16),
            out(256, F32), out(256, F32), out(256, BF16), out(256, F32), out(128, F32),
            out(256, BF16), out(256, BF16), out(256, BF16), out(256, BF16)]
    return pl.pallas_call(
        _inproj_kernel,
        grid=(b, n // tm),
        in_specs=[pl.BlockSpec((1, tm, d), row),
                  pl.BlockSpec((1, 8, d), lambda bi, i: (bi, 0, 0)),
                  _const_spec((1, d)),
                  _const_spec(lw["w1"].shape),
                  _const_spec((1, MLA_Q_RANK)),
                  _const_spec(lw["wq"].shape),
                  _const_spec((1, MLA_KV_RANK)),
                  _const_spec(lw["wkv"].shape),
                  pl.BlockSpec((tm, 8 * LANES), lambda bi, i: (i, 0))],
        out_specs=[o[1] for o in outs],
        out_shape=[o[0] for o in outs],
        compiler_params=_cparams(("parallel", "parallel")),
        name="inproj",
    )(x, mod, lw["g_pre_mix"], lw["w1"], lw["g_qa"], lw["wq"], lw["g_kva"], lw["wkv"], tab)


def _mla_kernel(*refs, has_lat):
    if has_lat:
        q_ref, kc_ref, vc_ref, kl_ref, vl_ref, o_ref = refs
    else:
        q_ref, kc_ref, vc_ref, o_ref = refs
    tq = q_ref.shape[1]
    lane = lax.broadcasted_iota(jnp.int32, (tq, LANES), 1)
    blocks = []
    for pr in range(MLA_HEADS // 2):
        res = []
        vsl = slice(LANES * pr, LANES * (pr + 1))
        for e in range(2):
            hsl = slice(LANES * (2 * pr + e), LANES * (2 * pr + e + 1))
            q = q_ref[0, :, hsl]
            s_c = _dot_nt(q, kc_ref[0, :, hsl])
            mx = jnp.max(s_c, axis=-1, keepdims=True)
            if has_lat:
                s_l = _dot_nt(q, kl_ref[0, :, hsl])
                mx = jnp.maximum(mx, jnp.max(s_l, axis=-1, keepdims=True))
            p_c = jnp.exp(s_c - mx)
            den = jnp.sum(p_c, axis=-1, keepdims=True)
            o = _dot(p_c.astype(BF16), vc_ref[0, :, vsl])
            if has_lat:
                p_l = jnp.exp(s_l - mx)
                den = den + jnp.sum(p_l, axis=-1, keepdims=True)
                o = o + _dot(p_l.astype(BF16), vl_ref[0, :, vsl])
            res.append(o / den)
        blocks.append(jnp.where(lane < MLA_V, res[0], res[1]))
    o_ref[0] = jnp.concatenate(blocks, axis=-1).astype(o_ref.dtype)


def _mla_attend(q, kc, vc, kl=None, vl=None, tq=512):
    b, n, _ = q.shape
    tq = min(tq, n)
    has_lat = kl is not None
    whole = lambda a: pl.BlockSpec((1,) + a.shape[1:], lambda bi, i: (bi, 0, 0))
    ins = [q, kc, vc] + ([kl, vl] if has_lat else [])
    specs = [pl.BlockSpec((1, tq, 512), lambda bi, i: (bi, i, 0))] + [whole(a) for a in ins[1:]]
    return pl.pallas_call(
        functools.partial(_mla_kernel, has_lat=has_lat),
        grid=(b, n // tq),
        in_specs=specs,
        out_specs=pl.BlockSpec((1, tq, 256), lambda bi, i: (bi, i, 0)),
        out_shape=jax.ShapeDtypeStruct((b, n, 256), BF16),
        compiler_params=_cparams(("parallel", "parallel")),
        name="mla_attend",
    )(*ins)


def _win_kernel(*refs, has_lat, n_lat):
    if has_lat:
        sink_ref, q_ref, kc_ref, vc_ref, kl_ref, vl_ref, o_ref = refs
    else:
        sink_ref, q_ref, kc_ref, vc_ref, o_ref = refs
    tq = q_ref.shape[1]
    wk = tq + 2 * WINDOW
    lane = lax.broadcasted_iota(jnp.int32, (tq, LANES), 1)
    if has_lat:
        t0 = pl.program_id(1) * tq
        start = pl.multiple_of(jnp.clip(t0 - WINDOW, 0, n_lat - wk), LANES)
        qpos = t0 + lax.broadcasted_iota(jnp.int32, (tq, wk), 0)
        kpos = start + lax.broadcasted_iota(jnp.int32, (tq, wk), 1)
        band = jnp.abs(kpos - qpos) <= WINDOW
    blocks = []
    for pr in range(WG_KV_HEADS):
        psl = slice(LANES * pr, LANES * (pr + 1))
        qp = q_ref[0, :, psl]
        kc = kc_ref[0, :, psl]
        vc = vc_ref[0, :, psl]
        if has_lat:
            kl = kl_ref[0, pl.ds(start, wk), psl]
            vl = vl_ref[0, pl.ds(start, wk), psl]
        res = []
        for e in range(2):
            own = (lane < HEAD_DIM) if e == 0 else (lane >= HEAD_DIM)
            q = jnp.where(own, qp, jnp.zeros_like(qp))
            sink = sink_ref[2 * pr + e]
            s_c = _dot_nt(q, kc)
            mx = jnp.maximum(jnp.max(s_c, axis=-1, keepdims=True), sink)
            if has_lat:
                s_l = jnp.where(band, _dot_nt(q, kl), NEG)
                mx = jnp.maximum(mx, jnp.max(s_l, axis=-1, keepdims=True))
            p_c = jnp.exp(s_c - mx)
            den = jnp.exp(sink - mx) + jnp.sum(p_c, axis=-1, keepdims=True)
            o = _dot(p_c.astype(BF16), vc)
            if has_lat:
                p_l = jnp.exp(s_l - mx)
                den = den + jnp.sum(p_l, axis=-1, keepdims=True)
                o = o + _dot(p_l.astype(BF16), vl)
            res.append(o / den)
        blocks.append(jnp.where(lane < HEAD_DIM, res[0], res[1]))
    o_ref[0] = jnp.concatenate(blocks, axis=-1).astype(o_ref.dtype)


def _win_attend(sink, q, kc, vc, kl=None, vl=None, tq=256):
    b, n, _ = q.shape
    tq = min(tq, n)
    has_lat = kl is not None
    whole = lambda a: pl.BlockSpec((1,) + a.shape[1:], lambda bi, i: (bi, 0, 0))
    ins = [q, kc, vc] + ([kl, vl] if has_lat else [])
    specs = ([pl.BlockSpec(memory_space=pltpu.SMEM),
              pl.BlockSpec((1, tq, 256), lambda bi, i: (bi, i, 0))] + [whole(a) for a in ins[1:]])
    return pl.pallas_call(
        functools.partial(_win_kernel, has_lat=has_lat, n_lat=n),
        grid=(b, n // tq),
        in_specs=specs,
        out_specs=pl.BlockSpec((1, tq, 256), lambda bi, i: (bi, i, 0)),
        out_shape=jax.ShapeDtypeStruct((b, n, 256), BF16),
        compiler_params=_cparams(("parallel", "parallel")),
        name="win_attend",
    )(sink, *ins)


def _fourier_kernel(u_ref, cs_ref, dft_ref, o_ref, ab_ref, *, tr):
    n = u_ref.shape[1]
    u = u_ref[0]
    ab = _dot(u, cs_ref[...])
    ab_ref[0:n, :] = ab[:, :256].astype(BF16)
    ab_ref[n:2 * n, :] = ab[:, 256:].astype(BF16)
    scale = 1.0 / math.sqrt(n * FN_GROUP_W)

    def body(r, carry):
        r0 = pl.multiple_of(r * tr, tr)
        y = _dot(dft_ref[pl.ds(r0, tr), :], ab_ref[...])
        o_ref[0, pl.ds(r0, tr), :] = (y * scale).astype(o_ref.dtype)
        return carry

    lax.fori_loop(0, n // tr, body, 0)


def _fourier(u, cs, dft):
    b, n, w = u.shape
    tr = min(512, n)
    return pl.pallas_call(
        functools.partial(_fourier_kernel, tr=tr),
        grid=(b,),
        in_specs=[pl.BlockSpec((1, n, w), lambda bi: (bi, 0, 0)),
                  _const_spec(cs.shape),
                  _const_spec(dft.shape)],
        out_specs=pl.BlockSpec((1, n, w), lambda bi: (bi, 0, 0)),
        out_shape=jax.ShapeDtypeStruct((b, n, w), BF16),
        scratch_shapes=[pltpu.VMEM((2 * n, w), BF16)],
        compiler_params=_cparams(("parallel",)),
        name="fourier",
    )(u, cs, dft)


def _mlstm_kernel(*refs, ctx_out):
    (uq_l, uk_l, uv_l, uo_l, gt_l, uq_c, uk_c, uv_c, uo_c, gt_c, wconv_ref, bg_ref) = refs[:12]
    if ctx_out:
        y_l, y_c = refs[12:14]
        scratch = refs[14:]
    else:
        y_l = refs[12]
        y_c = None
        scratch = refs[13:]
    qs_l, ks_l, hacc_l, qs_c, ks_c, hacc_c, cst, mst = scratch
    L = ML_CHUNK
    row = lax.broadcasted_iota(jnp.int32, (L, L), 0)
    col = lax.broadcasted_iota(jnp.int32, (L, L), 1)
    tril_f = (col <= row).astype(F32)
    row2 = lax.broadcasted_iota(jnp.int32, (L, 256), 0)
    bg = bg_ref[...]

    def prep(x_ref, out_ref, w3, scale):
        n = x_ref.shape[1]
        nc = n // L

        def body(c, carry):
            r0 = pl.multiple_of(c * L, L)
            xc = x_ref[0, pl.ds(r0, L), :]
            pstart = pl.multiple_of(jnp.maximum(r0 - 8, 0), 8)
            nstart = pl.multiple_of(jnp.minimum(r0 + L, n - 8), 8)
            prow = x_ref[0, pl.ds(pstart, 8), :][7:8, :] * (c > 0).astype(F32)
            nrow = x_ref[0, pl.ds(nstart, 8), :][0:1, :] * (c < nc - 1).astype(F32)
            xp = jnp.where(row2 == 0, prow, pltpu.roll(xc, 1, axis=0))
            xn = jnp.where(row2 == L - 1, nrow, pltpu.roll(xc, L - 1, axis=0))
            a = xp * w3[0:1, :] + xc * w3[1:2, :] + xn * w3[2:3, :]
            out_ref[pl.ds(r0, L), :] = (a * _sigmoid(a) * scale).astype(BF16)
            return carry

        lax.fori_loop(0, nc, body, 0)

    wq3 = wconv_ref[:, 0:256]
    wk3 = wconv_ref[:, 256:512]
    kscale = ML_HEAD_DIM ** -0.5
    prep(uq_l, qs_l, wq3, 1.0)
    prep(uk_l, ks_l, wk3, kscale)
    prep(uq_c, qs_c, wq3, 1.0)
    prep(uk_c, ks_c, wk3, kscale)
    hacc_l[...] = jnp.zeros_like(hacc_l)
    if ctx_out:
        hacc_c[...] = jnp.zeros_like(hacc_c)
    cst[...] = jnp.zeros_like(cst)
    mst[...] = jnp.zeros_like(mst)

    def chunk_step(qs, ks, uv, gt, hacc, c, d, need_h):
        r0 = pl.multiple_of(c * L, L)
        g = gt[0, pl.ds(r0, L), :] + bg
        ls = jnp.minimum(g, 0.0) - jnp.log(1.0 + jnp.exp(-jnp.abs(g)))
        cum = jnp.dot(tril_f, ls, precision=lax.Precision.HIGHEST, preferred_element_type=F32)
        tot = cum[L - 1:L, :]
        if d == 1:
            cum = tot - cum + ls
        g_t = g.T
        cum_t = cum.T
        lane = col
        qb = [qs[pl.ds(r0, L), LANES * p:LANES * (p + 1)] for p in range(2)]
        kb = [ks[pl.ds(r0, L), LANES * p:LANES * (p + 1)] for p in range(2)]
        vb = [uv[0, pl.ds(r0, L), LANES * p:LANES * (p + 1)] for p in range(2)]
        mask = (col <= row) if d == 0 else (col >= row)
        hs = []
        for hh in range(ML_HEADS):
            idx = d * ML_HEADS + hh
            icol = (0 if d == 0 else 8) + hh
            fcol = icol + 4
            p = hh // 2
            own = (lane < ML_HEAD_DIM) if hh % 2 == 0 else (lane >= ML_HEAD_DIM)
            a_col = cum[:, fcol:fcol + 1]
            li_col = g[:, icol:icol + 1]
            m_st = mst[idx][:, 0:1]
            c_aug = cst[idx]
            kh = jnp.where(own, kb[p], jnp.zeros_like(kb[p]))
            v_aug = jnp.where(own, vb[p], jnp.ones_like(vb[p]))
            if need_h:
                qh = jnp.where(own, qb[p], jnp.zeros_like(qb[p]))
                b_row = g_t[icol:icol + 1, :] - cum_t[fcol:fcol + 1, :]
                logd = jnp.where(mask, a_col + b_row, NEG)
                log_inter = a_col + m_st
                m_t = jnp.maximum(log_inter, jnp.max(logd, axis=-1, keepdims=True))
                w_intra = jnp.exp(logd - m_t)
                w_inter = jnp.exp(log_inter - m_t)
                s = _dot_nt(qh, kh) * w_intra
                r = _dot(s.astype(BF16), v_aug) + w_inter * _dot(qh, c_aug.astype(BF16))
                den = pltpu.roll(r, ML_HEAD_DIM, axis=1)
                hs.append(r / jnp.maximum(jnp.abs(den), jnp.exp(-m_t)))
            c_last = tot[:, fcol:fcol + 1]
            lw = c_last - a_col + li_col
            m_new = jnp.maximum(c_last + m_st, jnp.max(lw, axis=0, keepdims=True))
            kw = (kh.astype(F32) * jnp.exp(lw - m_new)).astype(BF16)
            cst[idx] = jnp.exp(c_last + m_st - m_new) * c_aug + _dot_tn(kw, v_aug)
            mst[idx] = jnp.broadcast_to(m_new, (1, LANES))
        if need_h:
            for p in range(2):
                hp = jnp.where(lane < ML_HEAD_DIM, hs[2 * p], hs[2 * p + 1])
                hacc[pl.ds(r0, L), LANES * p:LANES * (p + 1)] += hp

    def scan(qs, ks, uv, gt, hacc, n, need_h):
        nc = n // L

        def body(j, carry):
            chunk_step(qs, ks, uv, gt, hacc, j, 0, need_h)
            chunk_step(qs, ks, uv, gt, hacc, nc - 1 - j, 1, need_h)
            return carry

        lax.fori_loop(0, nc, body, 0)

    scan(qs_c, ks_c, uv_c, gt_c, hacc_c, uq_c.shape[1], ctx_out)
    scan(qs_l, ks_l, uv_l, gt_l, hacc_l, uq_l.shape[1], True)

    def finish(uo, hacc, y):
        n = uo.shape[1]

        def body(c, carry):
            r0 = pl.multiple_of(c * L, L)
            y[0, pl.ds(r0, L), :] = (_sigmoid(uo[0, pl.ds(r0, L), :]) * hacc[pl.ds(r0, L), :]).astype(y.dtype)
            return carry

        lax.fori_loop(0, n // L, body, 0)

    finish(uo_l, hacc_l, y_l)
    if ctx_out:
        finish(uo_c, hacc_c, y_c)


def _mlstm(lat, ctx, wconv, bg, ctx_out):
    b, n, _ = lat[0].shape
    nc = ctx[0].shape[1]
    whole = lambda a: pl.BlockSpec((1,) + a.shape[1:], lambda bi: (bi, 0, 0))
    ins = list(lat) + list(ctx)
    out_shape = [jax.ShapeDtypeStruct((b, n, 256), BF16)]
    out_specs = [pl.BlockSpec((1, n, 256), lambda bi: (bi, 0, 0))]
    if ctx_out:
        out_shape.append(jax.ShapeDtypeStruct((b, nc, 256), BF16))
        out_specs.append(pl.BlockSpec((1, nc, 256), lambda bi: (bi, 0, 0)))
    res = pl.pallas_call(
        functools.partial(_mlstm_kernel, ctx_out=ctx_out),
        grid=(b,),
        in_specs=[whole(a) for a in ins] + [_const_spec(wconv.shape), _const_spec(bg.shape)],
        out_specs=out_specs,
        out_shape=out_shape,
        scratch_shapes=[pltpu.VMEM((n, 256), BF16), pltpu.VMEM((n, 256), BF16), pltpu.VMEM((n, 256), F32),
                        pltpu.VMEM((nc, 256), BF16), pltpu.VMEM((nc, 256), BF16), pltpu.VMEM((nc, 256), F32),
                        pltpu.VMEM((2 * ML_HEADS, LANES, LANES), F32),
                        pltpu.VMEM((2 * ML_HEADS, 1, LANES), F32)],
        compiler_params=_cparams(("parallel",)),
        name="mlstm",
    )(*ins, wconv, bg)
    return (res[0], res[1]) if ctx_out else (res[0], None)


def _merge_kernel(x_ref, mod_ref, ya_ref, yb_ref, yc_ref, yd_ref, gpre_ref, gpost_ref, gffn_ref,
                  wg_ref, bgate_ref, wb_ref, wo_ref, x1_ref, h2_ref):
    x = x_ref[0]
    m = mod_ref[0]
    d = x.shape[-1]
    hb = (_rms(x, gpre_ref[...]) * (1.0 + m[1:2]) + m[0:1]).astype(BF16)
    acc = None
    for s, y_ref in enumerate((ya_ref, yb_ref, yc_ref, yd_ref)):
        gate = _sigmoid(_dot(hb, wg_ref[:, s * d:(s + 1) * d]) + bgate_ref[:, s * d:(s + 1) * d])
        term = gate * _dot(y_ref[0], wb_ref[s])
        acc = term if acc is None else acc + term
    y = _dot(acc.astype(BF16), wo_ref[...])
    x1 = x + m[2:3] * _rms(y, gpost_ref[...])
    x1_ref[0] = x1
    h2_ref[0] = (_rms(x1, gffn_ref[...]) * (1.0 + m[4:5]) + m[3:4]).astype(BF16)


def _merge(x, mod, ys, lw, tm):
    b, n, d = x.shape
    tm = min(tm, n)
    row = lambda bi, i: (bi, i, 0)
    return pl.pallas_call(
        _merge_kernel,
        grid=(b, n // tm),
        in_specs=[pl.BlockSpec((1, tm, d), row),
                  pl.BlockSpec((1, 8, d), lambda bi, i: (bi, 0, 0))]
                 + [pl.BlockSpec((1, tm, BRANCH_W), row)] * N_BRANCH
                 + [_const_spec((1, d))] * 3
                 + [_const_spec(lw["wg"].shape), _const_spec(lw["b_gate"].shape),
                    _const_spec(lw["wb"].shape), _const_spec(lw["wo"].shape)],
        out_specs=[pl.BlockSpec((1, tm, d), row), pl.BlockSpec((1, tm, d), row)],
        out_shape=[jax.ShapeDtypeStruct((b, n, d), F32), jax.ShapeDtypeStruct((b, n, d), BF16)],
        compiler_params=_cparams(("parallel", "parallel")),
        name="merge",
    )(x, mod, *ys, lw["g_pre_mix"], lw["g_post_mix"], lw["g_pre_ffn"],
      lw["wg"], lw["b_gate"], lw["wb"], lw["wo"])


FFN_HALO = 16
FFN_COL_CHUNKS = ((0, 1408), (1408, 2816))


def _ffn_kernel(h_ref, hp_ref, hn_ref, x1_ref, mod_ref, gpost_ref, wup_ref, wconv_ref, bconv_ref, wdown_ref,
                o_ref):
    i = pl.program_id(1)
    last = pl.num_programs(1) - 1
    tm = h_ref.shape[1]
    hm = h_ref[0]
    hp = jnp.where(i > 0, hp_ref[0], jnp.zeros_like(hp_ref[0]))
    hn = jnp.where(i < last, hn_ref[0], jnp.zeros_like(hn_ref[0]))
    hext = jnp.concatenate([hp, hm, hn], axis=0)
    ne = tm + 2 * FFN_HALO
    acc = None
    for lo, hi in FFN_COL_CHUNKS:
        a = _dot(hext, wup_ref[:, lo:hi])
        ap = pltpu.roll(a, 1, axis=0)[FFN_HALO:FFN_HALO + tm]
        an = pltpu.roll(a, ne - 1, axis=0)[FFN_HALO:FFN_HALO + tm]
        ac = a[FFN_HALO:FFN_HALO + tm]
        a = (ap * wconv_ref[0:1, lo:hi] + ac * wconv_ref[1:2, lo:hi] + an * wconv_ref[2:3, lo:hi]
             + bconv_ref[:, lo:hi])
        v = _dot(hm, wup_ref[:, D_FF + lo:D_FF + hi])
        act = (a * _sigmoid(a) * v).astype(BF16)
        term = _dot(act, wdown_ref[lo:hi, :])
        acc = term if acc is None else acc + term
    m = mod_ref[0]
    o_ref[0] = x1_ref[0] + m[5:6] * _rms(acc, gpost_ref[...])


def _ffn(h2, x1, mod, lw, tm):
    b, n, d = x1.shape
    tm = min(tm, n)
    row = lambda bi, i: (bi, i, 0)
    r = tm // FFN_HALO
    nblk = n // FFN_HALO
    return pl.pallas_call(
        _ffn_kernel,
        grid=(b, n // tm),
        in_specs=[pl.BlockSpec((1, tm, d), row),
                  pl.BlockSpec((1, FFN_HALO, d), lambda bi, i: (bi, jnp.maximum(i * r - 1, 0), 0)),
                  pl.BlockSpec((1, FFN_HALO, d), lambda bi, i: (bi, jnp.minimum((i + 1) * r, nblk - 1), 0)),
                  pl.BlockSpec((1, tm, d), row),
                  pl.BlockSpec((1, 8, d), lambda bi, i: (bi, 0, 0)),
                  _const_spec((1, d)),
                  _const_spec(lw["wup"].shape), _const_spec(lw["w_ffn_conv"].shape),
                  _const_spec(lw["b_ffn_conv"].shape), _const_spec(lw["wdown"].shape)],
        out_specs=pl.BlockSpec((1, tm, d), row),
        out_shape=jax.ShapeDtypeStruct((b, n, d), F32),
        compiler_params=_cparams(("parallel", "parallel")),
        name="conv_ffn",
    )(h2, h2, h2, x1, mod, lw["g_post_ffn"], lw["wup"], lw["w_ffn_conv"], lw["b_ffn_conv"], lw["wdown"])


def _rope_swap(d):
    half, nf = d // 2, d // 4
    idx = np.arange(d)
    j = idx % half
    return np.where(j < nf, idx + nf, idx - nf)


def _rope_cos_sin(n, d, identity):
    if identity:
        return jnp.ones((n, d), F32), jnp.zeros((n, d), F32)
    half, nf = d // 2, d // 4
    t = jnp.arange(n, dtype=jnp.int32)
    row = (t // GRID_W).astype(F32)[:, None]
    colp = (t % GRID_W).astype(F32)[:, None]
    inv = ROPE_BASE ** (-jnp.arange(nf, dtype=F32) / nf)
    ang = jnp.concatenate([row * inv, row * inv, colp * inv, colp * inv], axis=-1)
    sign = jnp.asarray(np.where(np.arange(d) % half < nf, -1.0, 1.0), F32)
    return jnp.cos(ang), jnp.sin(ang) * sign


def _rope_table(n, identity):
    ca, sa = _rope_cos_sin(n, MLA_ROPE, identity)
    cw, sw = _rope_cos_sin(n, HEAD_DIM, identity)
    scale_a = (MLA_NOPE + MLA_ROPE) ** -0.5
    scale_w = HEAD_DIM ** -0.5
    z32 = jnp.zeros((n, LANES - MLA_NOPE - MLA_ROPE), F32)
    one64 = jnp.ones((n, MLA_NOPE), F32)
    zero64 = jnp.zeros((n, MLA_NOPE), F32)
    qa_cos = jnp.concatenate([one64, ca, z32], -1) * scale_a
    qa_sin = jnp.concatenate([zero64, sa, z32], -1) * scale_a
    ka_cos = jnp.concatenate([zero64, ca, z32], -1)
    ka_sin = jnp.concatenate([zero64, sa, z32], -1)
    cw2 = jnp.concatenate([cw, cw], -1)
    sw2 = jnp.concatenate([sw, sw], -1)
    return jnp.concatenate([qa_cos, qa_sin, ka_cos, ka_sin, cw2 * scale_w, sw2 * scale_w, cw2, sw2], axis=-1)


def _dft_tables(n):
    k = jnp.arange(n, dtype=jnp.int32)
    ang = ((k[:, None] * k[None, :]) % n).astype(F32) * (2.0 * math.pi / n)
    dft = jnp.concatenate([jnp.cos(ang), -jnp.sin(ang)], axis=-1).astype(BF16)
    j = np.arange(FN_GROUP_W)
    a64 = (np.outer(j, j) % FN_GROUP_W) * (2.0 * np.pi / FN_GROUP_W)
    eye = np.eye(FN_GROUPS)
    cs = np.concatenate([np.kron(eye, np.cos(a64)), np.kron(eye, np.sin(a64))], axis=-1)
    return dft, jnp.asarray(cs, F32).astype(BF16)


def _prep_layer(l, p):
    d = D_MODEL
    w_in = p["w_in"][l]
    off = np.cumsum([0, MLA_Q_RANK, MLA_KV_RANK, MLA_ROPE, 256, 256, 256, 256, 16, 256, 128, 128, 256])
    seg = lambda i: w_in[:, int(off[i]):int(off[i + 1])]
    zeros = lambda w: jnp.zeros((d, w), F32)
    sw32 = _rope_swap(MLA_ROPE)
    sw64 = _rope_swap(HEAD_DIM)
    kr = seg(2)
    place_kr = lambda t: jnp.concatenate([zeros(MLA_NOPE), t, zeros(LANES - MLA_NOPE - MLA_ROPE)], -1)
    heads = lambda t, nh: [t[:, HEAD_DIM * i:HEAD_DIM * (i + 1)] for i in range(nh)]
    swap_heads = lambda t, nh: jnp.concatenate([hh[:, sw64] for hh in heads(t, nh)], -1)
    dup = lambda t: jnp.concatenate([hh for hh in heads(t, WG_KV_HEADS) for _ in range(2)], -1)
    wq, wk, wv = seg(8), seg(9), seg(10)
    w1 = jnp.concatenate([
        seg(0), seg(1), place_kr(kr), place_kr(kr[:, sw32]),
        seg(3), seg(4), seg(5), seg(6), jnp.concatenate([seg(7), zeros(LANES - 16)], -1),
        wq, swap_heads(wq, WG_HEADS), dup(wk), dup(swap_heads(wk, WG_KV_HEADS)), dup(wv), seg(11)], axis=-1)
    assert w1.shape[1] == C_END
    dq = MLA_NOPE + MLA_ROPE
    w_uq = p["w_uq"][l]
    zq = jnp.zeros((MLA_Q_RANK, LANES - dq), F32)
    qa, qb = [], []
    for hh in range(MLA_HEADS):
        wh = w_uq[:, dq * hh:dq * (hh + 1)]
        qa += [wh, zq]
        qb += [jnp.zeros((MLA_Q_RANK, MLA_NOPE), F32), wh[:, MLA_NOPE:][:, sw32], zq]
    w_ukv = p["w_ukv"][l]
    dkv = MLA_NOPE + MLA_V
    zk = jnp.zeros((MLA_KV_RANK, LANES - MLA_NOPE), F32)
    kn, vv = [], []
    for hh in range(MLA_HEADS):
        wh = w_ukv[:, dkv * hh:dkv * (hh + 1)]
        kn += [wh[:, :MLA_NOPE], zk]
        vv += [wh[:, MLA_NOPE:]]
    bg = jnp.concatenate([p["b_ml_gates"][l], jnp.zeros((LANES - 16,), F32)]).reshape(1, LANES)
    return {
        "w1": w1.astype(BF16),
        "wq": jnp.concatenate(qa + qb, -1).astype(BF16),
        "wkv": jnp.concatenate(kn + vv, -1).astype(BF16),
        "g_qa": p["g_qa"][l].reshape(1, -1), "g_kva": p["g_kva"][l].reshape(1, -1),
        "g_pre_mix": p["g_pre_mix"][l].reshape(1, d), "g_post_mix": p["g_post_mix"][l].reshape(1, d),
        "g_pre_ffn": p["g_pre_ffn"][l].reshape(1, d), "g_post_ffn": p["g_post_ffn"][l].reshape(1, d),
        "w_ml_conv": p["w_ml_conv"][l], "bg": bg, "wg_sink": p["wg_sink"][l],
        "wg": jnp.concatenate([p["w_gate"][l, s] for s in range(N_BRANCH)], -1).astype(BF16),
        "b_gate": p["b_gate"][l].reshape(1, N_BRANCH * d),
        "wb": p["w_branch"][l].astype(BF16), "wo": p["w_out"][l].astype(BF16),
        "wup": p["w_up"][l].astype(BF16), "w_ffn_conv": p["w_ffn_conv"][l],
        "b_ffn_conv": p["b_ffn_conv"][l].reshape(1, D_FF), "wdown": p["w_down"][l].astype(BF16),
    }


TM = 512


def kernel(x, c, ctx, c_ctx, w_mod, b_mod, g_pre_mix, g_post_mix, g_pre_ffn, g_post_ffn, w_in, g_qa, w_uq, g_kva,
           w_ukv, w_ml_conv, b_ml_gates, wg_sink, w_gate, b_gate, w_branch, w_out, w_up, w_ffn_conv, b_ffn_conv,
           w_down):
    p = dict(g_pre_mix=g_pre_mix, g_post_mix=g_post_mix, g_pre_ffn=g_pre_ffn, g_post_ffn=g_post_ffn, w_in=w_in,
             g_qa=g_qa, w_uq=w_uq, g_kva=g_kva, w_ukv=w_ukv, w_ml_conv=w_ml_conv, b_ml_gates=b_ml_gates,
             wg_sink=wg_sink, w_gate=w_gate, b_gate=b_gate, w_branch=w_branch, w_out=w_out, w_up=w_up,
             w_ffn_conv=w_ffn_conv, b_ffn_conv=b_ffn_conv, w_down=w_down)
    bsz, n, d = x.shape
    n_ctx = ctx.shape[1]
    depth = w_mod.shape[0]
    rows = -(-(bsz + 1) // 8) * 8
    c_all = jnp.concatenate([c, c_ctx[None, :], jnp.zeros((rows - bsz - 1, d), F32)], axis=0)
    mod_all = _modulation(c_all, w_mod, b_mod)
    tab_l = _rope_table(n, identity=False)
    tab_c = _rope_table(n_ctx, identity=True)
    dft_l, cs64 = _dft_tables(n)
    dft_c, _ = _dft_tables(n_ctx)
    xc = ctx
    for l in range(depth):
        ctx_out = l < depth - 1
        lw = _prep_layer(l, p)
        pad2 = jnp.zeros((bsz, 2, d), F32)
        mod_l = jnp.concatenate([mod_all[l, :bsz].reshape(bsz, 6, d), pad2], axis=1)
        mod_c = jnp.concatenate([jnp.broadcast_to(mod_all[l, bsz].reshape(1, 6, d), (bsz, 6, d)), pad2], axis=1)
        qa, ka, va, uq, uk, uv, uo, gt, qw, kw, vw, uf = _inproj(x, mod_l, lw, tab_l, TM)
        qa_c, ka_c, va_c, uq_c, uk_c, uv_c, uo_c, gt_c, qw_c, kw_c, vw_c, uf_c = _inproj(xc, mod_c, lw, tab_c, TM)
        ya = _mla_attend(qa, ka_c, va_c, ka, va)
        yb, yb_c = _mlstm((uq, uk, uv, uo, gt), (uq_c, uk_c, uv_c, uo_c, gt_c), lw["w_ml_conv"], lw["bg"], ctx_out)
        yc = _win_attend(lw["wg_sink"], qw, kw_c, vw_c, kw, vw)
        yd = _fourier(uf, cs64, dft_l)
        x1, h2 = _merge(x, mod_l, (ya, yb, yc, yd), lw, TM)
        x = _ffn(h2, x1, mod_l, lw, TM)
        if ctx_out:
            ya_c = _mla_attend(qa_c, ka_c, va_c)
            yc_c = _win_attend(lw["wg_sink"], qw_c, kw_c, vw_c)
            yd_c = _fourier(uf_c, cs64, dft_c)
            xc1, hc2 = _merge(xc, mod_c, (ya_c, yb_c, yc_c, yd_c), lw, TM)
            xc = _ffn(hc2, xc1, mod_c, lw, TM)
    return x
```

```python
import functools
import math

import numpy as np
import jax
import jax.numpy as jnp
from jax import lax
from jax.experimental import pallas as pl
from jax.experimental.pallas import tpu as pltpu

F32 = jnp.float32
BF16 = jnp.bfloat16

D_MODEL = 1024
GRID_W = 64
N_BRANCH = 4
BRANCH_W = 256
MLA_HEADS = 4
MLA_NOPE = 64
MLA_ROPE = 32
MLA_V = 64
MLA_Q_RANK = 256
MLA_KV_RANK = 256
ML_HEADS = 4
ML_HEAD_DIM = 64
ML_CHUNK = 128
WG_HEADS = 4
WG_KV_HEADS = 2
HEAD_DIM = 64
WINDOW = 128
FN_GROUPS = 4
FN_GROUP_W = 64
D_FF = 2816
ROPE_BASE = 10000.0
EPS = 1e-6
IN_SPLITS = (MLA_Q_RANK, MLA_KV_RANK, MLA_ROPE, 256, 256, 256, 256, 16, 256, 128, 128, 256)

LANES = 128
NEG = -1e30
LOG2E = 1.4426950408889634
VMEM_LIMIT = 56 * 1024 * 1024

C_CQ, C_CKV, C_KRA, C_KRB, C_UQ, C_UK, C_UV, C_UO, C_GT, C_WQA, C_WQB, C_WKA, C_WKB, C_WV, C_UF, C_END = (
    0, 256, 512, 640, 768, 1024, 1280, 1536, 1792, 1920, 2176, 2432, 2688, 2944, 3200, 3456)
T_QA_COS, T_QA_SIN, T_KA_COS, T_KA_SIN, T_QW_COS, T_QW_SIN, T_KW_COS, T_KW_SIN = range(8)


def _cparams(sem):
    return pltpu.CompilerParams(dimension_semantics=sem, vmem_limit_bytes=VMEM_LIMIT)


def _layer_spec(arr, l):
    nd = arr.ndim - 1
    return pl.BlockSpec((1,) + arr.shape[1:], lambda *_: (l,) + (0,) * nd, pipeline_mode=pl.Buffered(1))


def _const_spec(shape):
    nd = len(shape)
    return pl.BlockSpec(shape, lambda *_: (0,) * nd, pipeline_mode=pl.Buffered(1))


def _mod_spec(l, ctx_row):
    if ctx_row is None:
        return pl.BlockSpec((1, 1, 6, D_MODEL), lambda bi, *_: (l, bi, 0, 0))
    return pl.BlockSpec((1, 1, 6, D_MODEL), lambda *_: (l, ctx_row, 0, 0))


def _rms(x, g):
    ms = jnp.mean(x * x, axis=-1, keepdims=True)
    return x * lax.rsqrt(ms + EPS) * g


def _dot(a, b):
    return jnp.dot(a, b, preferred_element_type=F32)


def _dot_nt(a, b):
    return lax.dot_general(a, b, (((1,), (1,)), ((), ())), preferred_element_type=F32)


def _dot_tn(a, b):
    return lax.dot_general(a, b, (((0,), (0,)), ((), ())), preferred_element_type=F32)


def _sigmoid(x):
    return 1.0 / (1.0 + jnp.exp(-x))


def _mod_kernel(c_ref, w_ref, b_ref, o_ref):
    c = c_ref[...]
    s = (c * _sigmoid(c)).astype(BF16)
    o_ref[0] = _dot(s, w_ref[0].astype(BF16)) + b_ref[0]


def _modulation(c_all, w_mod, b_mod):
    depth, d, n = w_mod.shape
    rows = c_all.shape[0]
    tn = 1536
    return pl.pallas_call(
        _mod_kernel,
        grid=(depth, n // tn),
        in_specs=[pl.BlockSpec((rows, d), lambda l, j: (0, 0)),
                  pl.BlockSpec((1, d, tn), lambda l, j: (l, 0, j)),
                  pl.BlockSpec((1, 1, tn), lambda l, j: (l, 0, j))],
        out_specs=pl.BlockSpec((1, rows, tn), lambda l, j: (l, 0, j)),
        out_shape=jax.ShapeDtypeStruct((depth, rows, n), F32),
        compiler_params=_cparams(("parallel", "parallel")),
        name="modulation",
    )(c_all, w_mod, b_mod.reshape(depth, 1, n))


def _inproj_kernel(x_ref, mod_ref, gpre_ref, w1_ref, gqa_ref, wq_ref, gkva_ref, wkv_ref, tab_ref,
                   qa_ref, ka_ref, va_ref, uq_ref, uk_ref, uv_ref, uo_ref, gt_ref,
                   qw_ref, kw_ref, vw_ref, uf_ref):
    x = x_ref[0]
    m = mod_ref[0, 0]
    h = _rms(x, gpre_ref[0]) * (1.0 + m[1:2]) + m[0:1]
    hb = h.astype(BF16)

    def proj(lo, hi):
        return _dot(hb, w1_ref[0, :, lo:hi])

    def tab(j):
        return tab_ref[:, LANES * j:LANES * (j + 1)]

    def rep(t, n):
        return jnp.concatenate([t] * n, axis=-1)

    cq = _rms(proj(C_CQ, C_CKV), gqa_ref[0]).astype(BF16)
    q2 = _dot(cq, wq_ref[0])
    q = q2[:, :512] * rep(tab(T_QA_COS), 4) + q2[:, 512:] * rep(tab(T_QA_SIN), 4)
    qa_ref[0] = q.astype(BF16)
    ckv = _rms(proj(C_CKV, C_KRA), gkva_ref[0]).astype(BF16)
    kv = _dot(ckv, wkv_ref[0])
    kr = proj(C_KRA, C_KRB) * tab(T_KA_COS) + proj(C_KRB, C_UQ) * tab(T_KA_SIN)
    ka_ref[0] = (kv[:, :512] + rep(kr, 4)).astype(BF16)
    va_ref[0] = kv[:, 512:].astype(BF16)
    uq_ref[0] = proj(C_UQ, C_UK)
    uk_ref[0] = proj(C_UK, C_UV)
    uv_ref[0] = proj(C_UV, C_UO).astype(BF16)
    uo_ref[0] = proj(C_UO, C_GT)
    gt_ref[0] = proj(C_GT, C_WQA)
    qw = proj(C_WQA, C_WQB) * rep(tab(T_QW_COS), 2) + proj(C_WQB, C_WKA) * rep(tab(T_QW_SIN), 2)
    qw_ref[0] = qw.astype(BF16)
    kw = proj(C_WKA, C_WKB) * rep(tab(T_KW_COS), 2) + proj(C_WKB, C_WV) * rep(tab(T_KW_SIN), 2)
    kw_ref[0] = kw.astype(BF16)
    vw_ref[0] = proj(C_WV, C_UF).astype(BF16)
    uf_ref[0] = proj(C_UF, C_END).astype(BF16)


def _inproj(x, mod, ctx_row, pw, l, tab, tm):
    b, n, d = x.shape
    tm = min(tm, n)
    row = lambda bi, i: (bi, i, 0)

    def out(w, dt):
        return jax.ShapeDtypeStruct((b, n, w), dt), pl.BlockSpec((1, tm, w), row)

    outs = [out(512, BF16), out(512, BF16), out(256, BF16),
            out(256, F32), out(256, F32), out(256, BF16), out(256, F32), out(128, F32),
            out(256, BF16), out(256, BF16), out(256, BF16), out(256, BF16)]
    return pl.pallas_call(
        _inproj_kernel,
        grid=(b, n // tm),
        in_specs=[pl.BlockSpec((1, tm, d), row),
                  _mod_spec(l, ctx_row),
                  _layer_spec(pw["g_pre_mix"], l),
                  _layer_spec(pw["w1"], l),
                  _layer_spec(pw["g_qa"], l),
                  _layer_spec(pw["wq"], l),
                  _layer_spec(pw["g_kva"], l),
                  _layer_spec(pw["wkv"], l),
                  pl.BlockSpec((tm, 8 * LANES), lambda bi, i: (i, 0))],
        out_specs=[o[1] for o in outs],
        out_shape=[o[0] for o in outs],
        compiler_params=_cparams(("parallel", "parallel")),
        name="inproj",
    )(x, mod, pw["g_pre_mix"], pw["w1"], pw["g_qa"], pw["wq"], pw["g_kva"], pw["wkv"], tab)


def _mla_kernel(*refs, has_lat):
    if has_lat:
        q_ref, kc_ref, vc_ref, kl_ref, vl_ref, o_ref, kcat, vaug = refs
    else:
        q_ref, kc_ref, vc_ref, o_ref, kcat, vaug = refs
    nc = kc_ref.shape[1]
    nk = kcat.shape[0]

    @pl.when(pl.program_id(1) == 0)
    def _():
        kcat[0:nc, :] = kc_ref[0]
        if has_lat:
            kcat[nc:nk, :] = kl_ref[0]
        for hh in range(MLA_HEADS):
            vsl = slice(LANES * (hh // 2), LANES * (hh // 2 + 1))

            def keep(v):
                lane = lax.broadcasted_iota(jnp.int32, v.shape, 1)
                own = (lane < MLA_V) if hh % 2 == 0 else (lane >= MLA_V)
                return jnp.where(own, v, jnp.zeros_like(v))

            vaug[hh, 0:nc, 0:LANES] = keep(vc_ref[0, :, vsl])
            if has_lat:
                vaug[hh, nc:nk, 0:LANES] = keep(vl_ref[0, :, vsl])
            vaug[hh, :, LANES:2 * LANES] = jnp.ones((nk, LANES), BF16)

    blocks = []
    for pr in range(MLA_HEADS // 2):
        acc = None
        for e in range(2):
            hh = 2 * pr + e
            hsl = slice(LANES * hh, LANES * (hh + 1))
            s = _dot_nt(q_ref[0, :, hsl], kcat[:, hsl])
            mx = jnp.max(s, axis=-1, keepdims=True)
            p = jnp.exp2(s - mx).astype(BF16)
            o2 = _dot(p, vaug[hh])
            o = o2[:, :LANES] / o2[:, LANES:]
            acc = o if acc is None else acc + o
        blocks.append(acc)
    o_ref[0] = jnp.concatenate(blocks, axis=-1).astype(o_ref.dtype)


def _mla_attend(q, kc, vc, kl=None, vl=None, tq=512):
    b, n, _ = q.shape
    tq = min(tq, n)
    has_lat = kl is not None
    nk = kc.shape[1] + (kl.shape[1] if has_lat else 0)
    whole = lambda a: pl.BlockSpec((1,) + a.shape[1:], lambda bi, i: (bi, 0, 0))
    ins = [q, kc, vc] + ([kl, vl] if has_lat else [])
    specs = [pl.BlockSpec((1, tq, 512), lambda bi, i: (bi, i, 0))] + [whole(a) for a in ins[1:]]
    return pl.pallas_call(
        functools.partial(_mla_kernel, has_lat=has_lat),
        grid=(b, n // tq),
        in_specs=specs,
        out_specs=pl.BlockSpec((1, tq, 256), lambda bi, i: (bi, i, 0)),
        out_shape=jax.ShapeDtypeStruct((b, n, 256), BF16),
        scratch_shapes=[pltpu.VMEM((nk, 512), BF16), pltpu.VMEM((MLA_HEADS, nk, 2 * LANES), BF16)],
        compiler_params=_cparams(("parallel", "arbitrary")),
        name="mla_attend",
    )(*ins)


def _win_kernel(*refs, has_lat, n_lat, layer):
    if has_lat:
        sink_ref, q_ref, kc_ref, vc_ref, kl_ref, vl_ref, o_ref = refs
    else:
        sink_ref, q_ref, kc_ref, vc_ref, o_ref = refs
    tq = q_ref.shape[1]
    nc = kc_ref.shape[1]
    wk = tq + 2 * WINDOW
    lane = lax.broadcasted_iota(jnp.int32, (tq, LANES), 1)
    if has_lat:
        t0 = pl.program_id(1) * tq
        start = pl.multiple_of(jnp.clip(t0 - WINDOW, 0, n_lat - wk), LANES)
        qpos = t0 + lax.broadcasted_iota(jnp.int32, (tq, wk), 0)
        kpos = start + lax.broadcasted_iota(jnp.int32, (tq, wk), 1)
        band = jnp.abs(kpos - qpos) <= WINDOW

    def aug(v, e):
        vl_ = lax.broadcasted_iota(jnp.int32, v.shape, 1)
        own = (vl_ < HEAD_DIM) if e == 0 else (vl_ >= HEAD_DIM)
        return jnp.concatenate([jnp.where(own, v, jnp.zeros_like(v)), jnp.ones(v.shape, BF16)], axis=1)

    blocks = []
    for pr in range(WG_KV_HEADS):
        psl = slice(LANES * pr, LANES * (pr + 1))
        qp = q_ref[0, :, psl]
        kc = kc_ref[0, :, psl]
        vc = vc_ref[0, :, psl]
        if has_lat:
            kl = kl_ref[0, pl.ds(start, wk), psl]
            vl = vl_ref[0, pl.ds(start, wk), psl]
        acc = None
        for e in range(2):
            own = (lane < HEAD_DIM) if e == 0 else (lane >= HEAD_DIM)
            q = jnp.where(own, qp, jnp.zeros_like(qp))
            sink = sink_ref[layer, 2 * pr + e] * LOG2E
            s_c = _dot_nt(q, kc)
            mx = jnp.maximum(jnp.max(s_c, axis=-1, keepdims=True), sink)
            if has_lat:
                s_l = jnp.where(band, _dot_nt(q, kl), NEG)
                mx = jnp.maximum(mx, jnp.max(s_l, axis=-1, keepdims=True))
            o2 = _dot(jnp.exp2(s_c - mx).astype(BF16), aug(vc, e))
            if has_lat:
                o2 = o2 + _dot(jnp.exp2(s_l - mx).astype(BF16), aug(vl, e))
            o = o2[:, :LANES] / (o2[:, LANES:] + jnp.exp2(sink - mx))
            acc = o if acc is None else acc + o
        blocks.append(acc)
    o_ref[0] = jnp.concatenate(blocks, axis=-1).astype(o_ref.dtype)


def _win_attend(sink, layer, q, kc, vc, kl=None, vl=None, tq=256):
    b, n, _ = q.shape
    tq = min(tq, n)
    has_lat = kl is not None
    whole = lambda a: pl.BlockSpec((1,) + a.shape[1:], lambda bi, i: (bi, 0, 0))
    ins = [q, kc, vc] + ([kl, vl] if has_lat else [])
    specs = ([pl.BlockSpec(memory_space=pltpu.SMEM),
              pl.BlockSpec((1, tq, 256), lambda bi, i: (bi, i, 0))] + [whole(a) for a in ins[1:]])
    return pl.pallas_call(
        functools.partial(_win_kernel, has_lat=has_lat, n_lat=n, layer=layer),
        grid=(b, n // tq),
        in_specs=specs,
        out_specs=pl.BlockSpec((1, tq, 256), lambda bi, i: (bi, i, 0)),
        out_shape=jax.ShapeDtypeStruct((b, n, 256), BF16),
        compiler_params=_cparams(("parallel", "parallel")),
        name="win_attend",
    )(sink, *ins)


def _fourier_kernel(u_ref, cs_ref, dft_ref, o_ref, ab_ref, *, tr):
    n = u_ref.shape[1]
    u = u_ref[0]
    ab = _dot(u, cs_ref[...])
    ab_ref[0:n, :] = ab[:, :256].astype(BF16)
    ab_ref[n:2 * n, :] = ab[:, 256:].astype(BF16)
    scale = 1.0 / math.sqrt(n * FN_GROUP_W)

    def body(r, carry):
        r0 = pl.multiple_of(r * tr, tr)
        y = _dot(dft_ref[pl.ds(r0, tr), :], ab_ref[...])
        o_ref[0, pl.ds(r0, tr), :] = (y * scale).astype(o_ref.dtype)
        return carry

    lax.fori_loop(0, n // tr, body, 0)


def _fourier(u, cs, dft):
    b, n, w = u.shape
    tr = min(512, n)
    return pl.pallas_call(
        functools.partial(_fourier_kernel, tr=tr),
        grid=(b,),
        in_specs=[pl.BlockSpec((1, n, w), lambda bi: (bi, 0, 0)),
                  _const_spec(cs.shape),
                  _const_spec(dft.shape)],
        out_specs=pl.BlockSpec((1, n, w), lambda bi: (bi, 0, 0)),
        out_shape=jax.ShapeDtypeStruct((b, n, w), BF16),
        scratch_shapes=[pltpu.VMEM((2 * n, w), BF16)],
        compiler_params=_cparams(("parallel",)),
        name="fourier",
    )(u, cs, dft)


def _mlstm_kernel(*refs, ctx_out):
    (uq_l, uk_l, uv_l, uo_l, gt_l, uq_c, uk_c, uv_c, uo_c, gt_c, wconv_ref, bg_ref) = refs[:12]
    if ctx_out:
        y_l, y_c = refs[12:14]
        scratch = refs[14:]
    else:
        y_l = refs[12]
        y_c = None
        scratch = refs[13:]
    qs_l, ks_l, hacc_l, qs_c, ks_c, hacc_c, cst, mst = scratch
    L = ML_CHUNK
    row = lax.broadcasted_iota(jnp.int32, (L, L), 0)
    col = lax.broadcasted_iota(jnp.int32, (L, L), 1)
    tril_f = (col <= row).astype(F32)
    row2 = lax.broadcasted_iota(jnp.int32, (L, 256), 0)
    bg = bg_ref[0]

    def prep(x_ref, out_ref, w3, scale):
        n = x_ref.shape[1]
        nc = n // L

        def body(c, carry):
            r0 = pl.multiple_of(c * L, L)
            xc = x_ref[0, pl.ds(r0, L), :]
            pstart = pl.multiple_of(jnp.maximum(r0 - 8, 0), 8)
            nstart = pl.multiple_of(jnp.minimum(r0 + L, n - 8), 8)
            prow = x_ref[0, pl.ds(pstart, 8), :][7:8, :] * jnp.where(c > 0, 1.0, 0.0)
            nrow = x_ref[0, pl.ds(nstart, 8), :][0:1, :] * jnp.where(c < nc - 1, 1.0, 0.0)
            xp = jnp.where(row2 == 0, prow, pltpu.roll(xc, 1, axis=0))
            xn = jnp.where(row2 == L - 1, nrow, pltpu.roll(xc, L - 1, axis=0))
            a = xp * w3[0:1, :] + xc * w3[1:2, :] + xn * w3[2:3, :]
            out_ref[pl.ds(r0, L), :] = (a * _sigmoid(a) * scale).astype(BF16)
            return carry

        lax.fori_loop(0, nc, body, 0)

    wq3 = wconv_ref[0, :, 0:256]
    wk3 = wconv_ref[0, :, 256:512]
    kscale = ML_HEAD_DIM ** -0.5
    prep(uq_l, qs_l, wq3, 1.0)
    prep(uk_l, ks_l, wk3, kscale)
    prep(uq_c, qs_c, wq3, 1.0)
    prep(uk_c, ks_c, wk3, kscale)
    hacc_l[...] = jnp.zeros_like(hacc_l)
    if ctx_out:
        hacc_c[...] = jnp.zeros_like(hacc_c)
    cst[...] = jnp.zeros_like(cst)
    mst[...] = jnp.zeros_like(mst)

    def chunk_step(qs, ks, uv, gt, hacc, c, d, need_h):
        r0 = pl.multiple_of(c * L, L)
        g = gt[0, pl.ds(r0, L), :] + bg
        ls = jnp.minimum(g, 0.0) - jnp.log(1.0 + jnp.exp(-jnp.abs(g)))
        cum = jnp.dot(tril_f, ls, precision=lax.Precision.HIGHEST, preferred_element_type=F32)
        tot = cum[L - 1:L, :]
        if d == 1:
            cum = tot - cum + ls
        g_t = g.T
        cum_t = cum.T
        lane = col
        qb = [qs[pl.ds(r0, L), LANES * p:LANES * (p + 1)] for p in range(2)]
        kb = [ks[pl.ds(r0, L), LANES * p:LANES * (p + 1)] for p in range(2)]
        vb = [uv[0, pl.ds(r0, L), LANES * p:LANES * (p + 1)] for p in range(2)]
        mask = (col <= row) if d == 0 else (col >= row)
        hs = []
        for hh in range(ML_HEADS):
            idx = d * ML_HEADS + hh
            icol = (0 if d == 0 else 8) + hh
            fcol = icol + 4
            p = hh // 2
            own = (lane < ML_HEAD_DIM) if hh % 2 == 0 else (lane >= ML_HEAD_DIM)
            a_col = cum[:, fcol:fcol + 1]
            li_col = g[:, icol:icol + 1]
            m_st = mst[idx][:, 0:1]
            c_aug = cst[idx]
            kh = jnp.where(own, kb[p], jnp.zeros_like(kb[p]))
            v_aug = jnp.where(own, vb[p], jnp.ones_like(vb[p]))
            if need_h:
                qh = jnp.where(own, qb[p], jnp.zeros_like(qb[p]))
                b_row = g_t[icol:icol + 1, :] - cum_t[fcol:fcol + 1, :]
                logd = jnp.where(mask, a_col + b_row, NEG)
                log_inter = a_col + m_st
                m_t = jnp.maximum(log_inter, jnp.max(logd, axis=-1, keepdims=True))
                w_intra = jnp.exp(logd - m_t)
                w_inter = jnp.exp(log_inter - m_t)
                s = _dot_nt(qh, kh) * w_intra
                r = _dot(s.astype(BF16), v_aug) + w_inter * _dot(qh, c_aug.astype(BF16))
                den = pltpu.roll(r, ML_HEAD_DIM, axis=1)
                hs.append(r / jnp.maximum(jnp.abs(den), jnp.exp(-m_t)))
            c_last = tot[:, fcol:fcol + 1]
            lw = c_last - a_col + li_col
            m_new = jnp.maximum(c_last + m_st, jnp.max(lw, axis=0, keepdims=True))
            kw = (kh.astype(F32) * jnp.exp(lw - m_new)).astype(BF16)
            cst[idx] = jnp.exp(c_last + m_st - m_new) * c_aug + _dot_tn(kw, v_aug)
            mst[idx] = jnp.broadcast_to(m_new, (1, LANES))
        if need_h:
            for p in range(2):
                hp = jnp.where(lane < ML_HEAD_DIM, hs[2 * p], hs[2 * p + 1])
                hacc[pl.ds(r0, L), LANES * p:LANES * (p + 1)] += hp

    def scan(qs, ks, uv, gt, hacc, n, need_h):
        nc = n // L

        def body(j, carry):
            chunk_step(qs, ks, uv, gt, hacc, j, 0, need_h)
            chunk_step(qs, ks, uv, gt, hacc, nc - 1 - j, 1, need_h)
            return carry

        lax.fori_loop(0, nc, body, 0)

    scan(qs_c, ks_c, uv_c, gt_c, hacc_c, uq_c.shape[1], ctx_out)
    scan(qs_l, ks_l, uv_l, gt_l, hacc_l, uq_l.shape[1], True)

    def finish(uo, hacc, y):
        n = uo.shape[1]

        def body(c, carry):
            r0 = pl.multiple_of(c * L, L)
            y[0, pl.ds(r0, L), :] = (_sigmoid(uo[0, pl.ds(r0, L), :]) * hacc[pl.ds(r0, L), :]).astype(y.dtype)
            return carry

        lax.fori_loop(0, n // L, body, 0)

    finish(uo_l, hacc_l, y_l)
    if ctx_out:
        finish(uo_c, hacc_c, y_c)


def _mlstm(lat, ctx, pw, l, ctx_out):
    b, n, _ = lat[0].shape
    nc = ctx[0].shape[1]
    whole = lambda a: pl.BlockSpec((1,) + a.shape[1:], lambda bi: (bi, 0, 0))
    ins = list(lat) + list(ctx)
    out_shape = [jax.ShapeDtypeStruct((b, n, 256), BF16)]
    out_specs = [pl.BlockSpec((1, n, 256), lambda bi: (bi, 0, 0))]
    if ctx_out:
        out_shape.append(jax.ShapeDtypeStruct((b, nc, 256), BF16))
        out_specs.append(pl.BlockSpec((1, nc, 256), lambda bi: (bi, 0, 0)))
    res = pl.pallas_call(
        functools.partial(_mlstm_kernel, ctx_out=ctx_out),
        grid=(b,),
        in_specs=[whole(a) for a in ins] + [_layer_spec(pw["w_ml_conv"], l), _layer_spec(pw["bg"], l)],
        out_specs=out_specs,
        out_shape=out_shape,
        scratch_shapes=[pltpu.VMEM((n, 256), BF16), pltpu.VMEM((n, 256), BF16), pltpu.VMEM((n, 256), F32),
                        pltpu.VMEM((nc, 256), BF16), pltpu.VMEM((nc, 256), BF16), pltpu.VMEM((nc, 256), F32),
                        pltpu.VMEM((2 * ML_HEADS, LANES, LANES), F32),
                        pltpu.VMEM((2 * ML_HEADS, 1, LANES), F32)],
        compiler_params=_cparams(("parallel",)),
        name="mlstm",
    )(*ins, pw["w_ml_conv"], pw["bg"])
    return (res[0], res[1]) if ctx_out else (res[0], None)


def _merge_kernel(x_ref, mod_ref, ya_ref, yb_ref, yc_ref, yd_ref, gpre_ref, gpost_ref, gffn_ref,
                  wg_ref, bgate_ref, wb_ref, wo_ref, x1_ref, h2_ref):
    x = x_ref[0]
    m = mod_ref[0, 0]
    hb = (_rms(x, gpre_ref[0]) * (1.0 + m[1:2]) + m[0:1]).astype(BF16)
    acc = None
    for s, y_ref in enumerate((ya_ref, yb_ref, yc_ref, yd_ref)):
        gate = _sigmoid(_dot(hb, wg_ref[0, s]) + bgate_ref[0, s:s + 1, :])
        term = gate * _dot(y_ref[0], wb_ref[0, s])
        acc = term if acc is None else acc + term
    y = _dot(acc.astype(BF16), wo_ref[0])
    x1 = x + m[2:3] * _rms(y, gpost_ref[0])
    x1_ref[0] = x1
    h2_ref[0] = (_rms(x1, gffn_ref[0]) * (1.0 + m[4:5]) + m[3:4]).astype(BF16)


def _merge(x, mod, ctx_row, ys, pw, l, tm):
    b, n, d = x.shape
    tm = min(tm, n)
    row = lambda bi, i: (bi, i, 0)
    names = ("g_pre_mix", "g_post_mix", "g_pre_ffn", "wg", "b_gate", "wb", "wo")
    return pl.pallas_call(
        _merge_kernel,
        grid=(b, n // tm),
        in_specs=[pl.BlockSpec((1, tm, d), row), _mod_spec(l, ctx_row)]
                 + [pl.BlockSpec((1, tm, BRANCH_W), row)] * N_BRANCH
                 + [_layer_spec(pw[k], l) for k in names],
        out_specs=[pl.BlockSpec((1, tm, d), row), pl.BlockSpec((1, tm, d), row)],
        out_shape=[jax.ShapeDtypeStruct((b, n, d), F32), jax.ShapeDtypeStruct((b, n, d), BF16)],
        compiler_params=_cparams(("parallel", "parallel")),
        name="merge",
    )(x, mod, *ys, *[pw[k] for k in names])


FFN_HALO = 16
FFN_COL_CHUNKS = ((0, 1536), (1536, 2816))


def _ffn_kernel(h_ref, hp_ref, hn_ref, x1_ref, mod_ref, gpost_ref, wup_ref, wconv_ref, bconv_ref, wdown_ref,
                o_ref):
    i = pl.program_id(1)
    last = pl.num_programs(1) - 1
    tm = h_ref.shape[1]
    hm = h_ref[0]
    hp = jnp.where(i > 0, hp_ref[0], jnp.zeros_like(hp_ref[0]))
    hn = jnp.where(i < last, hn_ref[0], jnp.zeros_like(hn_ref[0]))
    hext = jnp.concatenate([hp, hm, hn], axis=0)
    ne = tm + 2 * FFN_HALO
    acc = None
    for lo, hi in FFN_COL_CHUNKS:
        a = _dot(hext, wup_ref[0, :, lo:hi])
        ap = pltpu.roll(a, 1, axis=0)[FFN_HALO:FFN_HALO + tm]
        an = pltpu.roll(a, ne - 1, axis=0)[FFN_HALO:FFN_HALO + tm]
        ac = a[FFN_HALO:FFN_HALO + tm]
        a = (ap * wconv_ref[0, 0:1, lo:hi] + ac * wconv_ref[0, 1:2, lo:hi] + an * wconv_ref[0, 2:3, lo:hi]
             + bconv_ref[0, :, lo:hi])
        v = _dot(hm, wup_ref[0, :, D_FF + lo:D_FF + hi])
        act = (a * _sigmoid(a) * v).astype(BF16)
        term = _dot(act, wdown_ref[0, lo:hi, :])
        acc = term if acc is None else acc + term
    m = mod_ref[0, 0]
    o_ref[0] = x1_ref[0] + m[5:6] * _rms(acc, gpost_ref[0])


def _ffn(h2, x1, mod, ctx_row, pw, l, tm):
    b, n, d = x1.shape
    tm = min(tm, n)
    row = lambda bi, i: (bi, i, 0)
    r = tm // FFN_HALO
    nblk = n // FFN_HALO
    names = ("g_post_ffn", "wup", "w_ffn_conv", "b_ffn_conv", "wdown")
    return pl.pallas_call(
        _ffn_kernel,
        grid=(b, n // tm),
        in_specs=[pl.BlockSpec((1, tm, d), row),
                  pl.BlockSpec((1, FFN_HALO, d), lambda bi, i: (bi, jnp.maximum(i * r - 1, 0), 0)),
                  pl.BlockSpec((1, FFN_HALO, d), lambda bi, i: (bi, jnp.minimum((i + 1) * r, nblk - 1), 0)),
                  pl.BlockSpec((1, tm, d), row),
                  _mod_spec(l, ctx_row)]
                 + [_layer_spec(pw[k], l) for k in names],
        out_specs=pl.BlockSpec((1, tm, d), row),
        out_shape=jax.ShapeDtypeStruct((b, n, d), F32),
        compiler_params=_cparams(("parallel", "parallel")),
        name="conv_ffn",
    )(h2, h2, h2, x1, mod, *[pw[k] for k in names])


def _rope_swap(d):
    half, nf = d // 2, d // 4
    idx = np.arange(d)
    j = idx % half
    return np.where(j < nf, idx + nf, idx - nf)


def _rope_cos_sin(n, d, identity):
    if identity:
        return jnp.ones((n, d), F32), jnp.zeros((n, d), F32)
    half, nf = d // 2, d // 4
    t = jnp.arange(n, dtype=jnp.int32)
    row = (t // GRID_W).astype(F32)[:, None]
    colp = (t % GRID_W).astype(F32)[:, None]
    inv = ROPE_BASE ** (-jnp.arange(nf, dtype=F32) / nf)
    ang = jnp.concatenate([row * inv, row * inv, colp * inv, colp * inv], axis=-1)
    sign = jnp.asarray(np.where(np.arange(d) % half < nf, -1.0, 1.0), F32)
    return jnp.cos(ang), jnp.sin(ang) * sign


def _rope_table(n, identity):
    ca, sa = _rope_cos_sin(n, MLA_ROPE, identity)
    cw, sw = _rope_cos_sin(n, HEAD_DIM, identity)
    scale_a = (MLA_NOPE + MLA_ROPE) ** -0.5 * LOG2E
    scale_w = HEAD_DIM ** -0.5 * LOG2E
    z32 = jnp.zeros((n, LANES - MLA_NOPE - MLA_ROPE), F32)
    one64 = jnp.ones((n, MLA_NOPE), F32)
    zero64 = jnp.zeros((n, MLA_NOPE), F32)
    qa_cos = jnp.concatenate([one64, ca, z32], -1) * scale_a
    qa_sin = jnp.concatenate([zero64, sa, z32], -1) * scale_a
    ka_cos = jnp.concatenate([zero64, ca, z32], -1)
    ka_sin = jnp.concatenate([zero64, sa, z32], -1)
    cw2 = jnp.concatenate([cw, cw], -1)
    sw2 = jnp.concatenate([sw, sw], -1)
    return jnp.concatenate([qa_cos, qa_sin, ka_cos, ka_sin, cw2 * scale_w, sw2 * scale_w, cw2, sw2], axis=-1)


def _dft_tables(n):
    k = jnp.arange(n, dtype=jnp.int32)
    ang = ((k[:, None] * k[None, :]) % n).astype(F32) * (2.0 * math.pi / n)
    dft = jnp.concatenate([jnp.cos(ang), -jnp.sin(ang)], axis=-1).astype(BF16)
    j = np.arange(FN_GROUP_W)
    a64 = (np.outer(j, j) % FN_GROUP_W) * (2.0 * np.pi / FN_GROUP_W)
    eye = np.eye(FN_GROUPS)
    cs = np.concatenate([np.kron(eye, np.cos(a64)), np.kron(eye, np.sin(a64))], axis=-1)
    return dft, jnp.asarray(cs, F32).astype(BF16)


def _w1_index():
    off = np.concatenate([[0], np.cumsum(IN_SPLITS)])
    z = int(off[-1])
    seg = lambda i: np.arange(off[i], off[i + 1])
    zeros = lambda w: np.full((w,), z)
    sw32, sw64 = _rope_swap(MLA_ROPE), _rope_swap(HEAD_DIM)
    place_kr = lambda t: np.concatenate([zeros(MLA_NOPE), t, zeros(LANES - MLA_NOPE - MLA_ROPE)])
    head = lambda i, h: off[i] + HEAD_DIM * h + np.arange(HEAD_DIM)
    dup = lambda i, perm: np.concatenate([head(i, h)[perm] for h in (0, 0, 1, 1)])
    ident = np.arange(HEAD_DIM)
    idx = np.concatenate([
        seg(0), seg(1), place_kr(seg(2)), place_kr(seg(2)[sw32]),
        seg(3), seg(4), seg(5), seg(6), seg(7), zeros(LANES - 16),
        seg(8), np.concatenate([head(8, h)[sw64] for h in range(WG_HEADS)]),
        dup(9, ident), dup(9, sw64), dup(10, ident), seg(11)])
    assert idx.shape == (C_END,)
    return idx.astype(np.int32)


def _wq_index():
    dq = MLA_NOPE + MLA_ROPE
    z = MLA_HEADS * dq
    sw32 = _rope_swap(MLA_ROPE)
    pad = np.full((LANES - dq,), z)
    qa = [np.concatenate([dq * h + np.arange(dq), pad]) for h in range(MLA_HEADS)]
    qb = [np.concatenate([np.full((MLA_NOPE,), z), dq * h + MLA_NOPE + sw32, pad]) for h in range(MLA_HEADS)]
    return np.concatenate(qa + qb).astype(np.int32)


def _wkv_index():
    dkv = MLA_NOPE + MLA_V
    z = MLA_HEADS * dkv
    kn = [np.concatenate([dkv * h + np.arange(MLA_NOPE), np.full((LANES - MLA_NOPE,), z)]) for h in range(MLA_HEADS)]
    vv = [dkv * h + MLA_NOPE + np.arange(MLA_V) for h in range(MLA_HEADS)]
    return np.concatenate(kn + vv).astype(np.int32)


def _take_cols(w, idx):
    zcol = jnp.zeros(w.shape[:-1] + (1,), w.dtype)
    return jnp.take(jnp.concatenate([w, zcol], axis=-1), jnp.asarray(idx), axis=-1).astype(BF16)


def _prep_weights(p):
    depth = p["w_in"].shape[0]
    vec = lambda a: a.reshape(depth, 1, a.shape[-1])
    bg = jnp.concatenate([p["b_ml_gates"], jnp.zeros((depth, LANES - 16), F32)], axis=-1)
    return {
        "w1": _take_cols(p["w_in"], _w1_index()),
        "wq": _take_cols(p["w_uq"], _wq_index()),
        "wkv": _take_cols(p["w_ukv"], _wkv_index()),
        "g_qa": vec(p["g_qa"]), "g_kva": vec(p["g_kva"]),
        "g_pre_mix": vec(p["g_pre_mix"]), "g_post_mix": vec(p["g_post_mix"]),
        "g_pre_ffn": vec(p["g_pre_ffn"]), "g_post_ffn": vec(p["g_post_ffn"]),
        "w_ml_conv": p["w_ml_conv"], "bg": vec(bg), "wg_sink": p["wg_sink"],
        "wg": p["w_gate"].astype(BF16), "b_gate": p["b_gate"],
        "wb": p["w_branch"].astype(BF16), "wo": p["w_out"].astype(BF16),
        "wup": p["w_up"].astype(BF16), "w_ffn_conv": p["w_ffn_conv"],
        "b_ffn_conv": vec(p["b_ffn_conv"]), "wdown": p["w_down"].astype(BF16),
    }


TM = 512


def kernel(x, c, ctx, c_ctx, w_mod, b_mod, g_pre_mix, g_post_mix, g_pre_ffn, g_post_ffn, w_in, g_qa, w_uq, g_kva,
           w_ukv, w_ml_conv, b_ml_gates, wg_sink, w_gate, b_gate, w_branch, w_out, w_up, w_ffn_conv, b_ffn_conv,
           w_down):
    p = dict(g_pre_mix=g_pre_mix, g_post_mix=g_post_mix, g_pre_ffn=g_pre_ffn, g_post_ffn=g_post_ffn, w_in=w_in,
             g_qa=g_qa, w_uq=w_uq, g_kva=g_kva, w_ukv=w_ukv, w_ml_conv=w_ml_conv, b_ml_gates=b_ml_gates,
             wg_sink=wg_sink, w_gate=w_gate, b_gate=b_gate, w_branch=w_branch, w_out=w_out, w_up=w_up,
             w_ffn_conv=w_ffn_conv, b_ffn_conv=b_ffn_conv, w_down=w_down)
    bsz, n, d = x.shape
    n_ctx = ctx.shape[1]
    depth = w_mod.shape[0]
    rows = -(-(bsz + 1) // 8) * 8
    c_all = jnp.concatenate([c, c_ctx[None, :], jnp.zeros((rows - bsz - 1, d), F32)], axis=0)
    mod = _modulation(c_all, w_mod, b_mod).reshape(depth, rows, 6, d)
    pw = _prep_weights(p)
    tab_l = _rope_table(n, identity=False)
    tab_c = _rope_table(n_ctx, identity=True)
    dft_l, cs64 = _dft_tables(n)
    dft_c, _ = _dft_tables(n_ctx)
    xc = ctx
    for l in range(depth):
        ctx_out = l < depth - 1
        qa, ka, va, uq, uk, uv, uo, gt, qw, kw, vw, uf = _inproj(x, mod, None, pw, l, tab_l, TM)
        qa_c, ka_c, va_c, uq_c, uk_c, uv_c, uo_c, gt_c, qw_c, kw_c, vw_c, uf_c = _inproj(
            xc, mod, bsz, pw, l, tab_c, TM)
        ya = _mla_attend(qa, ka_c, va_c, ka, va)
        yb, yb_c = _mlstm((uq, uk, uv, uo, gt), (uq_c, uk_c, uv_c, uo_c, gt_c), pw, l, ctx_out)
        yc = _win_attend(pw["wg_sink"], l, qw, kw_c, vw_c, kw, vw)
        yd = _fourier(uf, cs64, dft_l)
        x1, h2 = _merge(x, mod, None, (ya, yb, yc, yd), pw, l, TM)
        x = _ffn(h2, x1, mod, None, pw, l, TM)
        if ctx_out:
            ya_c = _mla_attend(qa_c, ka_c, va_c)
            yc_c = _win_attend(pw["wg_sink"], l, qw_c, kw_c, vw_c)
            yd_c = _fourier(uf_c, cs64, dft_c)
            xc1, hc2 = _merge(xc, mod, bsz, (ya_c, yb_c, yc_c, yd_c), pw, l, TM)
            xc = _ffn(hc2, xc1, mod, bsz, pw, l, TM)
    return x
```

```python
import functools
import math

import numpy as np
import jax
import jax.numpy as jnp
from jax import lax
from jax.experimental import pallas as pl
from jax.experimental.pallas import tpu as pltpu

F32 = jnp.float32
BF16 = jnp.bfloat16

D_MODEL = 1024
GRID_W = 64
N_BRANCH = 4
BRANCH_W = 256
MLA_HEADS = 4
MLA_NOPE = 64
MLA_ROPE = 32
MLA_V = 64
MLA_Q_RANK = 256
MLA_KV_RANK = 256
ML_HEADS = 4
ML_HEAD_DIM = 64
ML_CHUNK = 128
WG_HEADS = 4
WG_KV_HEADS = 2
HEAD_DIM = 64
WINDOW = 128
FN_GROUPS = 4
FN_GROUP_W = 64
D_FF = 2816
ROPE_BASE = 10000.0
EPS = 1e-6
IN_SPLITS = (MLA_Q_RANK, MLA_KV_RANK, MLA_ROPE, 256, 256, 256, 256, 16, 256, 128, 128, 256)

LANES = 128
NEG = -1e30
LOG2E = 1.4426950408889634
VMEM_LIMIT = 56 * 1024 * 1024

C_CQ, C_CKV, C_KRA, C_KRB, C_UQ, C_UK, C_WQA, C_WQB, C_WKA, C_WKB, C_WV, C_UF, C_END = (
    0, 256, 512, 640, 768, 1024, 1280, 1536, 1792, 2048, 2304, 2560, 2816)
R_UV, R_UO, R_GT, R_END = 0, 256, 512, 544
GT_ROWS = 32
T_QA_COS, T_QA_SIN, T_KA_COS, T_KA_SIN, T_QW_COS, T_QW_SIN, T_KW_COS, T_KW_SIN = range(8)


def _cparams(sem):
    return pltpu.CompilerParams(dimension_semantics=sem, vmem_limit_bytes=VMEM_LIMIT)


def _layer_spec(arr, l):
    nd = arr.ndim - 1
    return pl.BlockSpec((1,) + arr.shape[1:], lambda *_: (l,) + (0,) * nd, pipeline_mode=pl.Buffered(1))


def _const_spec(shape):
    nd = len(shape)
    return pl.BlockSpec(shape, lambda *_: (0,) * nd, pipeline_mode=pl.Buffered(1))


def _mod_spec(l, ctx_row):
    if ctx_row is None:
        return pl.BlockSpec((1, 1, 6, D_MODEL), lambda bi, *_: (l, bi, 0, 0))
    return pl.BlockSpec((1, 1, 6, D_MODEL), lambda *_: (l, ctx_row, 0, 0))


def _rms(x, g):
    ms = jnp.mean(x * x, axis=-1, keepdims=True)
    return x * lax.rsqrt(ms + EPS) * g


def _dot(a, b):
    return jnp.dot(a, b, preferred_element_type=F32)


def _dot_nt(a, b):
    return lax.dot_general(a, b, (((1,), (1,)), ((), ())), preferred_element_type=F32)


def _dot_tn(a, b):
    return lax.dot_general(a, b, (((0,), (0,)), ((), ())), preferred_element_type=F32)


def _sigmoid(x):
    return 1.0 / (1.0 + jnp.exp(-x))


def _mod_kernel(c_ref, w_ref, b_ref, o_ref):
    c = c_ref[...]
    s = (c * _sigmoid(c)).astype(BF16)
    o_ref[0] = _dot(s, w_ref[0].astype(BF16)) + b_ref[0]


def _modulation(c_all, w_mod, b_mod):
    depth, d, n = w_mod.shape
    rows = c_all.shape[0]
    tn = 1536
    return pl.pallas_call(
        _mod_kernel,
        grid=(depth, n // tn),
        in_specs=[pl.BlockSpec((rows, d), lambda l, j: (0, 0)),
                  pl.BlockSpec((1, d, tn), lambda l, j: (l, 0, j)),
                  pl.BlockSpec((1, 1, tn), lambda l, j: (l, 0, j))],
        out_specs=pl.BlockSpec((1, rows, tn), lambda l, j: (l, 0, j)),
        out_shape=jax.ShapeDtypeStruct((depth, rows, n), F32),
        compiler_params=_cparams(("parallel", "parallel")),
        name="modulation",
    )(c_all, w_mod, b_mod.reshape(depth, 1, n))


def _inproj_kernel(x_ref, mod_ref, gpre_ref, w1_ref, w1t_ref, gqa_ref, wq_ref, gkva_ref, wkv_ref, tab_ref,
                   qa_ref, ka_ref, va_ref, uq_ref, uk_ref, vt_ref, ot_ref, gt_ref,
                   qw_ref, kw_ref, vw_ref, uf_ref):
    x = x_ref[0]
    m = mod_ref[0, 0]
    h = _rms(x, gpre_ref[0]) * (1.0 + m[1:2]) + m[0:1]
    hb = h.astype(BF16)

    def proj(lo, hi):
        return _dot(hb, w1_ref[0, :, lo:hi])

    def proj_t(lo, hi, out_ref):
        r = _dot_nt(w1t_ref[0, lo:hi, :], hb)
        for j in range(out_ref.shape[1]):
            out_ref[0, j] = r[:, LANES * j:LANES * (j + 1)].astype(out_ref.dtype)

    def tab(j):
        return tab_ref[:, LANES * j:LANES * (j + 1)]

    def rep(t, n):
        return jnp.concatenate([t] * n, axis=-1)

    cq = _rms(proj(C_CQ, C_CKV), gqa_ref[0]).astype(BF16)
    q2 = _dot(cq, wq_ref[0])
    q = q2[:, :512] * rep(tab(T_QA_COS), 4) + q2[:, 512:] * rep(tab(T_QA_SIN), 4)
    qa_ref[0] = q.astype(BF16)
    ckv = _rms(proj(C_CKV, C_KRA), gkva_ref[0]).astype(BF16)
    kv = _dot(ckv, wkv_ref[0])
    kr = proj(C_KRA, C_KRB) * tab(T_KA_COS) + proj(C_KRB, C_UQ) * tab(T_KA_SIN)
    ka_ref[0] = (kv[:, :512] + rep(kr, 4)).astype(BF16)
    va_ref[0] = kv[:, 512:].astype(BF16)
    uq_ref[0] = proj(C_UQ, C_UK)
    uk_ref[0] = proj(C_UK, C_WQA)
    proj_t(R_UV, R_UO, vt_ref)
    proj_t(R_UO, R_GT, ot_ref)
    proj_t(R_GT, R_END, gt_ref)
    qw = proj(C_WQA, C_WQB) * rep(tab(T_QW_COS), 2) + proj(C_WQB, C_WKA) * rep(tab(T_QW_SIN), 2)
    qw_ref[0] = qw.astype(BF16)
    kw = proj(C_WKA, C_WKB) * rep(tab(T_KW_COS), 2) + proj(C_WKB, C_WV) * rep(tab(T_KW_SIN), 2)
    kw_ref[0] = kw.astype(BF16)
    vw_ref[0] = proj(C_WV, C_UF).astype(BF16)
    uf_ref[0] = proj(C_UF, C_END).astype(BF16)


def _inproj(x, mod, ctx_row, pw, l, tab, tm):
    b, n, d = x.shape
    tm = min(tm, n)
    row = lambda bi, i: (bi, i, 0)
    cpt = tm // LANES

    def out(w, dt):
        return jax.ShapeDtypeStruct((b, n, w), dt), pl.BlockSpec((1, tm, w), row)

    def out_t(w, dt):
        return (jax.ShapeDtypeStruct((b, n // LANES, w, LANES), dt),
                pl.BlockSpec((1, cpt, w, LANES), lambda bi, i: (bi, i, 0, 0)))

    outs = [out(512, BF16), out(512, BF16), out(256, BF16),
            out(256, F32), out(256, F32), out_t(256, BF16), out_t(256, F32), out_t(GT_ROWS, F32),
            out(256, BF16), out(256, BF16), out(256, BF16), out(256, BF16)]
    names = ("g_pre_mix", "w1", "w1t", "g_qa", "wq", "g_kva", "wkv")
    return pl.pallas_call(
        _inproj_kernel,
        grid=(b, n // tm),
        in_specs=[pl.BlockSpec((1, tm, d), row), _mod_spec(l, ctx_row)]
                 + [_layer_spec(pw[k], l) for k in names]
                 + [pl.BlockSpec((tm, 8 * LANES), lambda bi, i: (i, 0))],
        out_specs=[o[1] for o in outs],
        out_shape=[o[0] for o in outs],
        compiler_params=_cparams(("parallel", "parallel")),
        name="inproj",
    )(x, mod, *[pw[k] for k in names], tab)


def _mla_kernel(*refs, has_lat):
    if has_lat:
        q_ref, kc_ref, vc_ref, kl_ref, vl_ref, o_ref, kcat, vaug = refs
    else:
        q_ref, kc_ref, vc_ref, o_ref, kcat, vaug = refs
    nc = kc_ref.shape[1]
    nk = kcat.shape[0]

    @pl.when(pl.program_id(1) == 0)
    def _():
        kcat[0:nc, :] = kc_ref[0]
        if has_lat:
            kcat[nc:nk, :] = kl_ref[0]
        for hh in range(MLA_HEADS):
            vsl = slice(LANES * (hh // 2), LANES * (hh // 2 + 1))

            def keep(v):
                lane = lax.broadcasted_iota(jnp.int32, v.shape, 1)
                own = (lane < MLA_V) if hh % 2 == 0 else (lane >= MLA_V)
                return jnp.where(own, v, jnp.zeros_like(v))

            vaug[hh, 0:nc, 0:LANES] = keep(vc_ref[0, :, vsl])
            if has_lat:
                vaug[hh, nc:nk, 0:LANES] = keep(vl_ref[0, :, vsl])
            vaug[hh, :, LANES:2 * LANES] = jnp.ones((nk, LANES), BF16)

    blocks = []
    for pr in range(MLA_HEADS // 2):
        acc = None
        for e in range(2):
            hh = 2 * pr + e
            hsl = slice(LANES * hh, LANES * (hh + 1))
            s = _dot_nt(q_ref[0, :, hsl], kcat[:, hsl])
            mx = jnp.max(s, axis=-1, keepdims=True)
            p = jnp.exp2(s - mx).astype(BF16)
            o2 = _dot(p, vaug[hh])
            o = o2[:, :LANES] / o2[:, LANES:]
            acc = o if acc is None else acc + o
        blocks.append(acc)
    o_ref[0] = jnp.concatenate(blocks, axis=-1).astype(o_ref.dtype)


def _mla_attend(q, kc, vc, kl=None, vl=None, tq=512):
    b, n, _ = q.shape
    tq = min(tq, n)
    has_lat = kl is not None
    nk = kc.shape[1] + (kl.shape[1] if has_lat else 0)
    whole = lambda a: pl.BlockSpec((1,) + a.shape[1:], lambda bi, i: (bi, 0, 0))
    ins = [q, kc, vc] + ([kl, vl] if has_lat else [])
    specs = [pl.BlockSpec((1, tq, 512), lambda bi, i: (bi, i, 0))] + [whole(a) for a in ins[1:]]
    return pl.pallas_call(
        functools.partial(_mla_kernel, has_lat=has_lat),
        grid=(b, n // tq),
        in_specs=specs,
        out_specs=pl.BlockSpec((1, tq, 256), lambda bi, i: (bi, i, 0)),
        out_shape=jax.ShapeDtypeStruct((b, n, 256), BF16),
        scratch_shapes=[pltpu.VMEM((nk, 512), BF16), pltpu.VMEM((MLA_HEADS, nk, 2 * LANES), BF16)],
        compiler_params=_cparams(("parallel", "arbitrary")),
        name="mla_attend",
    )(*ins)


def _win_kernel(*refs, has_lat, n_lat, layer):
    if has_lat:
        sink_ref, q_ref, kc_ref, vc_ref, kl_ref, vl_ref, o_ref = refs
    else:
        sink_ref, q_ref, kc_ref, vc_ref, o_ref = refs
    tq = q_ref.shape[1]
    wk = tq + 2 * WINDOW
    lane = lax.broadcasted_iota(jnp.int32, (tq, LANES), 1)
    if has_lat:
        t0 = pl.program_id(1) * tq
        start = pl.multiple_of(jnp.clip(t0 - WINDOW, 0, n_lat - wk), LANES)
        qpos = t0 + lax.broadcasted_iota(jnp.int32, (tq, wk), 0)
        kpos = start + lax.broadcasted_iota(jnp.int32, (tq, wk), 1)
        band = jnp.abs(kpos - qpos) <= WINDOW

    def aug(v, e):
        vlane = lax.broadcasted_iota(jnp.int32, v.shape, 1)
        own = (vlane < HEAD_DIM) if e == 0 else (vlane >= HEAD_DIM)
        return jnp.concatenate([jnp.where(own, v, jnp.zeros_like(v)), jnp.ones(v.shape, BF16)], axis=1)

    blocks = []
    for pr in range(WG_KV_HEADS):
        psl = slice(LANES * pr, LANES * (pr + 1))
        qp = q_ref[0, :, psl]
        kc = kc_ref[0, :, psl]
        vc = vc_ref[0, :, psl]
        if has_lat:
            kl = kl_ref[0, pl.ds(start, wk), psl]
            vl = vl_ref[0, pl.ds(start, wk), psl]
        acc = None
        for e in range(2):
            own = (lane < HEAD_DIM) if e == 0 else (lane >= HEAD_DIM)
            q = jnp.where(own, qp, jnp.zeros_like(qp))
            sink = sink_ref[layer, 2 * pr + e] * LOG2E
            s_c = _dot_nt(q, kc)
            mx = jnp.maximum(jnp.max(s_c, axis=-1, keepdims=True), sink)
            if has_lat:
                s_l = jnp.where(band, _dot_nt(q, kl), NEG)
                mx = jnp.maximum(mx, jnp.max(s_l, axis=-1, keepdims=True))
            o2 = _dot(jnp.exp2(s_c - mx).astype(BF16), aug(vc, e))
            if has_lat:
                o2 = o2 + _dot(jnp.exp2(s_l - mx).astype(BF16), aug(vl, e))
            o = o2[:, :LANES] / (o2[:, LANES:] + jnp.exp2(sink - mx))
            acc = o if acc is None else acc + o
        blocks.append(acc)
    o_ref[0] = jnp.concatenate(blocks, axis=-1).astype(o_ref.dtype)


def _win_attend(sink, layer, q, kc, vc, kl=None, vl=None, tq=256):
    b, n, _ = q.shape
    tq = min(tq, n)
    has_lat = kl is not None
    whole = lambda a: pl.BlockSpec((1,) + a.shape[1:], lambda bi, i: (bi, 0, 0))
    ins = [q, kc, vc] + ([kl, vl] if has_lat else [])
    specs = ([pl.BlockSpec(memory_space=pltpu.SMEM),
              pl.BlockSpec((1, tq, 256), lambda bi, i: (bi, i, 0))] + [whole(a) for a in ins[1:]])
    return pl.pallas_call(
        functools.partial(_win_kernel, has_lat=has_lat, n_lat=n, layer=layer),
        grid=(b, n // tq),
        in_specs=specs,
        out_specs=pl.BlockSpec((1, tq, 256), lambda bi, i: (bi, i, 0)),
        out_shape=jax.ShapeDtypeStruct((b, n, 256), BF16),
        compiler_params=_cparams(("parallel", "parallel")),
        name="win_attend",
    )(sink, *ins)


def _fourier_kernel(u_ref, cs_ref, dft_ref, o_ref, ab_ref, *, tr):
    n = u_ref.shape[1]
    u = u_ref[0]
    ab = _dot(u, cs_ref[...])
    ab_ref[0:n, :] = ab[:, :256].astype(BF16)
    ab_ref[n:2 * n, :] = ab[:, 256:].astype(BF16)
    scale = 1.0 / math.sqrt(n * FN_GROUP_W)

    def body(r, carry):
        r0 = pl.multiple_of(r * tr, tr)
        y = _dot(dft_ref[pl.ds(r0, tr), :], ab_ref[...])
        o_ref[0, pl.ds(r0, tr), :] = (y * scale).astype(o_ref.dtype)
        return carry

    lax.fori_loop(0, n // tr, body, 0)


def _fourier(u, cs, dft):
    b, n, w = u.shape
    tr = min(512, n)
    return pl.pallas_call(
        functools.partial(_fourier_kernel, tr=tr),
        grid=(b,),
        in_specs=[pl.BlockSpec((1, n, w), lambda bi: (bi, 0, 0)),
                  _const_spec(cs.shape),
                  _const_spec(dft.shape)],
        out_specs=pl.BlockSpec((1, n, w), lambda bi: (bi, 0, 0)),
        out_shape=jax.ShapeDtypeStruct((b, n, w), BF16),
        scratch_shapes=[pltpu.VMEM((2 * n, w), BF16)],
        compiler_params=_cparams(("parallel",)),
        name="fourier",
    )(u, cs, dft)


def _split3(x):
    hi = x.astype(BF16).astype(F32)
    r1 = x - hi
    mid = r1.astype(BF16).astype(F32)
    lo = (r1 - mid).astype(BF16).astype(F32)
    return hi, mid, lo


def _mlstm_kernel(*refs, ctx_out):
    (uq_l, uk_l, vt_l, ot_l, gt_l, uq_c, uk_c, vt_c, ot_c, gt_c, wconv_ref, bgt_ref) = refs[:12]
    if ctx_out:
        y_l, y_c = refs[12:14]
        scratch = refs[14:]
    else:
        y_l = refs[12]
        y_c = None
        scratch = refs[13:]
    qs_l, ks_l, qs_c, ks_c = scratch[:4]
    hacc_l, hacc_c = scratch[4:6], scratch[6:8]
    cst = scratch[8:8 + 2 * ML_HEADS]
    mst = scratch[8 + 2 * ML_HEADS:]
    L = ML_CHUNK
    row = lax.broadcasted_iota(jnp.int32, (L, L), 0)
    col = lax.broadcasted_iota(jnp.int32, (L, L), 1)
    triu_b = (row <= col).astype(BF16)
    krow = lax.broadcasted_iota(jnp.int32, (32, L), 0)
    row2 = lax.broadcasted_iota(jnp.int32, (L, 256), 0)
    zeros8 = jnp.zeros((8, L), F32)

    def prep(x_ref, out_ref, w3, scale):
        n = x_ref.shape[1]
        nc = n // L

        def body(c, carry):
            r0 = pl.multiple_of(c * L, L)
            xc = x_ref[0, pl.ds(r0, L), :]
            pstart = pl.multiple_of(jnp.maximum(r0 - 8, 0), 8)
            nstart = pl.multiple_of(jnp.minimum(r0 + L, n - 8), 8)
            prow = x_ref[0, pl.ds(pstart, 8), :][7:8, :] * jnp.where(c > 0, 1.0, 0.0)
            nrow = x_ref[0, pl.ds(nstart, 8), :][0:1, :] * jnp.where(c < nc - 1, 1.0, 0.0)
            xp = jnp.where(row2 == 0, prow, pltpu.roll(xc, 1, axis=0))
            xn = jnp.where(row2 == L - 1, nrow, pltpu.roll(xc, L - 1, axis=0))
            a = xp * w3[0:1, :] + xc * w3[1:2, :] + xn * w3[2:3, :]
            out_ref[pl.ds(r0, L), :] = (a * _sigmoid(a) * scale).astype(BF16)
            return carry

        lax.fori_loop(0, nc, body, 0)

    wq3 = wconv_ref[0, :, 0:256]
    wk3 = wconv_ref[0, :, 256:512]
    kscale = ML_HEAD_DIM ** -0.5
    prep(uq_l, qs_l, wq3, 1.0)
    prep(uk_l, ks_l, wk3, kscale)
    prep(uq_c, qs_c, wq3, 1.0)
    prep(uk_c, ks_c, wk3, kscale)
    for ref in list(cst) + list(mst):
        ref[...] = jnp.zeros_like(ref)

    chains = [(d, hh) for d in range(2) for hh in range(ML_HEADS)]
    own_l = [(col < ML_HEAD_DIM) if hh % 2 == 0 else (col >= ML_HEAD_DIM) for hh in range(ML_HEADS)]
    own_r = [(row < ML_HEAD_DIM) if hh % 2 == 0 else (row >= ML_HEAD_DIM) for hh in range(ML_HEADS)]
    mask = [row <= col, row >= col]
    sel = [jnp.where((krow & 7) == hh, 1.0, 0.0).astype(BF16) for hh in range(ML_HEADS)]

    def chunk_pair(qs, ks, vt, gt, hacc, cs, need_h):
        r0 = [pl.multiple_of(c * L, L) for c in cs]
        cum, tot, b, m_st, wk, decay, lhs = [], [], [], [], [], [], []
        for d in range(2):
            g = gt[0, cs[d]] + bgt_ref[0]
            li = g[16 * d:16 * d + 8]
            gf = g[16 * d + 8:16 * d + 16]
            ls = jnp.minimum(gf, 0.0) - jnp.log(1.0 + jnp.exp(-jnp.abs(gf)))
            cu = sum(_dot(piece.astype(BF16), triu_b) for piece in _split3(ls))
            tt = cu[:, L - 1:L]
            if d == 1:
                cu = tt - cu + ls
            bb = li - cu
            ms = mst[d][...]
            lw = tt + bb
            m_new = jnp.maximum(tt + ms, jnp.max(lw, axis=1, keepdims=True))
            wk.append(jnp.exp(lw - m_new))
            decay.append(jnp.exp(tt + ms - m_new))
            mst[d][...] = m_new
            cum.append(cu)
            tot.append(tt)
            b.append(bb)
            m_st.append(ms)
            if need_h:
                lhs.append(jnp.concatenate(list(_split3(bb)) + [zeros8], axis=0).astype(BF16))
        kh, v_aug, c_aug, qh, kq, cq, bm = [], [], [], [], [], [], []
        for d, hh in chains:
            psl = slice(LANES * (hh // 2), LANES * (hh // 2 + 1))
            kb = ks[pl.ds(r0[d], L), psl]
            kh.append(jnp.where(own_l[hh], kb, jnp.zeros_like(kb)))
            vb = vt[0, cs[d], psl, :]
            v_aug.append(jnp.where(own_r[hh], vb, jnp.ones_like(vb)))
            c_aug.append(cst[d * ML_HEADS + hh][...])
            if need_h:
                qb = qs[pl.ds(r0[d], L), psl]
                qh.append(jnp.where(own_l[hh], qb, jnp.zeros_like(qb)))
        if need_h:
            for i, (d, hh) in enumerate(chains):
                bm.append(jnp.where(mask[d], _dot_tn(lhs[d], sel[hh]), NEG))
            for i in range(len(chains)):
                kq.append(_dot_nt(kh[i], qh[i]))
            for i in range(len(chains)):
                cq.append(_dot_nt(c_aug[i].astype(BF16), qh[i]))
            mu, st = [], []
            for i, (d, hh) in enumerate(chains):
                mu.append(jnp.maximum(m_st[d][hh:hh + 1, :], jnp.max(bm[i], axis=0, keepdims=True)))
            for i in range(len(chains)):
                st.append((kq[i] * jnp.exp(bm[i] - mu[i])).astype(BF16))
            r = []
            for i, (d, hh) in enumerate(chains):
                r.append(_dot(v_aug[i], st[i]) + cq[i] * jnp.exp(m_st[d][hh:hh + 1, :] - mu[i]))
            hs = []
            for i, (d, hh) in enumerate(chains):
                e_negm = jnp.exp(-(cum[d][hh:hh + 1, :] + mu[i]))
                den = pltpu.roll(r[i], ML_HEAD_DIM, axis=0)
                hs.append(r[i] / jnp.maximum(jnp.abs(den), e_negm))
            for d in range(2):
                for p in range(2):
                    i = d * ML_HEADS + 2 * p
                    hacc[d][cs[d], LANES * p:LANES * (p + 1), :] = jnp.where(row < ML_HEAD_DIM, hs[i], hs[i + 1])
        dc = []
        for i, (d, hh) in enumerate(chains):
            vw = (v_aug[i].astype(F32) * wk[d][hh:hh + 1, :]).astype(BF16)
            dc.append(_dot(vw, kh[i]))
        for i, (d, hh) in enumerate(chains):
            cst[i][...] = decay[d][hh:hh + 1, :] * c_aug[i] + dc[i]

    def scan(qs, ks, vt, gt, hacc, need_h):
        nc = gt.shape[1]

        def body(j, carry):
            chunk_pair(qs, ks, vt, gt, hacc, (j, nc - 1 - j), need_h)
            return carry

        lax.fori_loop(0, nc, body, 0)

    scan(qs_c, ks_c, vt_c, gt_c, hacc_c, ctx_out)
    scan(qs_l, ks_l, vt_l, gt_l, hacc_l, True)

    def finish(ot, hacc, y):
        def body(c, carry):
            y[0, c] = (_sigmoid(ot[0, c]) * (hacc[0][c] + hacc[1][c])).astype(y.dtype)
            return carry

        lax.fori_loop(0, ot.shape[1], body, 0)

    finish(ot_l, hacc_l, y_l)
    if ctx_out:
        finish(ot_c, hacc_c, y_c)


def _mlstm(lat, ctx, pw, l, ctx_out):
    b, n, _ = lat[0].shape
    nc = ctx[0].shape[1]
    whole = lambda a: pl.BlockSpec((1,) + a.shape[1:], lambda bi: (bi,) + (0,) * (a.ndim - 1))
    ins = list(lat) + list(ctx)
    tshape = lambda m: (b, m // LANES, 256, LANES)
    out_shape = [jax.ShapeDtypeStruct(tshape(n), BF16)]
    out_specs = [pl.BlockSpec((1,) + tshape(n)[1:], lambda bi: (bi, 0, 0, 0))]
    if ctx_out:
        out_shape.append(jax.ShapeDtypeStruct(tshape(nc), BF16))
        out_specs.append(pl.BlockSpec((1,) + tshape(nc)[1:], lambda bi: (bi, 0, 0, 0)))
    res = pl.pallas_call(
        functools.partial(_mlstm_kernel, ctx_out=ctx_out),
        grid=(b,),
        in_specs=[whole(a) for a in ins] + [_layer_spec(pw["w_ml_conv"], l), _layer_spec(pw["bgt"], l)],
        out_specs=out_specs,
        out_shape=out_shape,
        scratch_shapes=[pltpu.VMEM((n, 256), BF16), pltpu.VMEM((n, 256), BF16),
                        pltpu.VMEM((nc, 256), BF16), pltpu.VMEM((nc, 256), BF16)]
                       + [pltpu.VMEM((n // LANES, 256, LANES), F32)] * 2
                       + [pltpu.VMEM((nc // LANES, 256, LANES), F32)] * 2
                       + [pltpu.VMEM((LANES, LANES), F32)] * (2 * ML_HEADS)
                       + [pltpu.VMEM((8, LANES), F32)] * 2,
        compiler_params=_cparams(("parallel",)),
        name="mlstm",
    )(*ins, pw["w_ml_conv"], pw["bgt"])
    return (res[0], res[1]) if ctx_out else (res[0], None)


def _merge_kernel(x_ref, mod_ref, ya_ref, yb_ref, yc_ref, yd_ref, gpre_ref, gpost_ref, gffn_ref,
                  wg_ref, bgate_ref, wb_ref, wo_ref, x1_ref, h2_ref):
    x = x_ref[0]
    m = mod_ref[0, 0]
    hb = (_rms(x, gpre_ref[0]) * (1.0 + m[1:2]) + m[0:1]).astype(BF16)
    acc = None
    yb_t = jnp.concatenate([yb_ref[0, j] for j in range(yb_ref.shape[1])], axis=1)
    for s, y_ref in enumerate((ya_ref, yb_ref, yc_ref, yd_ref)):
        gate = _sigmoid(_dot(hb, wg_ref[0, s]) + bgate_ref[0, s:s + 1, :])
        branch = _dot_tn(yb_t, wb_ref[0, s]) if s == 1 else _dot(y_ref[0], wb_ref[0, s])
        term = gate * branch
        acc = term if acc is None else acc + term
    y = _dot(acc.astype(BF16), wo_ref[0])
    x1 = x + m[2:3] * _rms(y, gpost_ref[0])
    x1_ref[0] = x1
    h2_ref[0] = (_rms(x1, gffn_ref[0]) * (1.0 + m[4:5]) + m[3:4]).astype(BF16)


def _merge(x, mod, ctx_row, ys, pw, l, tm):
    b, n, d = x.shape
    tm = min(tm, n)
    row = lambda bi, i: (bi, i, 0)
    names = ("g_pre_mix", "g_post_mix", "g_pre_ffn", "wg", "b_gate", "wb", "wo")
    return pl.pallas_call(
        _merge_kernel,
        grid=(b, n // tm),
        in_specs=[pl.BlockSpec((1, tm, d), row), _mod_spec(l, ctx_row)]
                 + [pl.BlockSpec((1, tm, BRANCH_W), row),
                    pl.BlockSpec((1, tm // LANES, BRANCH_W, LANES), lambda bi, i: (bi, i, 0, 0)),
                    pl.BlockSpec((1, tm, BRANCH_W), row), pl.BlockSpec((1, tm, BRANCH_W), row)]
                 + [_layer_spec(pw[k], l) for k in names],
        out_specs=[pl.BlockSpec((1, tm, d), row), pl.BlockSpec((1, tm, d), row)],
        out_shape=[jax.ShapeDtypeStruct((b, n, d), F32), jax.ShapeDtypeStruct((b, n, d), BF16)],
        compiler_params=_cparams(("parallel", "parallel")),
        name="merge",
    )(x, mod, *ys, *[pw[k] for k in names])


FFN_HALO = 16
FFN_COL_CHUNKS = ((0, 1536), (1536, 2816))


def _ffn_kernel(h_ref, hp_ref, hn_ref, x1_ref, mod_ref, gpost_ref, wup_ref, wconv_ref, bconv_ref, wdown_ref,
                o_ref):
    i = pl.program_id(1)
    last = pl.num_programs(1) - 1
    tm = h_ref.shape[1]
    hm = h_ref[0]
    hp = jnp.where(i > 0, hp_ref[0], jnp.zeros_like(hp_ref[0]))
    hn = jnp.where(i < last, hn_ref[0], jnp.zeros_like(hn_ref[0]))
    hext = jnp.concatenate([hp, hm, hn], axis=0)
    ne = tm + 2 * FFN_HALO
    acc = None
    for lo, hi in FFN_COL_CHUNKS:
        a = _dot(hext, wup_ref[0, :, lo:hi])
        ap = pltpu.roll(a, 1, axis=0)[FFN_HALO:FFN_HALO + tm]
        an = pltpu.roll(a, ne - 1, axis=0)[FFN_HALO:FFN_HALO + tm]
        ac = a[FFN_HALO:FFN_HALO + tm]
        a = (ap * wconv_ref[0, 0:1, lo:hi] + ac * wconv_ref[0, 1:2, lo:hi] + an * wconv_ref[0, 2:3, lo:hi]
             + bconv_ref[0, :, lo:hi])
        v = _dot(hm, wup_ref[0, :, D_FF + lo:D_FF + hi])
        act = (a * _sigmoid(a) * v).astype(BF16)
        term = _dot(act, wdown_ref[0, lo:hi, :])
        acc = term if acc is None else acc + term
    m = mod_ref[0, 0]
    o_ref[0] = x1_ref[0] + m[5:6] * _rms(acc, gpost_ref[0])


def _ffn(h2, x1, mod, ctx_row, pw, l, tm):
    b, n, d = x1.shape
    tm = min(tm, n)
    row = lambda bi, i: (bi, i, 0)
    r = tm // FFN_HALO
    nblk = n // FFN_HALO
    names = ("g_post_ffn", "wup", "w_ffn_conv", "b_ffn_conv", "wdown")
    return pl.pallas_call(
        _ffn_kernel,
        grid=(b, n // tm),
        in_specs=[pl.BlockSpec((1, tm, d), row),
                  pl.BlockSpec((1, FFN_HALO, d), lambda bi, i: (bi, jnp.maximum(i * r - 1, 0), 0)),
                  pl.BlockSpec((1, FFN_HALO, d), lambda bi, i: (bi, jnp.minimum((i + 1) * r, nblk - 1), 0)),
                  pl.BlockSpec((1, tm, d), row),
                  _mod_spec(l, ctx_row)]
                 + [_layer_spec(pw[k], l) for k in names],
        out_specs=pl.BlockSpec((1, tm, d), row),
        out_shape=jax.ShapeDtypeStruct((b, n, d), F32),
        compiler_params=_cparams(("parallel", "parallel")),
        name="conv_ffn",
    )(h2, h2, h2, x1, mod, *[pw[k] for k in names])


def _rope_swap(d):
    half, nf = d // 2, d // 4
    idx = np.arange(d)
    j = idx % half
    return np.where(j < nf, idx + nf, idx - nf)


def _rope_cos_sin(n, d, identity):
    if identity:
        return jnp.ones((n, d), F32), jnp.zeros((n, d), F32)
    half, nf = d // 2, d // 4
    t = jnp.arange(n, dtype=jnp.int32)
    row = (t // GRID_W).astype(F32)[:, None]
    colp = (t % GRID_W).astype(F32)[:, None]
    inv = ROPE_BASE ** (-jnp.arange(nf, dtype=F32) / nf)
    ang = jnp.concatenate([row * inv, row * inv, colp * inv, colp * inv], axis=-1)
    sign = jnp.asarray(np.where(np.arange(d) % half < nf, -1.0, 1.0), F32)
    return jnp.cos(ang), jnp.sin(ang) * sign


def _rope_table(n, identity):
    ca, sa = _rope_cos_sin(n, MLA_ROPE, identity)
    cw, sw = _rope_cos_sin(n, HEAD_DIM, identity)
    scale_a = (MLA_NOPE + MLA_ROPE) ** -0.5 * LOG2E
    scale_w = HEAD_DIM ** -0.5 * LOG2E
    z32 = jnp.zeros((n, LANES - MLA_NOPE - MLA_ROPE), F32)
    one64 = jnp.ones((n, MLA_NOPE), F32)
    zero64 = jnp.zeros((n, MLA_NOPE), F32)
    qa_cos = jnp.concatenate([one64, ca, z32], -1) * scale_a
    qa_sin = jnp.concatenate([zero64, sa, z32], -1) * scale_a
    ka_cos = jnp.concatenate([zero64, ca, z32], -1)
    ka_sin = jnp.concatenate([zero64, sa, z32], -1)
    cw2 = jnp.concatenate([cw, cw], -1)
    sw2 = jnp.concatenate([sw, sw], -1)
    return jnp.concatenate([qa_cos, qa_sin, ka_cos, ka_sin, cw2 * scale_w, sw2 * scale_w, cw2, sw2], axis=-1)


def _dft_tables(n):
    k = jnp.arange(n, dtype=jnp.int32)
    ang = ((k[:, None] * k[None, :]) % n).astype(F32) * (2.0 * math.pi / n)
    dft = jnp.concatenate([jnp.cos(ang), -jnp.sin(ang)], axis=-1).astype(BF16)
    j = np.arange(FN_GROUP_W)
    a64 = (np.outer(j, j) % FN_GROUP_W) * (2.0 * np.pi / FN_GROUP_W)
    eye = np.eye(FN_GROUPS)
    cs = np.concatenate([np.kron(eye, np.cos(a64)), np.kron(eye, np.sin(a64))], axis=-1)
    return dft, jnp.asarray(cs, F32).astype(BF16)


def _w1_index():
    off = np.concatenate([[0], np.cumsum(IN_SPLITS)])
    z = int(off[-1])
    seg = lambda i: np.arange(off[i], off[i + 1])
    zeros = lambda w: np.full((w,), z)
    sw32, sw64 = _rope_swap(MLA_ROPE), _rope_swap(HEAD_DIM)
    place_kr = lambda t: np.concatenate([zeros(MLA_NOPE), t, zeros(LANES - MLA_NOPE - MLA_ROPE)])
    head = lambda i, h: off[i] + HEAD_DIM * h + np.arange(HEAD_DIM)
    dup = lambda i, perm: np.concatenate([head(i, h)[perm] for h in (0, 0, 1, 1)])
    ident = np.arange(HEAD_DIM)
    idx = np.concatenate([
        seg(0), seg(1), place_kr(seg(2)), place_kr(seg(2)[sw32]),
        seg(3), seg(4),
        seg(8), np.concatenate([head(8, h)[sw64] for h in range(WG_HEADS)]),
        dup(9, ident), dup(9, sw64), dup(10, ident), seg(11)])
    assert idx.shape == (C_END,)
    return idx.astype(np.int32)


def _w1t_index():
    off = np.concatenate([[0], np.cumsum(IN_SPLITS)])
    z = int(off[-1])
    gates = np.concatenate([np.concatenate([off[7] + ML_HEADS * k + np.arange(ML_HEADS), np.full((8 - ML_HEADS,), z)])
                            for k in range(4)])
    idx = np.concatenate([np.arange(off[5], off[7]), gates])
    assert idx.shape == (R_END,)
    return idx.astype(np.int32)


def _gate_bias_index():
    return np.concatenate([np.concatenate([ML_HEADS * k + np.arange(ML_HEADS), np.full((8 - ML_HEADS,), 4 * ML_HEADS)])
                           for k in range(4)]).astype(np.int32)


def _wq_index():
    dq = MLA_NOPE + MLA_ROPE
    z = MLA_HEADS * dq
    sw32 = _rope_swap(MLA_ROPE)
    pad = np.full((LANES - dq,), z)
    qa = [np.concatenate([dq * h + np.arange(dq), pad]) for h in range(MLA_HEADS)]
    qb = [np.concatenate([np.full((MLA_NOPE,), z), dq * h + MLA_NOPE + sw32, pad]) for h in range(MLA_HEADS)]
    return np.concatenate(qa + qb).astype(np.int32)


def _wkv_index():
    dkv = MLA_NOPE + MLA_V
    z = MLA_HEADS * dkv
    kn = [np.concatenate([dkv * h + np.arange(MLA_NOPE), np.full((LANES - MLA_NOPE,), z)]) for h in range(MLA_HEADS)]
    vv = [dkv * h + MLA_NOPE + np.arange(MLA_V) for h in range(MLA_HEADS)]
    return np.concatenate(kn + vv).astype(np.int32)


def _take_cols(w, idx):
    z = w.shape[-1]
    idx = np.asarray(idx)
    cuts = [0] + [i for i in range(1, len(idx)) if not ((idx[i] == idx[i - 1] + 1 and idx[i] != z)
                                                        or (idx[i] == z and idx[i - 1] == z))] + [len(idx)]
    parts = []
    for lo, hi in zip(cuts[:-1], cuts[1:]):
        if idx[lo] == z:
            parts.append(jnp.zeros(w.shape[:-1] + (hi - lo,), w.dtype))
        else:
            parts.append(w[..., int(idx[lo]):int(idx[hi - 1]) + 1])
    return jnp.concatenate(parts, axis=-1)


def _prep_weights(p):
    depth = p["w_in"].shape[0]
    vec = lambda a: a.reshape(depth, 1, a.shape[-1])
    bgt = jnp.broadcast_to(_take_cols(p["b_ml_gates"], _gate_bias_index())[:, :, None], (depth, GT_ROWS, LANES))
    return {
        "w1": _take_cols(p["w_in"], _w1_index()).astype(BF16),
        "w1t": jnp.swapaxes(_take_cols(p["w_in"], _w1t_index()), 1, 2).astype(BF16),
        "wq": _take_cols(p["w_uq"], _wq_index()).astype(BF16),
        "wkv": _take_cols(p["w_ukv"], _wkv_index()).astype(BF16),
        "g_qa": vec(p["g_qa"]), "g_kva": vec(p["g_kva"]),
        "g_pre_mix": vec(p["g_pre_mix"]), "g_post_mix": vec(p["g_post_mix"]),
        "g_pre_ffn": vec(p["g_pre_ffn"]), "g_post_ffn": vec(p["g_post_ffn"]),
        "w_ml_conv": p["w_ml_conv"], "bgt": bgt, "wg_sink": p["wg_sink"],
        "wg": p["w_gate"].astype(BF16), "b_gate": p["b_gate"],
        "wb": p["w_branch"].astype(BF16), "wo": p["w_out"].astype(BF16),
        "wup": p["w_up"].astype(BF16), "w_ffn_conv": p["w_ffn_conv"],
        "b_ffn_conv": vec(p["b_ffn_conv"]), "wdown": p["w_down"].astype(BF16),
    }


TM = 512


def kernel(x, c, ctx, c_ctx, w_mod, b_mod, g_pre_mix, g_post_mix, g_pre_ffn, g_post_ffn, w_in, g_qa, w_uq, g_kva,
           w_ukv, w_ml_conv, b_ml_gates, wg_sink, w_gate, b_gate, w_branch, w_out, w_up, w_ffn_conv, b_ffn_conv,
           w_down):
    p = dict(g_pre_mix=g_pre_mix, g_post_mix=g_post_mix, g_pre_ffn=g_pre_ffn, g_post_ffn=g_post_ffn, w_in=w_in,
             g_qa=g_qa, w_uq=w_uq, g_kva=g_kva, w_ukv=w_ukv, w_ml_conv=w_ml_conv, b_ml_gates=b_ml_gates,
             wg_sink=wg_sink, w_gate=w_gate, b_gate=b_gate, w_branch=w_branch, w_out=w_out, w_up=w_up,
             w_ffn_conv=w_ffn_conv, b_ffn_conv=b_ffn_conv, w_down=w_down)
    bsz, n, d = x.shape
    n_ctx = ctx.shape[1]
    depth = w_mod.shape[0]
    rows = -(-(bsz + 1) // 8) * 8
    c_all = jnp.concatenate([c, c_ctx[None, :], jnp.zeros((rows - bsz - 1, d), F32)], axis=0)
    mod = _modulation(c_all, w_mod, b_mod).reshape(depth, rows, 6, d)
    pw = _prep_weights(p)
    tab_l = _rope_table(n, identity=False)
    tab_c = _rope_table(n_ctx, identity=True)
    dft_l, cs64 = _dft_tables(n)
    dft_c, _ = _dft_tables(n_ctx)
    xc = ctx
    for l in range(depth):
        ctx_out = l < depth - 1
        qa, ka, va, uq, uk, vt, ot, gt, qw, kw, vw, uf = _inproj(x, mod, None, pw, l, tab_l, TM)
        qa_c, ka_c, va_c, uq_c, uk_c, vt_c, ot_c, gt_c, qw_c, kw_c, vw_c, uf_c = _inproj(
            xc, mod, bsz, pw, l, tab_c, TM)
        ya = _mla_attend(qa, ka_c, va_c, ka, va)
        yb, yb_c = _mlstm((uq, uk, vt, ot, gt), (uq_c, uk_c, vt_c, ot_c, gt_c), pw, l, ctx_out)
        yc = _win_attend(pw["wg_sink"], l, qw, kw_c, vw_c, kw, vw)
        yd = _fourier(uf, cs64, dft_l)
        x1, h2 = _merge(x, mod, None, (ya, yb, yc, yd), pw, l, TM)
        x = _ffn(h2, x1, mod, None, pw, l, TM)
        if ctx_out:
            ya_c = _mla_attend(qa_c, ka_c, va_c)
            yc_c = _win_attend(pw["wg_sink"], l, qw_c, kw_c, vw_c)
            yd_c = _fourier(uf_c, cs64, dft_c)
            xc1, hc2 = _merge(xc, mod, bsz, (ya_c, yb_c, yc_c, yd_c), pw, l, TM)
            xc = _ffn(hc2, xc1, mod, bsz, pw, l, TM)
    return x
```

```python
import functools
import math

import numpy as np
import jax
import jax.numpy as jnp
from jax import lax
from jax.experimental import pallas as pl
from jax.experimental.pallas import tpu as pltpu

F32 = jnp.float32
BF16 = jnp.bfloat16

D_MODEL = 1024
GRID_W = 64
N_BRANCH = 4
BRANCH_W = 256
MLA_HEADS = 4
MLA_NOPE = 64
MLA_ROPE = 32
MLA_V = 64
MLA_Q_RANK = 256
MLA_KV_RANK = 256
ML_HEADS = 4
ML_HEAD_DIM = 64
ML_CHUNK = 128
WG_HEADS = 4
WG_KV_HEADS = 2
HEAD_DIM = 64
WINDOW = 128
FN_GROUPS = 4
FN_GROUP_W = 64
D_FF = 2816
ROPE_BASE = 10000.0
EPS = 1e-6
IN_SPLITS = (MLA_Q_RANK, MLA_KV_RANK, MLA_ROPE, 256, 256, 256, 256, 16, 256, 128, 128, 256)

LANES = 128
NEG = -1e30
LOG2E = 1.4426950408889634
VMEM_LIMIT = 56 * 1024 * 1024

C_CQ, C_CKV, C_KRA, C_KRB, C_UQ, C_UK, C_WQA, C_WQB, C_WKA, C_WKB, C_WV, C_UF, C_END = (
    0, 256, 512, 640, 768, 1024, 1280, 1536, 1792, 2048, 2304, 2560, 2816)
R_UV, R_UO, R_GT, R_END = 0, 256, 512, 544
GT_ROWS = 32
T_QA_COS, T_QA_SIN, T_KA_COS, T_KA_SIN, T_QW_COS, T_QW_SIN, T_KW_COS, T_KW_SIN = range(8)


def _cparams(sem):
    return pltpu.CompilerParams(dimension_semantics=sem, vmem_limit_bytes=VMEM_LIMIT)


def _layer_spec(arr, l):
    nd = arr.ndim - 1
    return pl.BlockSpec((1,) + arr.shape[1:], lambda *_: (l,) + (0,) * nd, pipeline_mode=pl.Buffered(1))


def _const_spec(shape):
    nd = len(shape)
    return pl.BlockSpec(shape, lambda *_: (0,) * nd, pipeline_mode=pl.Buffered(1))


def _mod_spec(l, ctx_row):
    if ctx_row is None:
        return pl.BlockSpec((1, 1, 6, D_MODEL), lambda bi, *_: (l, bi, 0, 0))
    return pl.BlockSpec((1, 1, 6, D_MODEL), lambda *_: (l, ctx_row, 0, 0))


def _rms(x, g):
    ms = jnp.mean(x * x, axis=-1, keepdims=True)
    return x * lax.rsqrt(ms + EPS) * g


def _dot(a, b):
    return jnp.dot(a, b, preferred_element_type=F32)


def _dot_nt(a, b):
    return lax.dot_general(a, b, (((1,), (1,)), ((), ())), preferred_element_type=F32)


def _dot_tn(a, b):
    return lax.dot_general(a, b, (((0,), (0,)), ((), ())), preferred_element_type=F32)


def _sigmoid(x):
    return 1.0 / (1.0 + jnp.exp(-x))


def _mod_kernel(c_ref, w_ref, b_ref, o_ref):
    c = c_ref[...]
    s = (c * _sigmoid(c)).astype(BF16)
    o_ref[0] = _dot(s, w_ref[0].astype(BF16)) + b_ref[0]


def _modulation(c_all, w_mod, b_mod):
    depth, d, n = w_mod.shape
    rows = c_all.shape[0]
    tn = 1536
    return pl.pallas_call(
        _mod_kernel,
        grid=(depth, n // tn),
        in_specs=[pl.BlockSpec((rows, d), lambda l, j: (0, 0)),
                  pl.BlockSpec((1, d, tn), lambda l, j: (l, 0, j)),
                  pl.BlockSpec((1, 1, tn), lambda l, j: (l, 0, j))],
        out_specs=pl.BlockSpec((1, rows, tn), lambda l, j: (l, 0, j)),
        out_shape=jax.ShapeDtypeStruct((depth, rows, n), F32),
        compiler_params=_cparams(("parallel", "parallel")),
        name="modulation",
    )(c_all, w_mod, b_mod.reshape(depth, 1, n))


def _inproj_kernel(x_ref, mod_ref, gpre_ref, w1_ref, w1t_ref, gqa_ref, wq_ref, gkva_ref, wkv_ref, tab_ref,
                   qa_ref, ka_ref, va_ref, uq_ref, uk_ref, vt_ref, ot_ref, gt_ref,
                   qw_ref, kw_ref, vw_ref, uf_ref):
    x = x_ref[0]
    m = mod_ref[0, 0]
    h = _rms(x, gpre_ref[0]) * (1.0 + m[1:2]) + m[0:1]
    hb = h.astype(BF16)

    u = _dot(hb, w1_ref[0])

    def proj(lo, hi):
        return u[:, lo:hi]

    def proj_t(lo, hi, out_ref):
        r = _dot_nt(w1t_ref[0, lo:hi, :], hb)
        for j in range(out_ref.shape[1]):
            out_ref[0, j] = r[:, LANES * j:LANES * (j + 1)].astype(out_ref.dtype)

    def tab(j):
        return tab_ref[:, LANES * j:LANES * (j + 1)]

    def rep(t, n):
        return jnp.concatenate([t] * n, axis=-1)

    cq = _rms(proj(C_CQ, C_CKV), gqa_ref[0]).astype(BF16)
    q2 = _dot(cq, wq_ref[0])
    q = q2[:, :512] * rep(tab(T_QA_COS), 4) + q2[:, 512:] * rep(tab(T_QA_SIN), 4)
    qa_ref[0] = q.astype(BF16)
    ckv = _rms(proj(C_CKV, C_KRA), gkva_ref[0]).astype(BF16)
    kv = _dot(ckv, wkv_ref[0])
    kr = proj(C_KRA, C_KRB) * tab(T_KA_COS) + proj(C_KRB, C_UQ) * tab(T_KA_SIN)
    ka_ref[0] = (kv[:, :512] + rep(kr, 4)).astype(BF16)
    va_ref[0] = kv[:, 512:].astype(BF16)
    uq_ref[0] = proj(C_UQ, C_UK).astype(uq_ref.dtype)
    uk_ref[0] = proj(C_UK, C_WQA).astype(uk_ref.dtype)
    proj_t(R_UV, R_UO, vt_ref)
    proj_t(R_UO, R_GT, ot_ref)
    proj_t(R_GT, R_END, gt_ref)
    qw = proj(C_WQA, C_WQB) * rep(tab(T_QW_COS), 2) + proj(C_WQB, C_WKA) * rep(tab(T_QW_SIN), 2)
    qw_ref[0] = qw.astype(BF16)
    kw = proj(C_WKA, C_WKB) * rep(tab(T_KW_COS), 2) + proj(C_WKB, C_WV) * rep(tab(T_KW_SIN), 2)
    kw_ref[0] = kw.astype(BF16)
    vw_ref[0] = proj(C_WV, C_UF).astype(BF16)
    uf_ref[0] = proj(C_UF, C_END).astype(BF16)


def _inproj(x, mod, ctx_row, pw, l, tab, tm):
    b, n, d = x.shape
    tm = min(tm, n)
    row = lambda bi, i: (bi, i, 0)
    cpt = tm // LANES

    def out(w, dt):
        return jax.ShapeDtypeStruct((b, n, w), dt), pl.BlockSpec((1, tm, w), row)

    def out_t(w, dt):
        return (jax.ShapeDtypeStruct((b, n // LANES, w, LANES), dt),
                pl.BlockSpec((1, cpt, w, LANES), lambda bi, i: (bi, i, 0, 0)))

    outs = [out(512, BF16), out(512, BF16), out(256, BF16),
            out(256, BF16), out(256, BF16), out_t(256, BF16), out_t(256, BF16), out_t(GT_ROWS, F32),
            out(256, BF16), out(256, BF16), out(256, BF16), out(256, BF16)]
    names = ("g_pre_mix", "w1", "w1t", "g_qa", "wq", "g_kva", "wkv")
    return pl.pallas_call(
        _inproj_kernel,
        grid=(b, n // tm),
        in_specs=[pl.BlockSpec((1, tm, d), row), _mod_spec(l, ctx_row)]
                 + [_layer_spec(pw[k], l) for k in names]
                 + [pl.BlockSpec((tm, 8 * LANES), lambda bi, i: (i, 0))],
        out_specs=[o[1] for o in outs],
        out_shape=[o[0] for o in outs],
        compiler_params=_cparams(("parallel", "parallel")),
        name="inproj",
    )(x, mod, *[pw[k] for k in names], tab)


def _mla_kernel(*refs, has_lat):
    if has_lat:
        q_ref, kc_ref, vc_ref, kl_ref, vl_ref, o_ref, kcat, vaug = refs
    else:
        q_ref, kc_ref, vc_ref, o_ref, kcat, vaug = refs
    nc = kc_ref.shape[1]
    nk = kcat.shape[0]

    @pl.when(pl.program_id(1) == 0)
    def _():
        kcat[0:nc, :] = kc_ref[0]
        if has_lat:
            kcat[nc:nk, :] = kl_ref[0]
        for hh in range(MLA_HEADS):
            vsl = slice(LANES * (hh // 2), LANES * (hh // 2 + 1))

            def keep(v):
                lane = lax.broadcasted_iota(jnp.int32, v.shape, 1)
                own = (lane < MLA_V) if hh % 2 == 0 else (lane >= MLA_V)
                return jnp.where(own, v, jnp.zeros_like(v))

            vaug[hh, 0:nc, 0:LANES] = keep(vc_ref[0, :, vsl])
            if has_lat:
                vaug[hh, nc:nk, 0:LANES] = keep(vl_ref[0, :, vsl])
            vaug[hh, :, LANES:2 * LANES] = jnp.ones((nk, LANES), BF16)

    def logits(hh):
        hsl = slice(LANES * hh, LANES * (hh + 1))
        return _dot_nt(q_ref[0, :, hsl], kcat[:, hsl])

    def attend(hh, s):
        mx = jnp.max(s, axis=-1, keepdims=True)
        o2 = _dot(jnp.exp2(s - mx).astype(BF16), vaug[hh])
        return o2[:, :LANES] / o2[:, LANES:]

    outs = []
    s_next = logits(0)
    for hh in range(MLA_HEADS):
        s_cur = s_next
        if hh + 1 < MLA_HEADS:
            s_next = logits(hh + 1)
        outs.append(attend(hh, s_cur))
    o_ref[0] = jnp.concatenate([outs[0] + outs[1], outs[2] + outs[3]], axis=-1).astype(o_ref.dtype)


def _mla_attend(q, kc, vc, kl=None, vl=None, tq=512):
    b, n, _ = q.shape
    tq = min(tq, n)
    has_lat = kl is not None
    nk = kc.shape[1] + (kl.shape[1] if has_lat else 0)
    whole = lambda a: pl.BlockSpec((1,) + a.shape[1:], lambda bi, i: (bi, 0, 0))
    ins = [q, kc, vc] + ([kl, vl] if has_lat else [])
    specs = [pl.BlockSpec((1, tq, 512), lambda bi, i: (bi, i, 0))] + [whole(a) for a in ins[1:]]
    return pl.pallas_call(
        functools.partial(_mla_kernel, has_lat=has_lat),
        grid=(b, n // tq),
        in_specs=specs,
        out_specs=pl.BlockSpec((1, tq, 256), lambda bi, i: (bi, i, 0)),
        out_shape=jax.ShapeDtypeStruct((b, n, 256), BF16),
        scratch_shapes=[pltpu.VMEM((nk, 512), BF16), pltpu.VMEM((MLA_HEADS, nk, 2 * LANES), BF16)],
        compiler_params=_cparams(("parallel", "arbitrary")),
        name="mla_attend",
    )(*ins)


def _win_kernel(*refs, has_lat, n_lat, layer):
    if has_lat:
        sink_ref, q_ref, kc_ref, vc_ref, kl_ref, vl_ref, o_ref = refs
    else:
        sink_ref, q_ref, kc_ref, vc_ref, o_ref = refs
    tq = q_ref.shape[1]
    wk = tq + 2 * WINDOW
    lane = lax.broadcasted_iota(jnp.int32, (tq, LANES), 1)
    if has_lat:
        t0 = pl.program_id(1) * tq
        start = pl.multiple_of(jnp.clip(t0 - WINDOW, 0, n_lat - wk), LANES)
        qpos = t0 + lax.broadcasted_iota(jnp.int32, (tq, wk), 0)
        kpos = start + lax.broadcasted_iota(jnp.int32, (tq, wk), 1)
        band = jnp.abs(kpos - qpos) <= WINDOW

    def aug(v, e):
        vlane = lax.broadcasted_iota(jnp.int32, v.shape, 1)
        own = (vlane < HEAD_DIM) if e == 0 else (vlane >= HEAD_DIM)
        return jnp.concatenate([jnp.where(own, v, jnp.zeros_like(v)), jnp.ones(v.shape, BF16)], axis=1)

    heads = [(pr, e) for pr in range(WG_KV_HEADS) for e in range(2)]
    psl = [slice(LANES * pr, LANES * (pr + 1)) for pr in range(WG_KV_HEADS)]
    sink = [sink_ref[layer, hq] * LOG2E for hq in range(WG_HEADS)]
    s_c, s_l, mx = [], [], []
    for pr, e in heads:
        qp = q_ref[0, :, psl[pr]]
        own = (lane < HEAD_DIM) if e == 0 else (lane >= HEAD_DIM)
        q = jnp.where(own, qp, jnp.zeros_like(qp))
        s_c.append(_dot_nt(q, kc_ref[0, :, psl[pr]]))
        if has_lat:
            s_l.append(jnp.where(band, _dot_nt(q, kl_ref[0, pl.ds(start, wk), psl[pr]]), NEG))
    for i in range(WG_HEADS):
        m = jnp.maximum(jnp.max(s_c[i], axis=-1, keepdims=True), sink[i])
        if has_lat:
            m = jnp.maximum(m, jnp.max(s_l[i], axis=-1, keepdims=True))
        mx.append(m)
    o2 = []
    for i, (pr, e) in enumerate(heads):
        o = _dot(jnp.exp2(s_c[i] - mx[i]).astype(BF16), aug(vc_ref[0, :, psl[pr]], e))
        if has_lat:
            o = o + _dot(jnp.exp2(s_l[i] - mx[i]).astype(BF16), aug(vl_ref[0, pl.ds(start, wk), psl[pr]], e))
        o2.append(o)
    outs = [o2[i][:, :LANES] / (o2[i][:, LANES:] + jnp.exp2(sink[i] - mx[i])) for i in range(WG_HEADS)]
    o_ref[0] = jnp.concatenate([outs[0] + outs[1], outs[2] + outs[3]], axis=-1).astype(o_ref.dtype)


def _win_attend(sink, layer, q, kc, vc, kl=None, vl=None, tq=256):
    b, n, _ = q.shape
    tq = min(tq, n)
    has_lat = kl is not None
    whole = lambda a: pl.BlockSpec((1,) + a.shape[1:], lambda bi, i: (bi, 0, 0))
    ins = [q, kc, vc] + ([kl, vl] if has_lat else [])
    specs = ([pl.BlockSpec(memory_space=pltpu.SMEM),
              pl.BlockSpec((1, tq, 256), lambda bi, i: (bi, i, 0))] + [whole(a) for a in ins[1:]])
    return pl.pallas_call(
        functools.partial(_win_kernel, has_lat=has_lat, n_lat=n, layer=layer),
        grid=(b, n // tq),
        in_specs=specs,
        out_specs=pl.BlockSpec((1, tq, 256), lambda bi, i: (bi, i, 0)),
        out_shape=jax.ShapeDtypeStruct((b, n, 256), BF16),
        compiler_params=_cparams(("parallel", "parallel")),
        name="win_attend",
    )(sink, *ins)


def _fourier_kernel(u_ref, cs_ref, dft_ref, o_ref, ab_ref, *, tr):
    n = u_ref.shape[1]
    u = u_ref[0]
    ab = _dot(u, cs_ref[...])
    ab_ref[0:n, :] = ab[:, :256].astype(BF16)
    ab_ref[n:2 * n, :] = ab[:, 256:].astype(BF16)
    scale = 1.0 / math.sqrt(n * FN_GROUP_W)

    def body(r, carry):
        r0 = pl.multiple_of(r * tr, tr)
        y = _dot(dft_ref[pl.ds(r0, tr), :], ab_ref[...])
        o_ref[0, pl.ds(r0, tr), :] = (y * scale).astype(o_ref.dtype)
        return carry

    lax.fori_loop(0, n // tr, body, 0)


def _fourier(u, cs, dft):
    b, n, w = u.shape
    tr = min(512, n)
    return pl.pallas_call(
        functools.partial(_fourier_kernel, tr=tr),
        grid=(b,),
        in_specs=[pl.BlockSpec((1, n, w), lambda bi: (bi, 0, 0)),
                  _const_spec(cs.shape),
                  _const_spec(dft.shape)],
        out_specs=pl.BlockSpec((1, n, w), lambda bi: (bi, 0, 0)),
        out_shape=jax.ShapeDtypeStruct((b, n, w), BF16),
        scratch_shapes=[pltpu.VMEM((2 * n, w), BF16)],
        compiler_params=_cparams(("parallel",)),
        name="fourier",
    )(u, cs, dft)


ML_ROWS = 2


def _split3(x):
    hi = x.astype(BF16).astype(F32)
    r1 = x - hi
    mid = r1.astype(BF16).astype(F32)
    lo = (r1 - mid).astype(BF16).astype(F32)
    return hi, mid, lo


def _mlstm_kernel(*refs, ctx_out):
    (uq_l, uk_l, vt_l, ot_l, gt_l, uq_c, uk_c, vt_c, ot_c, gt_c, wconv_ref, bgt_ref) = refs[:12]
    if ctx_out:
        y_l, y_c = refs[12:14]
        scratch = refs[14:]
    else:
        y_l = refs[12]
        y_c = None
        scratch = refs[13:]
    nrows = uq_l.shape[0]
    ns = 2 * nrows
    qs_l, ks_l, qs_c, ks_c = scratch[:4]
    hacc_l, hacc_c = scratch[4:4 + ns], scratch[4 + ns:4 + 2 * ns]
    cst = scratch[4 + 2 * ns:4 + 2 * ns + ns * ML_HEADS]
    mst = scratch[4 + 2 * ns + ns * ML_HEADS:]
    L = ML_CHUNK
    row = lax.broadcasted_iota(jnp.int32, (L, L), 0)
    col = lax.broadcasted_iota(jnp.int32, (L, L), 1)
    triu_b = (row <= col).astype(BF16)
    krow = lax.broadcasted_iota(jnp.int32, (32, L), 0)
    row2 = lax.broadcasted_iota(jnp.int32, (L, 256), 0)
    zeros8 = jnp.zeros((8, L), F32)

    def prep(x_ref, out_ref, w3, scale):
        n = x_ref.shape[1]
        nc = n // L

        def body(c, carry):
            r0 = pl.multiple_of(c * L, L)
            pstart = pl.multiple_of(jnp.maximum(r0 - 16, 0), 16)
            nstart = pl.multiple_of(jnp.minimum(r0 + L, n - 16), 16)
            for rr in range(nrows):
                xc = x_ref[rr, pl.ds(r0, L), :].astype(F32)
                prow = x_ref[rr, pl.ds(pstart, 16), :].astype(F32)[15:16, :] * jnp.where(c > 0, 1.0, 0.0)
                nrow = x_ref[rr, pl.ds(nstart, 16), :].astype(F32)[0:1, :] * jnp.where(c < nc - 1, 1.0, 0.0)
                xp = jnp.where(row2 == 0, prow, pltpu.roll(xc, 1, axis=0))
                xn = jnp.where(row2 == L - 1, nrow, pltpu.roll(xc, L - 1, axis=0))
                a = xp * w3[0:1, :] + xc * w3[1:2, :] + xn * w3[2:3, :]
                out_ref[rr, pl.ds(r0, L), :] = (a * _sigmoid(a) * scale).astype(BF16)
            return carry

        lax.fori_loop(0, nc, body, 0)

    wq3 = wconv_ref[0, :, 0:256]
    wk3 = wconv_ref[0, :, 256:512]
    kscale = ML_HEAD_DIM ** -0.5
    prep(uq_l, qs_l, wq3, 1.0)
    prep(uk_l, ks_l, wk3, kscale)
    prep(uq_c, qs_c, wq3, 1.0)
    prep(uk_c, ks_c, wk3, kscale)
    for ref in list(cst) + list(mst):
        ref[...] = jnp.zeros_like(ref)

    chains = [(si, hh) for si in range(ns) for hh in range(ML_HEADS)]
    own_l = [(col < ML_HEAD_DIM) if hh % 2 == 0 else (col >= ML_HEAD_DIM) for hh in range(ML_HEADS)]
    own_r = [(row < ML_HEAD_DIM) if hh % 2 == 0 else (row >= ML_HEAD_DIM) for hh in range(ML_HEADS)]
    mask = [row <= col, row >= col]
    sel = [jnp.where((krow & 7) == hh, 1.0, 0.0).astype(BF16) for hh in range(ML_HEADS)]

    def chunk_pair(qs, ks, vt, gt, hacc, cs, need_h):
        r0 = [pl.multiple_of(c * L, L) for c in cs]
        cum, tot, b, m_st, wk, decay, lhs = [], [], [], [], [], [], []
        for d in range(ns):
            rr, dd = d // 2, d % 2
            g = gt[rr, cs[dd]] + bgt_ref[0]
            li = g[16 * dd:16 * dd + 8]
            gf = g[16 * dd + 8:16 * dd + 16]
            ls = jnp.minimum(gf, 0.0) - jnp.log(1.0 + jnp.exp(-jnp.abs(gf)))
            cu = sum(_dot(piece.astype(BF16), triu_b) for piece in _split3(ls))
            tt = cu[:, L - 1:L]
            if dd == 1:
                cu = tt - cu + ls
            bb = li - cu
            ms = mst[d][...]
            lw = tt + bb
            m_new = jnp.maximum(tt + ms, jnp.max(lw, axis=1, keepdims=True))
            wk.append(jnp.exp(lw - m_new))
            decay.append(jnp.exp(tt + ms - m_new))
            mst[d][...] = m_new
            cum.append(cu)
            tot.append(tt)
            b.append(bb)
            m_st.append(ms)
            if need_h:
                lhs.append(jnp.concatenate(list(_split3(bb)) + [zeros8], axis=0).astype(BF16))
        kh, v_aug, c_aug, qh, kq, cq, bm = [], [], [], [], [], [], []
        for d, hh in chains:
            rr, dd = d // 2, d % 2
            psl = slice(LANES * (hh // 2), LANES * (hh // 2 + 1))
            kb = ks[rr, pl.ds(r0[dd], L), psl]
            kh.append(jnp.where(own_l[hh], kb, jnp.zeros_like(kb)))
            vb = vt[rr, cs[dd], psl, :]
            v_aug.append(jnp.where(own_r[hh], vb, jnp.ones_like(vb)))
            c_aug.append(cst[d * ML_HEADS + hh][...])
            if need_h:
                qb = qs[rr, pl.ds(r0[dd], L), psl]
                qh.append(jnp.where(own_l[hh], qb, jnp.zeros_like(qb)))
        if need_h:
            for i, (d, hh) in enumerate(chains):
                bm.append(jnp.where(mask[d % 2], _dot_tn(lhs[d], sel[hh]), NEG))
            for i in range(len(chains)):
                kq.append(_dot_nt(kh[i], qh[i]))
            for i in range(len(chains)):
                cq.append(_dot_nt(c_aug[i].astype(BF16), qh[i]))
            mu, st = [], []
            for i, (d, hh) in enumerate(chains):
                mu.append(jnp.maximum(m_st[d][hh:hh + 1, :], jnp.max(bm[i], axis=0, keepdims=True)))
            for i in range(len(chains)):
                st.append((kq[i] * jnp.exp(bm[i] - mu[i])).astype(BF16))
            r = []
            for i, (d, hh) in enumerate(chains):
                r.append(_dot(v_aug[i], st[i]) + cq[i] * jnp.exp(m_st[d][hh:hh + 1, :] - mu[i]))
            hs = []
            for i, (d, hh) in enumerate(chains):
                e_negm = jnp.exp(-(cum[d][hh:hh + 1, :] + mu[i]))
                den = pltpu.roll(r[i], ML_HEAD_DIM, axis=0)
                hs.append(r[i] / jnp.maximum(jnp.abs(den), e_negm))
            for d in range(ns):
                for p in range(2):
                    i = d * ML_HEADS + 2 * p
                    hacc[d][cs[d % 2], LANES * p:LANES * (p + 1), :] = jnp.where(row < ML_HEAD_DIM, hs[i], hs[i + 1])
        dc = []
        for i, (d, hh) in enumerate(chains):
            vw = (v_aug[i].astype(F32) * wk[d][hh:hh + 1, :]).astype(BF16)
            dc.append(_dot(vw, kh[i]))
        for i, (d, hh) in enumerate(chains):
            cst[i][...] = decay[d][hh:hh + 1, :] * c_aug[i] + dc[i]

    def scan(qs, ks, vt, gt, hacc, need_h):
        nc = gt.shape[1]

        def body(j, carry):
            chunk_pair(qs, ks, vt, gt, hacc, (j, nc - 1 - j), need_h)
            return carry

        lax.fori_loop(0, nc, body, 0)

    scan(qs_c, ks_c, vt_c, gt_c, hacc_c, ctx_out)
    scan(qs_l, ks_l, vt_l, gt_l, hacc_l, True)

    def finish(ot, hacc, y):
        def body(c, carry):
            for rr in range(nrows):
                gate = _sigmoid(ot[rr, c].astype(F32))
                y[rr, c] = (gate * (hacc[2 * rr][c] + hacc[2 * rr + 1][c])).astype(y.dtype)
            return carry

        lax.fori_loop(0, ot.shape[1], body, 0)

    finish(ot_l, hacc_l, y_l)
    if ctx_out:
        finish(ot_c, hacc_c, y_c)


def _mlstm(lat, ctx, pw, l, ctx_out):
    b, n, _ = lat[0].shape
    nc = ctx[0].shape[1]
    nr = ML_ROWS if b % ML_ROWS == 0 else 1
    ns = 2 * nr
    whole = lambda a: pl.BlockSpec((nr,) + a.shape[1:], lambda bi: (bi,) + (0,) * (a.ndim - 1))
    ins = list(lat) + list(ctx)
    tshape = lambda m: (b, m // LANES, 256, LANES)
    out_shape = [jax.ShapeDtypeStruct(tshape(n), BF16)]
    out_specs = [pl.BlockSpec((nr,) + tshape(n)[1:], lambda bi: (bi, 0, 0, 0))]
    if ctx_out:
        out_shape.append(jax.ShapeDtypeStruct(tshape(nc), BF16))
        out_specs.append(pl.BlockSpec((nr,) + tshape(nc)[1:], lambda bi: (bi, 0, 0, 0)))
    res = pl.pallas_call(
        functools.partial(_mlstm_kernel, ctx_out=ctx_out),
        grid=(b // nr,),
        in_specs=[whole(a) for a in ins] + [_layer_spec(pw["w_ml_conv"], l), _layer_spec(pw["bgt"], l)],
        out_specs=out_specs,
        out_shape=out_shape,
        scratch_shapes=[pltpu.VMEM((nr, n, 256), BF16), pltpu.VMEM((nr, n, 256), BF16),
                        pltpu.VMEM((nr, nc, 256), BF16), pltpu.VMEM((nr, nc, 256), BF16)]
                       + [pltpu.VMEM((n // LANES, 256, LANES), F32)] * ns
                       + [pltpu.VMEM((nc // LANES, 256, LANES), F32)] * ns
                       + [pltpu.VMEM((LANES, LANES), F32)] * (ns * ML_HEADS)
                       + [pltpu.VMEM((8, LANES), F32)] * ns,
        compiler_params=_cparams(("parallel",)),
        name="mlstm",
    )(*ins, pw["w_ml_conv"], pw["bgt"])
    return (res[0], res[1]) if ctx_out else (res[0], None)


MERGE_GROUPS = 2


def _merge_kernel(x_ref, mod_ref, ya_ref, yb_ref, yc_ref, yd_ref, gpre_ref, gpost_ref, gffn_ref,
                  wg_ref, bgate_ref, wb_ref, wo_ref, x1_ref, h2_ref):
    m = mod_ref[0, 0]
    tm = x_ref.shape[1]
    groups = min(MERGE_GROUPS, tm // LANES)
    th = tm // groups
    rows = [slice(a * th, (a + 1) * th) for a in range(groups)]
    xs = [x_ref[0, r, :] for r in rows]
    hbs = [(_rms(x, gpre_ref[0]) * (1.0 + m[1:2]) + m[0:1]).astype(BF16) for x in xs]
    cpg = th // LANES
    accs = [None] * groups
    for s, y_ref in enumerate((ya_ref, yb_ref, yc_ref, yd_ref)):
        for a, r in enumerate(rows):
            gate = _sigmoid(_dot(hbs[a], wg_ref[0, s]) + bgate_ref[0, s:s + 1, :])
            if s == 1:
                yb_t = jnp.concatenate([yb_ref[0, a * cpg + j] for j in range(cpg)], axis=1)
                branch = _dot_tn(yb_t, wb_ref[0, s])
            else:
                branch = _dot(y_ref[0, r, :], wb_ref[0, s])
            term = gate * branch
            accs[a] = term if accs[a] is None else accs[a] + term
    ys = [_dot(acc.astype(BF16), wo_ref[0]) for acc in accs]
    for a, r in enumerate(rows):
        x1 = xs[a] + m[2:3] * _rms(ys[a], gpost_ref[0])
        x1_ref[0, r, :] = x1
        h2_ref[0, r, :] = (_rms(x1, gffn_ref[0]) * (1.0 + m[4:5]) + m[3:4]).astype(BF16)


def _merge(x, mod, ctx_row, ys, pw, l, tm):
    b, n, d = x.shape
    tm = min(tm, n)
    row = lambda bi, i: (bi, i, 0)
    names = ("g_pre_mix", "g_post_mix", "g_pre_ffn", "wg", "b_gate", "wb", "wo")
    return pl.pallas_call(
        _merge_kernel,
        grid=(b, n // tm),
        in_specs=[pl.BlockSpec((1, tm, d), row), _mod_spec(l, ctx_row)]
                 + [pl.BlockSpec((1, tm, BRANCH_W), row),
                    pl.BlockSpec((1, tm // LANES, BRANCH_W, LANES), lambda bi, i: (bi, i, 0, 0)),
                    pl.BlockSpec((1, tm, BRANCH_W), row), pl.BlockSpec((1, tm, BRANCH_W), row)]
                 + [_layer_spec(pw[k], l) for k in names],
        out_specs=[pl.BlockSpec((1, tm, d), row), pl.BlockSpec((1, tm, d), row)],
        out_shape=[jax.ShapeDtypeStruct((b, n, d), F32), jax.ShapeDtypeStruct((b, n, d), BF16)],
        compiler_params=_cparams(("parallel", "parallel")),
        name="merge",
    )(x, mod, *ys, *[pw[k] for k in names])


FFN_HALO = 16
FFN_COL_CHUNKS = ((0, 1536), (1536, 2816))


def _ffn_kernel(h_ref, hp_ref, hn_ref, x1_ref, mod_ref, gpost_ref, wup_ref, wconv_ref, bconv_ref, wdown_ref,
                o_ref):
    i = pl.program_id(1)
    last = pl.num_programs(1) - 1
    tm = h_ref.shape[1]
    hm = h_ref[0]
    hp = jnp.where(i > 0, hp_ref[0], jnp.zeros_like(hp_ref[0]))
    hn = jnp.where(i < last, hn_ref[0], jnp.zeros_like(hn_ref[0]))
    hext = jnp.concatenate([hp, hm, hn], axis=0)
    ne = tm + 2 * FFN_HALO
    acc = None
    for lo, hi in FFN_COL_CHUNKS:
        a = _dot(hext, wup_ref[0, :, lo:hi])
        ap = pltpu.roll(a, 1, axis=0)[FFN_HALO:FFN_HALO + tm]
        an = pltpu.roll(a, ne - 1, axis=0)[FFN_HALO:FFN_HALO + tm]
        ac = a[FFN_HALO:FFN_HALO + tm]
        a = (ap * wconv_ref[0, 0:1, lo:hi] + ac * wconv_ref[0, 1:2, lo:hi] + an * wconv_ref[0, 2:3, lo:hi]
             + bconv_ref[0, :, lo:hi])
        v = _dot(hm, wup_ref[0, :, D_FF + lo:D_FF + hi])
        act = (a * _sigmoid(a) * v).astype(BF16)
        term = _dot(act, wdown_ref[0, lo:hi, :])
        acc = term if acc is None else acc + term
    m = mod_ref[0, 0]
    o_ref[0] = x1_ref[0] + m[5:6] * _rms(acc, gpost_ref[0])


def _ffn(h2, x1, mod, ctx_row, pw, l, tm):
    b, n, d = x1.shape
    tm = min(tm, n)
    row = lambda bi, i: (bi, i, 0)
    r = tm // FFN_HALO
    nblk = n // FFN_HALO
    names = ("g_post_ffn", "wup", "w_ffn_conv", "b_ffn_conv", "wdown")
    return pl.pallas_call(
        _ffn_kernel,
        grid=(b, n // tm),
        in_specs=[pl.BlockSpec((1, tm, d), row),
                  pl.BlockSpec((1, FFN_HALO, d), lambda bi, i: (bi, jnp.maximum(i * r - 1, 0), 0)),
                  pl.BlockSpec((1, FFN_HALO, d), lambda bi, i: (bi, jnp.minimum((i + 1) * r, nblk - 1), 0)),
                  pl.BlockSpec((1, tm, d), row),
                  _mod_spec(l, ctx_row)]
                 + [_layer_spec(pw[k], l) for k in names],
        out_specs=pl.BlockSpec((1, tm, d), row),
        out_shape=jax.ShapeDtypeStruct((b, n, d), F32),
        compiler_params=_cparams(("parallel", "parallel")),
        name="conv_ffn",
    )(h2, h2, h2, x1, mod, *[pw[k] for k in names])


def _rope_swap(d):
    half, nf = d // 2, d // 4
    idx = np.arange(d)
    j = idx % half
    return np.where(j < nf, idx + nf, idx - nf)


def _rope_cos_sin(n, d, identity):
    if identity:
        return jnp.ones((n, d), F32), jnp.zeros((n, d), F32)
    half, nf = d // 2, d // 4
    t = jnp.arange(n, dtype=jnp.int32)
    row = (t // GRID_W).astype(F32)[:, None]
    colp = (t % GRID_W).astype(F32)[:, None]
    inv = ROPE_BASE ** (-jnp.arange(nf, dtype=F32) / nf)
    ang = jnp.concatenate([row * inv, row * inv, colp * inv, colp * inv], axis=-1)
    sign = jnp.asarray(np.where(np.arange(d) % half < nf, -1.0, 1.0), F32)
    return jnp.cos(ang), jnp.sin(ang) * sign


def _rope_table(n, identity):
    ca, sa = _rope_cos_sin(n, MLA_ROPE, identity)
    cw, sw = _rope_cos_sin(n, HEAD_DIM, identity)
    scale_a = (MLA_NOPE + MLA_ROPE) ** -0.5 * LOG2E
    scale_w = HEAD_DIM ** -0.5 * LOG2E
    z32 = jnp.zeros((n, LANES - MLA_NOPE - MLA_ROPE), F32)
    one64 = jnp.ones((n, MLA_NOPE), F32)
    zero64 = jnp.zeros((n, MLA_NOPE), F32)
    qa_cos = jnp.concatenate([one64, ca, z32], -1) * scale_a
    qa_sin = jnp.concatenate([zero64, sa, z32], -1) * scale_a
    ka_cos = jnp.concatenate([zero64, ca, z32], -1)
    ka_sin = jnp.concatenate([zero64, sa, z32], -1)
    cw2 = jnp.concatenate([cw, cw], -1)
    sw2 = jnp.concatenate([sw, sw], -1)
    return jnp.concatenate([qa_cos, qa_sin, ka_cos, ka_sin, cw2 * scale_w, sw2 * scale_w, cw2, sw2], axis=-1)


def _dft_tables(n):
    n1 = n // FN_GROUP_W
    k = jnp.arange(n, dtype=jnp.int32)[:, None]
    ang1 = ((k * jnp.arange(n1, dtype=jnp.int32)[None, :]) % n1).astype(F32) * (2.0 * math.pi / n1)
    ang0 = ((k * jnp.arange(FN_GROUP_W, dtype=jnp.int32)[None, :]) % n).astype(F32) * (2.0 * math.pi / n)
    c1, s1 = jnp.cos(ang1)[:, :, None], jnp.sin(ang1)[:, :, None]
    c0, s0 = jnp.cos(ang0)[:, None, :], jnp.sin(ang0)[:, None, :]
    cos_kt = (c1 * c0 - s1 * s0).reshape(n, n)
    sin_kt = (s1 * c0 + c1 * s0).reshape(n, n)
    dft = jnp.concatenate([cos_kt, -sin_kt], axis=-1).astype(BF16)
    j = np.arange(FN_GROUP_W)
    a64 = (np.outer(j, j) % FN_GROUP_W) * (2.0 * np.pi / FN_GROUP_W)
    eye = np.eye(FN_GROUPS)
    cs = np.concatenate([np.kron(eye, np.cos(a64)), np.kron(eye, np.sin(a64))], axis=-1)
    return dft, jnp.asarray(cs, F32).astype(BF16)


def _w1_index():
    off = np.concatenate([[0], np.cumsum(IN_SPLITS)])
    z = int(off[-1])
    seg = lambda i: np.arange(off[i], off[i + 1])
    zeros = lambda w: np.full((w,), z)
    sw32, sw64 = _rope_swap(MLA_ROPE), _rope_swap(HEAD_DIM)
    place_kr = lambda t: np.concatenate([zeros(MLA_NOPE), t, zeros(LANES - MLA_NOPE - MLA_ROPE)])
    head = lambda i, h: off[i] + HEAD_DIM * h + np.arange(HEAD_DIM)
    dup = lambda i, perm: np.concatenate([head(i, h)[perm] for h in (0, 0, 1, 1)])
    ident = np.arange(HEAD_DIM)
    idx = np.concatenate([
        seg(0), seg(1), place_kr(seg(2)), place_kr(seg(2)[sw32]),
        seg(3), seg(4),
        seg(8), np.concatenate([head(8, h)[sw64] for h in range(WG_HEADS)]),
        dup(9, ident), dup(9, sw64), dup(10, ident), seg(11)])
    assert idx.shape == (C_END,)
    return idx.astype(np.int32)


def _w1t_index():
    off = np.concatenate([[0], np.cumsum(IN_SPLITS)])
    z = int(off[-1])
    gates = np.concatenate([np.concatenate([off[7] + ML_HEADS * k + np.arange(ML_HEADS), np.full((8 - ML_HEADS,), z)])
                            for k in range(4)])
    idx = np.concatenate([np.arange(off[5], off[7]), gates])
    assert idx.shape == (R_END,)
    return idx.astype(np.int32)


def _gate_bias_index():
    return np.concatenate([np.concatenate([ML_HEADS * k + np.arange(ML_HEADS), np.full((8 - ML_HEADS,), 4 * ML_HEADS)])
                           for k in range(4)]).astype(np.int32)


def _wq_index():
    dq = MLA_NOPE + MLA_ROPE
    z = MLA_HEADS * dq
    sw32 = _rope_swap(MLA_ROPE)
    pad = np.full((LANES - dq,), z)
    qa = [np.concatenate([dq * h + np.arange(dq), pad]) for h in range(MLA_HEADS)]
    qb = [np.concatenate([np.full((MLA_NOPE,), z), dq * h + MLA_NOPE + sw32, pad]) for h in range(MLA_HEADS)]
    return np.concatenate(qa + qb).astype(np.int32)


def _wkv_index():
    dkv = MLA_NOPE + MLA_V
    z = MLA_HEADS * dkv
    kn = [np.concatenate([dkv * h + np.arange(MLA_NOPE), np.full((LANES - MLA_NOPE,), z)]) for h in range(MLA_HEADS)]
    vv = [dkv * h + MLA_NOPE + np.arange(MLA_V) for h in range(MLA_HEADS)]
    return np.concatenate(kn + vv).astype(np.int32)


def _take_cols(w, idx):
    z = w.shape[-1]
    idx = np.asarray(idx)
    cuts = [0] + [i for i in range(1, len(idx)) if not ((idx[i] == idx[i - 1] + 1 and idx[i] != z)
                                                        or (idx[i] == z and idx[i - 1] == z))] + [len(idx)]
    parts = []
    for lo, hi in zip(cuts[:-1], cuts[1:]):
        if idx[lo] == z:
            parts.append(jnp.zeros(w.shape[:-1] + (hi - lo,), w.dtype))
        else:
            parts.append(w[..., int(idx[lo]):int(idx[hi - 1]) + 1])
    return jnp.concatenate(parts, axis=-1)


def _prep_weights(p):
    depth = p["w_in"].shape[0]
    vec = lambda a: a.reshape(depth, 1, a.shape[-1])
    bgt = jnp.broadcast_to(_take_cols(p["b_ml_gates"], _gate_bias_index())[:, :, None], (depth, GT_ROWS, LANES))
    return {
        "w1": _take_cols(p["w_in"], _w1_index()).astype(BF16),
        "w1t": jnp.swapaxes(_take_cols(p["w_in"], _w1t_index()), 1, 2).astype(BF16),
        "wq": _take_cols(p["w_uq"], _wq_index()).astype(BF16),
        "wkv": _take_cols(p["w_ukv"], _wkv_index()).astype(BF16),
        "g_qa": vec(p["g_qa"]), "g_kva": vec(p["g_kva"]),
        "g_pre_mix": vec(p["g_pre_mix"]), "g_post_mix": vec(p["g_post_mix"]),
        "g_pre_ffn": vec(p["g_pre_ffn"]), "g_post_ffn": vec(p["g_post_ffn"]),
        "w_ml_conv": p["w_ml_conv"], "bgt": bgt, "wg_sink": p["wg_sink"],
        "wg": p["w_gate"].astype(BF16), "b_gate": p["b_gate"],
        "wb": p["w_branch"].astype(BF16), "wo": p["w_out"].astype(BF16),
        "wup": p["w_up"].astype(BF16), "w_ffn_conv": p["w_ffn_conv"],
        "b_ffn_conv": vec(p["b_ffn_conv"]), "wdown": p["w_down"].astype(BF16),
    }


TM = 512


def kernel(x, c, ctx, c_ctx, w_mod, b_mod, g_pre_mix, g_post_mix, g_pre_ffn, g_post_ffn, w_in, g_qa, w_uq, g_kva,
           w_ukv, w_ml_conv, b_ml_gates, wg_sink, w_gate, b_gate, w_branch, w_out, w_up, w_ffn_conv, b_ffn_conv,
           w_down):
    p = dict(g_pre_mix=g_pre_mix, g_post_mix=g_post_mix, g_pre_ffn=g_pre_ffn, g_post_ffn=g_post_ffn, w_in=w_in,
             g_qa=g_qa, w_uq=w_uq, g_kva=g_kva, w_ukv=w_ukv, w_ml_conv=w_ml_conv, b_ml_gates=b_ml_gates,
             wg_sink=wg_sink, w_gate=w_gate, b_gate=b_gate, w_branch=w_branch, w_out=w_out, w_up=w_up,
             w_ffn_conv=w_ffn_conv, b_ffn_conv=b_ffn_conv, w_down=w_down)
    bsz, n, d = x.shape
    n_ctx = ctx.shape[1]
    depth = w_mod.shape[0]
    rows = -(-(bsz + 1) // 8) * 8
    c_all = jnp.concatenate([c, c_ctx[None, :], jnp.zeros((rows - bsz - 1, d), F32)], axis=0)
    mod = _modulation(c_all, w_mod, b_mod).reshape(depth, rows, 6, d)
    pw = _prep_weights(p)
    tab_l = _rope_table(n, identity=False)
    tab_c = _rope_table(n_ctx, identity=True)
    dft_l, cs64 = _dft_tables(n)
    dft_c, _ = _dft_tables(n_ctx)
    xc = ctx
    for l in range(depth):
        ctx_out = l < depth - 1
        qa, ka, va, uq, uk, vt, ot, gt, qw, kw, vw, uf = _inproj(x, mod, None, pw, l, tab_l, TM)
        qa_c, ka_c, va_c, uq_c, uk_c, vt_c, ot_c, gt_c, qw_c, kw_c, vw_c, uf_c = _inproj(
            xc, mod, bsz, pw, l, tab_c, TM)
        ya = _mla_attend(qa, ka_c, va_c, ka, va)
        yb, yb_c = _mlstm((uq, uk, vt, ot, gt), (uq_c, uk_c, vt_c, ot_c, gt_c), pw, l, ctx_out)
        yc = _win_attend(pw["wg_sink"], l, qw, kw_c, vw_c, kw, vw)
        yd = _fourier(uf, cs64, dft_l)
        x1, h2 = _merge(x, mod, None, (ya, yb, yc, yd), pw, l, TM)
        x = _ffn(h2, x1, mod, None, pw, l, TM)
        if ctx_out:
            ya_c = _mla_attend(qa_c, ka_c, va_c)
            yc_c = _win_attend(pw["wg_sink"], l, qw_c, kw_c, vw_c)
            yd_c = _fourier(uf_c, cs64, dft_c)
            xc1, hc2 = _merge(xc, mod, bsz, (ya_c, yb_c, yc_c, yd_c), pw, l, TM)
            xc = _ffn(hc2, xc1, mod, bsz, pw, l, TM)
    return x
```

```python
import functools
import math

import numpy as np
import jax
import jax.numpy as jnp
from jax import lax
from jax.experimental import pallas as pl
from jax.experimental.pallas import tpu as pltpu

F32 = jnp.float32
BF16 = jnp.bfloat16

D_MODEL = 1024
GRID_W = 64
N_BRANCH = 4
BRANCH_W = 256
MLA_HEADS = 4
MLA_NOPE = 64
MLA_ROPE = 32
MLA_V = 64
MLA_Q_RANK = 256
MLA_KV_RANK = 256
ML_HEADS = 4
ML_HEAD_DIM = 64
ML_CHUNK = 128
WG_HEADS = 4
WG_KV_HEADS = 2
HEAD_DIM = 64
WINDOW = 128
FN_GROUPS = 4
FN_GROUP_W = 64
D_FF = 2816
ROPE_BASE = 10000.0
EPS = 1e-6
IN_SPLITS = (MLA_Q_RANK, MLA_KV_RANK, MLA_ROPE, 256, 256, 256, 256, 16, 256, 128, 128, 256)

LANES = 128
NEG = -1e30
LOG2E = 1.4426950408889634
VMEM_LIMIT = 56 * 1024 * 1024

C_CQ, C_CKV, C_KRA, C_KRB, C_WQA, C_WQB, C_WKA, C_WKB, C_WV, C_UF, C_UQ, C_UK, C_END = (
    0, 256, 512, 640, 768, 1024, 1280, 1536, 1792, 2048, 2304, 2560, 2816)
IN_HALO = 16
R_UV, R_UO, R_GT, R_END = 0, 256, 512, 544
GT_ROWS = 32
T_QA_COS, T_QA_SIN, T_KA_COS, T_KA_SIN, T_QW_COS, T_QW_SIN, T_KW_COS, T_KW_SIN = range(8)


def _cparams(sem):
    return pltpu.CompilerParams(dimension_semantics=sem, vmem_limit_bytes=VMEM_LIMIT)


def _layer_spec(arr, l):
    nd = arr.ndim - 1
    return pl.BlockSpec((1,) + arr.shape[1:], lambda *_: (l,) + (0,) * nd, pipeline_mode=pl.Buffered(1))


def _const_spec(shape):
    nd = len(shape)
    return pl.BlockSpec(shape, lambda *_: (0,) * nd, pipeline_mode=pl.Buffered(1))


def _mod_spec(l, ctx_row):
    if ctx_row is None:
        return pl.BlockSpec((1, 1, 6, D_MODEL), lambda bi, *_: (l, bi, 0, 0))
    return pl.BlockSpec((1, 1, 6, D_MODEL), lambda *_: (l, ctx_row, 0, 0))


def _rms(x, g):
    ms = jnp.mean(x * x, axis=-1, keepdims=True)
    return x * lax.rsqrt(ms + EPS) * g


def _dot(a, b):
    return jnp.dot(a, b, preferred_element_type=F32)


def _dot_nt(a, b):
    return lax.dot_general(a, b, (((1,), (1,)), ((), ())), preferred_element_type=F32)


def _dot_tn(a, b):
    return lax.dot_general(a, b, (((0,), (0,)), ((), ())), preferred_element_type=F32)


def _sigmoid(x):
    return 1.0 / (1.0 + jnp.exp(-x))


def _mod_kernel(c_ref, w_ref, b_ref, o_ref):
    c = c_ref[...]
    s = (c * _sigmoid(c)).astype(BF16)
    o_ref[0] = _dot(s, w_ref[0].astype(BF16)) + b_ref[0]


def _modulation(c_all, w_mod, b_mod):
    depth, d, n = w_mod.shape
    rows = c_all.shape[0]
    tn = 1536
    return pl.pallas_call(
        _mod_kernel,
        grid=(depth, n // tn),
        in_specs=[pl.BlockSpec((rows, d), lambda l, j: (0, 0)),
                  pl.BlockSpec((1, d, tn), lambda l, j: (l, 0, j)),
                  pl.BlockSpec((1, 1, tn), lambda l, j: (l, 0, j))],
        out_specs=pl.BlockSpec((1, rows, tn), lambda l, j: (l, 0, j)),
        out_shape=jax.ShapeDtypeStruct((depth, rows, n), F32),
        compiler_params=_cparams(("parallel", "parallel")),
        name="modulation",
    )(c_all, w_mod, b_mod.reshape(depth, 1, n))


def _inproj_kernel(x_ref, xp_ref, xn_ref, mod_ref, gpre_ref, w1_ref, w1t_ref, gqa_ref, wq_ref, gkva_ref, wkv_ref,
                   wconv_ref, tab_ref,
                   qa_ref, ka_ref, va_ref, qs_ref, ks_ref, vt_ref, ot_ref, gt_ref,
                   qw_ref, kw_ref, vw_ref, uf_ref):
    i = pl.program_id(1)
    last = pl.num_programs(1) - 1
    tm = x_ref.shape[1]
    m = mod_ref[0, 0]

    def modulated(xv):
        return (_rms(xv, gpre_ref[0]) * (1.0 + m[1:2]) + m[0:1]).astype(BF16)

    hb = modulated(x_ref[0])
    u = _dot(hb, w1_ref[0, :, 0:C_UQ])

    hp = modulated(xp_ref[0])
    hn = modulated(xn_ref[0])
    hext = jnp.concatenate([jnp.where(i > 0, hp, jnp.zeros_like(hp)), hb,
                            jnp.where(i < last, hn, jnp.zeros_like(hn))], axis=0)
    uqk = _dot(hext, w1_ref[0, :, C_UQ:C_END])
    ne = tm + 2 * IN_HALO
    wc = wconv_ref[0]
    a = (pltpu.roll(uqk, 1, axis=0)[IN_HALO:IN_HALO + tm] * wc[0:1, :] + uqk[IN_HALO:IN_HALO + tm] * wc[1:2, :]
         + pltpu.roll(uqk, ne - 1, axis=0)[IN_HALO:IN_HALO + tm] * wc[2:3, :])
    a = a * _sigmoid(a)
    qs_ref[0] = a[:, :256].astype(BF16)
    ks_ref[0] = (a[:, 256:] * ML_HEAD_DIM ** -0.5).astype(BF16)

    def proj(lo, hi):
        return u[:, lo:hi]

    def proj_t(lo, hi, out_ref):
        r = _dot_nt(w1t_ref[0, lo:hi, :], hb)
        for j in range(out_ref.shape[1]):
            out_ref[0, j] = r[:, LANES * j:LANES * (j + 1)].astype(out_ref.dtype)

    def tab(j):
        return tab_ref[:, LANES * j:LANES * (j + 1)]

    def rep(t, n):
        return jnp.concatenate([t] * n, axis=-1)

    cq = _rms(proj(C_CQ, C_CKV), gqa_ref[0]).astype(BF16)
    q2 = _dot(cq, wq_ref[0])
    q = q2[:, :512] * rep(tab(T_QA_COS), 4) + q2[:, 512:] * rep(tab(T_QA_SIN), 4)
    qa_ref[0] = q.astype(BF16)
    ckv = _rms(proj(C_CKV, C_KRA), gkva_ref[0]).astype(BF16)
    kv = _dot(ckv, wkv_ref[0])
    kr = proj(C_KRA, C_KRB) * tab(T_KA_COS) + proj(C_KRB, C_WQA) * tab(T_KA_SIN)
    ka_ref[0] = (kv[:, :512] + rep(kr, 4)).astype(BF16)
    va_ref[0] = kv[:, 512:].astype(BF16)
    proj_t(R_UV, R_UO, vt_ref)
    proj_t(R_UO, R_GT, ot_ref)
    proj_t(R_GT, R_END, gt_ref)
    qw = proj(C_WQA, C_WQB) * rep(tab(T_QW_COS), 2) + proj(C_WQB, C_WKA) * rep(tab(T_QW_SIN), 2)
    qw_ref[0] = qw.astype(BF16)
    kw = proj(C_WKA, C_WKB) * rep(tab(T_KW_COS), 2) + proj(C_WKB, C_WV) * rep(tab(T_KW_SIN), 2)
    kw_ref[0] = kw.astype(BF16)
    vw_ref[0] = proj(C_WV, C_UF).astype(BF16)
    uf_ref[0] = proj(C_UF, C_UQ).astype(BF16)


def _inproj(x, mod, ctx_row, pw, l, tab, tm):
    b, n, d = x.shape
    tm = min(tm, n)
    row = lambda bi, i: (bi, i, 0)
    cpt = tm // LANES

    def out(w, dt):
        return jax.ShapeDtypeStruct((b, n, w), dt), pl.BlockSpec((1, tm, w), row)

    def out_t(w, dt):
        return (jax.ShapeDtypeStruct((b, n // LANES, w, LANES), dt),
                pl.BlockSpec((1, cpt, w, LANES), lambda bi, i: (bi, i, 0, 0)))

    outs = [out(512, BF16), out(512, BF16), out(256, BF16),
            out(256, BF16), out(256, BF16), out_t(256, BF16), out_t(256, BF16), out_t(GT_ROWS, F32),
            out(256, BF16), out(256, BF16), out(256, BF16), out(256, BF16)]
    names = ("g_pre_mix", "w1", "w1t", "g_qa", "wq", "g_kva", "wkv", "w_ml_conv")
    r = tm // IN_HALO
    nblk = n // IN_HALO
    return pl.pallas_call(
        _inproj_kernel,
        grid=(b, n // tm),
        in_specs=[pl.BlockSpec((1, tm, d), row),
                  pl.BlockSpec((1, IN_HALO, d), lambda bi, i: (bi, jnp.maximum(i * r - 1, 0), 0)),
                  pl.BlockSpec((1, IN_HALO, d), lambda bi, i: (bi, jnp.minimum((i + 1) * r, nblk - 1), 0)),
                  _mod_spec(l, ctx_row)]
                 + [_layer_spec(pw[k], l) for k in names]
                 + [pl.BlockSpec((tm, 8 * LANES), lambda bi, i: (i, 0))],
        out_specs=[o[1] for o in outs],
        out_shape=[o[0] for o in outs],
        compiler_params=_cparams(("parallel", "parallel")),
        name="inproj",
    )(x, x, x, mod, *[pw[k] for k in names], tab)


def _mla_kernel(*refs, has_lat):
    if has_lat:
        q_ref, kc_ref, vc_ref, kl_ref, vl_ref, o_ref, kcat, vaug = refs
    else:
        q_ref, kc_ref, vc_ref, o_ref, kcat, vaug = refs
    nc = kc_ref.shape[1]
    nk = kcat.shape[0]

    @pl.when(pl.program_id(1) == 0)
    def _():
        kcat[0:nc, :] = kc_ref[0]
        if has_lat:
            kcat[nc:nk, :] = kl_ref[0]
        for hh in range(MLA_HEADS):
            vsl = slice(LANES * (hh // 2), LANES * (hh // 2 + 1))

            def keep(v):
                lane = lax.broadcasted_iota(jnp.int32, v.shape, 1)
                own = (lane < MLA_V) if hh % 2 == 0 else (lane >= MLA_V)
                return jnp.where(own, v, jnp.zeros_like(v))

            vaug[hh, 0:nc, 0:LANES] = keep(vc_ref[0, :, vsl])
            if has_lat:
                vaug[hh, nc:nk, 0:LANES] = keep(vl_ref[0, :, vsl])
            vaug[hh, :, LANES:2 * LANES] = jnp.ones((nk, LANES), BF16)

    def logits(hh):
        hsl = slice(LANES * hh, LANES * (hh + 1))
        return _dot_nt(q_ref[0, :, hsl], kcat[:, hsl])

    def attend(hh, s):
        mx = jnp.max(s, axis=-1, keepdims=True)
        o2 = _dot(jnp.exp2(s - mx).astype(BF16), vaug[hh])
        return o2[:, :LANES] / o2[:, LANES:]

    outs = []
    s_next = logits(0)
    for hh in range(MLA_HEADS):
        s_cur = s_next
        if hh + 1 < MLA_HEADS:
            s_next = logits(hh + 1)
        outs.append(attend(hh, s_cur))
    o_ref[0] = jnp.concatenate([outs[0] + outs[1], outs[2] + outs[3]], axis=-1).astype(o_ref.dtype)


def _mla_attend(q, kc, vc, kl=None, vl=None, tq=512):
    b, n, _ = q.shape
    tq = min(tq, n)
    has_lat = kl is not None
    nk = kc.shape[1] + (kl.shape[1] if has_lat else 0)
    whole = lambda a: pl.BlockSpec((1,) + a.shape[1:], lambda bi, i: (bi, 0, 0))
    ins = [q, kc, vc] + ([kl, vl] if has_lat else [])
    specs = [pl.BlockSpec((1, tq, 512), lambda bi, i: (bi, i, 0))] + [whole(a) for a in ins[1:]]
    return pl.pallas_call(
        functools.partial(_mla_kernel, has_lat=has_lat),
        grid=(b, n // tq),
        in_specs=specs,
        out_specs=pl.BlockSpec((1, tq, 256), lambda bi, i: (bi, i, 0)),
        out_shape=jax.ShapeDtypeStruct((b, n, 256), BF16),
        scratch_shapes=[pltpu.VMEM((nk, 512), BF16), pltpu.VMEM((MLA_HEADS, nk, 2 * LANES), BF16)],
        compiler_params=_cparams(("parallel", "arbitrary")),
        name="mla_attend",
    )(*ins)


def _win_kernel(*refs, has_lat, n_lat, layer):
    if has_lat:
        sink_ref, q_ref, kc_ref, vc_ref, kl_ref, vl_ref, o_ref = refs
    else:
        sink_ref, q_ref, kc_ref, vc_ref, o_ref = refs
    tq = q_ref.shape[1]
    wk = tq + 2 * WINDOW
    lane = lax.broadcasted_iota(jnp.int32, (tq, LANES), 1)
    if has_lat:
        t0 = pl.program_id(1) * tq
        start = pl.multiple_of(jnp.clip(t0 - WINDOW, 0, n_lat - wk), LANES)
        qpos = t0 + lax.broadcasted_iota(jnp.int32, (tq, wk), 0)
        kpos = start + lax.broadcasted_iota(jnp.int32, (tq, wk), 1)
        band = jnp.abs(kpos - qpos) <= WINDOW

    def aug(v, e):
        vlane = lax.broadcasted_iota(jnp.int32, v.shape, 1)
        own = (vlane < HEAD_DIM) if e == 0 else (vlane >= HEAD_DIM)
        return jnp.concatenate([jnp.where(own, v, jnp.zeros_like(v)), jnp.ones(v.shape, BF16)], axis=1)

    heads = [(pr, e) for pr in range(WG_KV_HEADS) for e in range(2)]
    psl = [slice(LANES * pr, LANES * (pr + 1)) for pr in range(WG_KV_HEADS)]
    sink = [sink_ref[layer, hq] * LOG2E for hq in range(WG_HEADS)]
    s_c, s_l, mx = [], [], []
    for pr, e in heads:
        qp = q_ref[0, :, psl[pr]]
        own = (lane < HEAD_DIM) if e == 0 else (lane >= HEAD_DIM)
        q = jnp.where(own, qp, jnp.zeros_like(qp))
        s_c.append(_dot_nt(q, kc_ref[0, :, psl[pr]]))
        if has_lat:
            s_l.append(jnp.where(band, _dot_nt(q, kl_ref[0, pl.ds(start, wk), psl[pr]]), NEG))
    for i in range(WG_HEADS):
        m = jnp.maximum(jnp.max(s_c[i], axis=-1, keepdims=True), sink[i])
        if has_lat:
            m = jnp.maximum(m, jnp.max(s_l[i], axis=-1, keepdims=True))
        mx.append(m)
    o2 = []
    for i, (pr, e) in enumerate(heads):
        o = _dot(jnp.exp2(s_c[i] - mx[i]).astype(BF16), aug(vc_ref[0, :, psl[pr]], e))
        if has_lat:
            o = o + _dot(jnp.exp2(s_l[i] - mx[i]).astype(BF16), aug(vl_ref[0, pl.ds(start, wk), psl[pr]], e))
        o2.append(o)
    outs = [o2[i][:, :LANES] / (o2[i][:, LANES:] + jnp.exp2(sink[i] - mx[i])) for i in range(WG_HEADS)]
    o_ref[0] = jnp.concatenate([outs[0] + outs[1], outs[2] + outs[3]], axis=-1).astype(o_ref.dtype)


def _win_attend(sink, layer, q, kc, vc, kl=None, vl=None, tq=256):
    b, n, _ = q.shape
    tq = min(tq, n)
    has_lat = kl is not None
    whole = lambda a: pl.BlockSpec((1,) + a.shape[1:], lambda bi, i: (bi, 0, 0))
    ins = [q, kc, vc] + ([kl, vl] if has_lat else [])
    specs = ([pl.BlockSpec(memory_space=pltpu.SMEM),
              pl.BlockSpec((1, tq, 256), lambda bi, i: (bi, i, 0))] + [whole(a) for a in ins[1:]])
    return pl.pallas_call(
        functools.partial(_win_kernel, has_lat=has_lat, n_lat=n, layer=layer),
        grid=(b, n // tq),
        in_specs=specs,
        out_specs=pl.BlockSpec((1, tq, 256), lambda bi, i: (bi, i, 0)),
        out_shape=jax.ShapeDtypeStruct((b, n, 256), BF16),
        compiler_params=_cparams(("parallel", "parallel")),
        name="win_attend",
    )(sink, *ins)


def _fourier_kernel(u_ref, cs_ref, dft_ref, o_ref, ab_ref, *, tr):
    n = u_ref.shape[1]
    u = u_ref[0]
    ab = _dot(u, cs_ref[...])
    ab_ref[0:n, :] = ab[:, :256].astype(BF16)
    ab_ref[n:2 * n, :] = ab[:, 256:].astype(BF16)
    scale = 1.0 / math.sqrt(n * FN_GROUP_W)

    def body(r, carry):
        r0 = pl.multiple_of(r * tr, tr)
        y = _dot(dft_ref[pl.ds(r0, tr), :], ab_ref[...])
        o_ref[0, pl.ds(r0, tr), :] = (y * scale).astype(o_ref.dtype)
        return carry

    lax.fori_loop(0, n // tr, body, 0)


def _fourier(u, cs, dft):
    b, n, w = u.shape
    tr = min(512, n)
    return pl.pallas_call(
        functools.partial(_fourier_kernel, tr=tr),
        grid=(b,),
        in_specs=[pl.BlockSpec((1, n, w), lambda bi: (bi, 0, 0)),
                  _const_spec(cs.shape),
                  _const_spec(dft.shape)],
        out_specs=pl.BlockSpec((1, n, w), lambda bi: (bi, 0, 0)),
        out_shape=jax.ShapeDtypeStruct((b, n, w), BF16),
        scratch_shapes=[pltpu.VMEM((2 * n, w), BF16)],
        compiler_params=_cparams(("parallel",)),
        name="fourier",
    )(u, cs, dft)


ML_ROWS = 2


def _split3(x):
    hi = x.astype(BF16).astype(F32)
    r1 = x - hi
    mid = r1.astype(BF16).astype(F32)
    lo = (r1 - mid).astype(BF16).astype(F32)
    return hi, mid, lo


def _mlstm_kernel(*refs, ctx_out):
    (qs_l, ks_l, vt_l, ot_l, gt_l, qs_c, ks_c, vt_c, ot_c, gt_c, bgt_ref) = refs[:11]
    if ctx_out:
        y_l, y_c = refs[11:13]
        scratch = refs[13:]
    else:
        y_l = refs[11]
        y_c = None
        scratch = refs[12:]
    nrows = qs_l.shape[0]
    ns = 2 * nrows
    hacc_l, hacc_c = scratch[0:ns], scratch[ns:2 * ns]
    cst = scratch[2 * ns:2 * ns + ns * ML_HEADS]
    mst = scratch[2 * ns + ns * ML_HEADS:]
    L = ML_CHUNK
    row = lax.broadcasted_iota(jnp.int32, (L, L), 0)
    col = lax.broadcasted_iota(jnp.int32, (L, L), 1)
    triu_b = (row <= col).astype(BF16)
    krow = lax.broadcasted_iota(jnp.int32, (32, L), 0)
    zeros8 = jnp.zeros((8, L), F32)

    for ref in list(cst) + list(mst):
        ref[...] = jnp.zeros_like(ref)

    chains = [(si, hh) for si in range(ns) for hh in range(ML_HEADS)]
    own_l = [(col < ML_HEAD_DIM) if hh % 2 == 0 else (col >= ML_HEAD_DIM) for hh in range(ML_HEADS)]
    own_r = [(row < ML_HEAD_DIM) if hh % 2 == 0 else (row >= ML_HEAD_DIM) for hh in range(ML_HEADS)]
    mask = [row <= col, row >= col]
    sel = [jnp.where((krow & 7) == hh, 1.0, 0.0).astype(BF16) for hh in range(ML_HEADS)]

    def chunk_pair(qs, ks, vt, gt, hacc, cs, need_h):
        r0 = [pl.multiple_of(c * L, L) for c in cs]
        cum, tot, b, m_st, wk, decay, lhs = [], [], [], [], [], [], []
        for d in range(ns):
            rr, dd = d // 2, d % 2
            g = gt[rr, cs[dd]] + bgt_ref[0]
            li = g[16 * dd:16 * dd + 8]
            gf = g[16 * dd + 8:16 * dd + 16]
            ls = jnp.minimum(gf, 0.0) - jnp.log(1.0 + jnp.exp(-jnp.abs(gf)))
            cu = sum(_dot(piece.astype(BF16), triu_b) for piece in _split3(ls))
            tt = cu[:, L - 1:L]
            if dd == 1:
                cu = tt - cu + ls
            bb = li - cu
            ms = mst[d][...]
            lw = tt + bb
            m_new = jnp.maximum(tt + ms, jnp.max(lw, axis=1, keepdims=True))
            wk.append(jnp.exp(lw - m_new))
            decay.append(jnp.exp(tt + ms - m_new))
            mst[d][...] = m_new
            cum.append(cu)
            tot.append(tt)
            b.append(bb)
            m_st.append(ms)
            if need_h:
                lhs.append(jnp.concatenate(list(_split3(bb)) + [zeros8], axis=0).astype(BF16))
        kh, v_aug, c_aug, qh, kq, cq, bm = [], [], [], [], [], [], []
        for d, hh in chains:
            rr, dd = d // 2, d % 2
            psl = slice(LANES * (hh // 2), LANES * (hh // 2 + 1))
            kb = ks[rr, pl.ds(r0[dd], L), psl]
            kh.append(jnp.where(own_l[hh], kb, jnp.zeros_like(kb)))
            vb = vt[rr, cs[dd], psl, :]
            v_aug.append(jnp.where(own_r[hh], vb, jnp.ones_like(vb)))
            c_aug.append(cst[d * ML_HEADS + hh][...])
            if need_h:
                qb = qs[rr, pl.ds(r0[dd], L), psl]
                qh.append(jnp.where(own_l[hh], qb, jnp.zeros_like(qb)))
        if need_h:
            for i, (d, hh) in enumerate(chains):
                bm.append(jnp.where(mask[d % 2], _dot_tn(lhs[d], sel[hh]), NEG))
            for i in range(len(chains)):
                kq.append(_dot_nt(kh[i], qh[i]))
            for i in range(len(chains)):
                cq.append(_dot_nt(c_aug[i].astype(BF16), qh[i]))
            mu, st = [], []
            for i, (d, hh) in enumerate(chains):
                mu.append(jnp.maximum(m_st[d][hh:hh + 1, :], jnp.max(bm[i], axis=0, keepdims=True)))
            for i in range(len(chains)):
                st.append((kq[i] * jnp.exp(bm[i] - mu[i])).astype(BF16))
            r = []
            for i, (d, hh) in enumerate(chains):
                r.append(_dot(v_aug[i], st[i]) + cq[i] * jnp.exp(m_st[d][hh:hh + 1, :] - mu[i]))
            hs = []
            for i, (d, hh) in enumerate(chains):
                e_negm = jnp.exp(-(cum[d][hh:hh + 1, :] + mu[i]))
                den = pltpu.roll(r[i], ML_HEAD_DIM, axis=0)
                hs.append(r[i] / jnp.maximum(jnp.abs(den), e_negm))
            for d in range(ns):
                for p in range(2):
                    i = d * ML_HEADS + 2 * p
                    hacc[d][cs[d % 2], LANES * p:LANES * (p + 1), :] = jnp.where(row < ML_HEAD_DIM, hs[i], hs[i + 1])
        dc = []
        for i, (d, hh) in enumerate(chains):
            vw = (v_aug[i].astype(F32) * wk[d][hh:hh + 1, :]).astype(BF16)
            dc.append(_dot(vw, kh[i]))
        for i, (d, hh) in enumerate(chains):
            cst[i][...] = decay[d][hh:hh + 1, :] * c_aug[i] + dc[i]

    def scan(qs, ks, vt, gt, hacc, need_h):
        nc = gt.shape[1]

        def body(j, carry):
            chunk_pair(qs, ks, vt, gt, hacc, (j, nc - 1 - j), need_h)
            return carry

        lax.fori_loop(0, nc, body, 0, unroll=2)

    scan(qs_c, ks_c, vt_c, gt_c, hacc_c, ctx_out)
    scan(qs_l, ks_l, vt_l, gt_l, hacc_l, True)

    def finish(ot, hacc, y):
        def body(c, carry):
            for rr in range(nrows):
                gate = _sigmoid(ot[rr, c].astype(F32))
                y[rr, c] = (gate * (hacc[2 * rr][c] + hacc[2 * rr + 1][c])).astype(y.dtype)
            return carry

        lax.fori_loop(0, ot.shape[1], body, 0)

    finish(ot_l, hacc_l, y_l)
    if ctx_out:
        finish(ot_c, hacc_c, y_c)


def _mlstm(lat, ctx, pw, l, ctx_out):
    b, n, _ = lat[0].shape
    nc = ctx[0].shape[1]
    nr = ML_ROWS if b % ML_ROWS == 0 else 1
    ns = 2 * nr
    whole = lambda a: pl.BlockSpec((nr,) + a.shape[1:], lambda bi: (bi,) + (0,) * (a.ndim - 1))
    ins = list(lat) + list(ctx)
    tshape = lambda m: (b, m // LANES, 256, LANES)
    out_shape = [jax.ShapeDtypeStruct(tshape(n), BF16)]
    out_specs = [pl.BlockSpec((nr,) + tshape(n)[1:], lambda bi: (bi, 0, 0, 0))]
    if ctx_out:
        out_shape.append(jax.ShapeDtypeStruct(tshape(nc), BF16))
        out_specs.append(pl.BlockSpec((nr,) + tshape(nc)[1:], lambda bi: (bi, 0, 0, 0)))
    res = pl.pallas_call(
        functools.partial(_mlstm_kernel, ctx_out=ctx_out),
        grid=(b // nr,),
        in_specs=[whole(a) for a in ins] + [_layer_spec(pw["bgt"], l)],
        out_specs=out_specs,
        out_shape=out_shape,
        scratch_shapes=[pltpu.VMEM((n // LANES, 256, LANES), F32)] * ns
                       + [pltpu.VMEM((nc // LANES, 256, LANES), F32)] * ns
                       + [pltpu.VMEM((LANES, LANES), F32)] * (ns * ML_HEADS)
                       + [pltpu.VMEM((8, LANES), F32)] * ns,
        compiler_params=_cparams(("parallel",)),
        name="mlstm",
    )(*ins, pw["bgt"])
    return (res[0], res[1]) if ctx_out else (res[0], None)


MERGE_GROUPS = 2


def _merge_kernel(x_ref, mod_ref, ya_ref, yb_ref, yc_ref, yd_ref, gpre_ref, gpost_ref, gffn_ref,
                  wg_ref, bgate_ref, wb_ref, wo_ref, x1_ref, h2_ref):
    m = mod_ref[0, 0]
    tm = x_ref.shape[1]
    groups = min(MERGE_GROUPS, tm // LANES)
    th = tm // groups
    rows = [slice(a * th, (a + 1) * th) for a in range(groups)]
    xs = [x_ref[0, r, :] for r in rows]
    hbs = [(_rms(x, gpre_ref[0]) * (1.0 + m[1:2]) + m[0:1]).astype(BF16) for x in xs]
    cpg = th // LANES
    accs = [None] * groups
    for s, y_ref in enumerate((ya_ref, yb_ref, yc_ref, yd_ref)):
        for a, r in enumerate(rows):
            gate = _sigmoid(_dot(hbs[a], wg_ref[0, s]) + bgate_ref[0, s:s + 1, :])
            if s == 1:
                yb_t = jnp.concatenate([yb_ref[0, a * cpg + j] for j in range(cpg)], axis=1)
                branch = _dot_tn(yb_t, wb_ref[0, s])
            else:
                branch = _dot(y_ref[0, r, :], wb_ref[0, s])
            term = gate * branch
            accs[a] = term if accs[a] is None else accs[a] + term
    ys = [_dot(acc.astype(BF16), wo_ref[0]) for acc in accs]
    for a, r in enumerate(rows):
        x1 = xs[a] + m[2:3] * _rms(ys[a], gpost_ref[0])
        x1_ref[0, r, :] = x1
        h2_ref[0, r, :] = (_rms(x1, gffn_ref[0]) * (1.0 + m[4:5]) + m[3:4]).astype(BF16)


def _merge(x, mod, ctx_row, ys, pw, l, tm):
    b, n, d = x.shape
    tm = min(tm, n)
    row = lambda bi, i: (bi, i, 0)
    names = ("g_pre_mix", "g_post_mix", "g_pre_ffn", "wg", "b_gate", "wb", "wo")
    return pl.pallas_call(
        _merge_kernel,
        grid=(b, n // tm),
        in_specs=[pl.BlockSpec((1, tm, d), row), _mod_spec(l, ctx_row)]
                 + [pl.BlockSpec((1, tm, BRANCH_W), row),
                    pl.BlockSpec((1, tm // LANES, BRANCH_W, LANES), lambda bi, i: (bi, i, 0, 0)),
                    pl.BlockSpec((1, tm, BRANCH_W), row), pl.BlockSpec((1, tm, BRANCH_W), row)]
                 + [_layer_spec(pw[k], l) for k in names],
        out_specs=[pl.BlockSpec((1, tm, d), row), pl.BlockSpec((1, tm, d), row)],
        out_shape=[jax.ShapeDtypeStruct((b, n, d), F32), jax.ShapeDtypeStruct((b, n, d), BF16)],
        compiler_params=_cparams(("parallel", "parallel")),
        name="merge",
    )(x, mod, *ys, *[pw[k] for k in names])


FFN_HALO = 16
FFN_COL_CHUNKS = ((0, 1536), (1536, 2816))


def _ffn_kernel(h_ref, hp_ref, hn_ref, x1_ref, mod_ref, gpost_ref, wup_ref, wconv_ref, bconv_ref, wdown_ref,
                o_ref):
    i = pl.program_id(1)
    last = pl.num_programs(1) - 1
    tm = h_ref.shape[1]
    hm = h_ref[0]
    hp = jnp.where(i > 0, hp_ref[0], jnp.zeros_like(hp_ref[0]))
    hn = jnp.where(i < last, hn_ref[0], jnp.zeros_like(hn_ref[0]))
    hext = jnp.concatenate([hp, hm, hn], axis=0)
    ne = tm + 2 * FFN_HALO
    acc = None
    for lo, hi in FFN_COL_CHUNKS:
        a = _dot(hext, wup_ref[0, :, lo:hi])
        ap = pltpu.roll(a, 1, axis=0)[FFN_HALO:FFN_HALO + tm]
        an = pltpu.roll(a, ne - 1, axis=0)[FFN_HALO:FFN_HALO + tm]
        ac = a[FFN_HALO:FFN_HALO + tm]
        a = (ap * wconv_ref[0, 0:1, lo:hi] + ac * wconv_ref[0, 1:2, lo:hi] + an * wconv_ref[0, 2:3, lo:hi]
             + bconv_ref[0, :, lo:hi])
        v = _dot(hm, wup_ref[0, :, D_FF + lo:D_FF + hi])
        act = (a * _sigmoid(a) * v).astype(BF16)
        term = _dot(act, wdown_ref[0, lo:hi, :])
        acc = term if acc is None else acc + term
    m = mod_ref[0, 0]
    o_ref[0] = x1_ref[0] + m[5:6] * _rms(acc, gpost_ref[0])


def _ffn(h2, x1, mod, ctx_row, pw, l, tm):
    b, n, d = x1.shape
    tm = min(tm, n)
    row = lambda bi, i: (bi, i, 0)
    r = tm // FFN_HALO
    nblk = n // FFN_HALO
    names = ("g_post_ffn", "wup", "w_ffn_conv", "b_ffn_conv", "wdown")
    return pl.pallas_call(
        _ffn_kernel,
        grid=(b, n // tm),
        in_specs=[pl.BlockSpec((1, tm, d), row),
                  pl.BlockSpec((1, FFN_HALO, d), lambda bi, i: (bi, jnp.maximum(i * r - 1, 0), 0)),
                  pl.BlockSpec((1, FFN_HALO, d), lambda bi, i: (bi, jnp.minimum((i + 1) * r, nblk - 1), 0)),
                  pl.BlockSpec((1, tm, d), row),
                  _mod_spec(l, ctx_row)]
                 + [_layer_spec(pw[k], l) for k in names],
        out_specs=pl.BlockSpec((1, tm, d), row),
        out_shape=jax.ShapeDtypeStruct((b, n, d), F32),
        compiler_params=_cparams(("parallel", "parallel")),
        name="conv_ffn",
    )(h2, h2, h2, x1, mod, *[pw[k] for k in names])


def _rope_swap(d):
    half, nf = d // 2, d // 4
    idx = np.arange(d)
    j = idx % half
    return np.where(j < nf, idx + nf, idx - nf)


def _rope_cos_sin(n, d, identity):
    if identity:
        return jnp.ones((n, d), F32), jnp.zeros((n, d), F32)
    half, nf = d // 2, d // 4
    t = jnp.arange(n, dtype=jnp.int32)
    row = (t // GRID_W).astype(F32)[:, None]
    colp = (t % GRID_W).astype(F32)[:, None]
    inv = ROPE_BASE ** (-jnp.arange(nf, dtype=F32) / nf)
    ang = jnp.concatenate([row * inv, row * inv, colp * inv, colp * inv], axis=-1)
    sign = jnp.asarray(np.where(np.arange(d) % half < nf, -1.0, 1.0), F32)
    return jnp.cos(ang), jnp.sin(ang) * sign


def _rope_table(n, identity):
    ca, sa = _rope_cos_sin(n, MLA_ROPE, identity)
    cw, sw = _rope_cos_sin(n, HEAD_DIM, identity)
    scale_a = (MLA_NOPE + MLA_ROPE) ** -0.5 * LOG2E
    scale_w = HEAD_DIM ** -0.5 * LOG2E
    z32 = jnp.zeros((n, LANES - MLA_NOPE - MLA_ROPE), F32)
    one64 = jnp.ones((n, MLA_NOPE), F32)
    zero64 = jnp.zeros((n, MLA_NOPE), F32)
    qa_cos = jnp.concatenate([one64, ca, z32], -1) * scale_a
    qa_sin = jnp.concatenate([zero64, sa, z32], -1) * scale_a
    ka_cos = jnp.concatenate([zero64, ca, z32], -1)
    ka_sin = jnp.concatenate([zero64, sa, z32], -1)
    cw2 = jnp.concatenate([cw, cw], -1)
    sw2 = jnp.concatenate([sw, sw], -1)
    return jnp.concatenate([qa_cos, qa_sin, ka_cos, ka_sin, cw2 * scale_w, sw2 * scale_w, cw2, sw2], axis=-1)


def _dft_tables(n):
    n1 = n // FN_GROUP_W
    k = jnp.arange(n, dtype=jnp.int32)[:, None]
    ang1 = ((k * jnp.arange(n1, dtype=jnp.int32)[None, :]) % n1).astype(F32) * (2.0 * math.pi / n1)
    ang0 = ((k * jnp.arange(FN_GROUP_W, dtype=jnp.int32)[None, :]) % n).astype(F32) * (2.0 * math.pi / n)
    c1, s1 = jnp.cos(ang1)[:, :, None], jnp.sin(ang1)[:, :, None]
    c0, s0 = jnp.cos(ang0)[:, None, :], jnp.sin(ang0)[:, None, :]
    cos_kt = (c1 * c0 - s1 * s0).reshape(n, n)
    sin_kt = (s1 * c0 + c1 * s0).reshape(n, n)
    dft = jnp.concatenate([cos_kt, -sin_kt], axis=-1).astype(BF16)
    j = np.arange(FN_GROUP_W)
    a64 = (np.outer(j, j) % FN_GROUP_W) * (2.0 * np.pi / FN_GROUP_W)
    eye = np.eye(FN_GROUPS)
    cs = np.concatenate([np.kron(eye, np.cos(a64)), np.kron(eye, np.sin(a64))], axis=-1)
    return dft, jnp.asarray(cs, F32).astype(BF16)


def _w1_index():
    off = np.concatenate([[0], np.cumsum(IN_SPLITS)])
    z = int(off[-1])
    seg = lambda i: np.arange(off[i], off[i + 1])
    zeros = lambda w: np.full((w,), z)
    sw32, sw64 = _rope_swap(MLA_ROPE), _rope_swap(HEAD_DIM)
    place_kr = lambda t: np.concatenate([zeros(MLA_NOPE), t, zeros(LANES - MLA_NOPE - MLA_ROPE)])
    head = lambda i, h: off[i] + HEAD_DIM * h + np.arange(HEAD_DIM)
    dup = lambda i, perm: np.concatenate([head(i, h)[perm] for h in (0, 0, 1, 1)])
    ident = np.arange(HEAD_DIM)
    idx = np.concatenate([
        seg(0), seg(1), place_kr(seg(2)), place_kr(seg(2)[sw32]),
        seg(8), np.concatenate([head(8, h)[sw64] for h in range(WG_HEADS)]),
        dup(9, ident), dup(9, sw64), dup(10, ident), seg(11),
        seg(3), seg(4)])
    assert idx.shape == (C_END,)
    return idx.astype(np.int32)


def _w1t_index():
    off = np.concatenate([[0], np.cumsum(IN_SPLITS)])
    z = int(off[-1])
    gates = np.concatenate([np.concatenate([off[7] + ML_HEADS * k + np.arange(ML_HEADS), np.full((8 - ML_HEADS,), z)])
                            for k in range(4)])
    idx = np.concatenate([np.arange(off[5], off[7]), gates])
    assert idx.shape == (R_END,)
    return idx.astype(np.int32)


def _gate_bias_index():
    return np.concatenate([np.concatenate([ML_HEADS * k + np.arange(ML_HEADS), np.full((8 - ML_HEADS,), 4 * ML_HEADS)])
                           for k in range(4)]).astype(np.int32)


def _wq_index():
    dq = MLA_NOPE + MLA_ROPE
    z = MLA_HEADS * dq
    sw32 = _rope_swap(MLA_ROPE)
    pad = np.full((LANES - dq,), z)
    qa = [np.concatenate([dq * h + np.arange(dq), pad]) for h in range(MLA_HEADS)]
    qb = [np.concatenate([np.full((MLA_NOPE,), z), dq * h + MLA_NOPE + sw32, pad]) for h in range(MLA_HEADS)]
    return np.concatenate(qa + qb).astype(np.int32)


def _wkv_index():
    dkv = MLA_NOPE + MLA_V
    z = MLA_HEADS * dkv
    kn = [np.concatenate([dkv * h + np.arange(MLA_NOPE), np.full((LANES - MLA_NOPE,), z)]) for h in range(MLA_HEADS)]
    vv = [dkv * h + MLA_NOPE + np.arange(MLA_V) for h in range(MLA_HEADS)]
    return np.concatenate(kn + vv).astype(np.int32)


def _take_cols(w, idx):
    z = w.shape[-1]
    idx = np.asarray(idx)
    cuts = [0] + [i for i in range(1, len(idx)) if not ((idx[i] == idx[i - 1] + 1 and idx[i] != z)
                                                        or (idx[i] == z and idx[i - 1] == z))] + [len(idx)]
    parts = []
    for lo, hi in zip(cuts[:-1], cuts[1:]):
        if idx[lo] == z:
            parts.append(jnp.zeros(w.shape[:-1] + (hi - lo,), w.dtype))
        else:
            parts.append(w[..., int(idx[lo]):int(idx[hi - 1]) + 1])
    return jnp.concatenate(parts, axis=-1)


def _prep_weights(p):
    depth = p["w_in"].shape[0]
    vec = lambda a: a.reshape(depth, 1, a.shape[-1])
    bgt = jnp.broadcast_to(_take_cols(p["b_ml_gates"], _gate_bias_index())[:, :, None], (depth, GT_ROWS, LANES))
    return {
        "w1": _take_cols(p["w_in"], _w1_index()).astype(BF16),
        "w1t": jnp.swapaxes(_take_cols(p["w_in"], _w1t_index()), 1, 2).astype(BF16),
        "wq": _take_cols(p["w_uq"], _wq_index()).astype(BF16),
        "wkv": _take_cols(p["w_ukv"], _wkv_index()).astype(BF16),
        "g_qa": vec(p["g_qa"]), "g_kva": vec(p["g_kva"]),
        "g_pre_mix": vec(p["g_pre_mix"]), "g_post_mix": vec(p["g_post_mix"]),
        "g_pre_ffn": vec(p["g_pre_ffn"]), "g_post_ffn": vec(p["g_post_ffn"]),
        "w_ml_conv": p["w_ml_conv"], "bgt": bgt, "wg_sink": p["wg_sink"],
        "wg": p["w_gate"].astype(BF16), "b_gate": p["b_gate"],
        "wb": p["w_branch"].astype(BF16), "wo": p["w_out"].astype(BF16),
        "wup": p["w_up"].astype(BF16), "w_ffn_conv": p["w_ffn_conv"],
        "b_ffn_conv": vec(p["b_ffn_conv"]), "wdown": p["w_down"].astype(BF16),
    }


TM = 512


def kernel(x, c, ctx, c_ctx, w_mod, b_mod, g_pre_mix, g_post_mix, g_pre_ffn, g_post_ffn, w_in, g_qa, w_uq, g_kva,
           w_ukv, w_ml_conv, b_ml_gates, wg_sink, w_gate, b_gate, w_branch, w_out, w_up, w_ffn_conv, b_ffn_conv,
           w_down):
    p = dict(g_pre_mix=g_pre_mix, g_post_mix=g_post_mix, g_pre_ffn=g_pre_ffn, g_post_ffn=g_post_ffn, w_in=w_in,
             g_qa=g_qa, w_uq=w_uq, g_kva=g_kva, w_ukv=w_ukv, w_ml_conv=w_ml_conv, b_ml_gates=b_ml_gates,
             wg_sink=wg_sink, w_gate=w_gate, b_gate=b_gate, w_branch=w_branch, w_out=w_out, w_up=w_up,
             w_ffn_conv=w_ffn_conv, b_ffn_conv=b_ffn_conv, w_down=w_down)
    bsz, n, d = x.shape
    n_ctx = ctx.shape[1]
    depth = w_mod.shape[0]
    rows = -(-(bsz + 1) // 8) * 8
    c_all = jnp.concatenate([c, c_ctx[None, :], jnp.zeros((rows - bsz - 1, d), F32)], axis=0)
    mod = _modulation(c_all, w_mod, b_mod).reshape(depth, rows, 6, d)
    pw = _prep_weights(p)
    tab_l = _rope_table(n, identity=False)
    tab_c = _rope_table(n_ctx, identity=True)
    dft_l, cs64 = _dft_tables(n)
    dft_c, _ = _dft_tables(n_ctx)
    xc = ctx
    for l in range(depth):
        ctx_out = l < depth - 1
        qa, ka, va, uq, uk, vt, ot, gt, qw, kw, vw, uf = _inproj(x, mod, None, pw, l, tab_l, TM)
        qa_c, ka_c, va_c, uq_c, uk_c, vt_c, ot_c, gt_c, qw_c, kw_c, vw_c, uf_c = _inproj(
            xc, mod, bsz, pw, l, tab_c, TM)
        ya = _mla_attend(qa, ka_c, va_c, ka, va)
        yb, yb_c = _mlstm((uq, uk, vt, ot, gt), (uq_c, uk_c, vt_c, ot_c, gt_c), pw, l, ctx_out)
        yc = _win_attend(pw["wg_sink"], l, qw, kw_c, vw_c, kw, vw)
        yd = _fourier(uf, cs64, dft_l)
        x1, h2 = _merge(x, mod, None, (ya, yb, yc, yd), pw, l, TM)
        x = _ffn(h2, x1, mod, None, pw, l, TM)
        if ctx_out:
            ya_c = _mla_attend(qa_c, ka_c, va_c)
            yc_c = _win_attend(pw["wg_sink"], l, qw_c, kw_c, vw_c)
            yd_c = _fourier(uf_c, cs64, dft_c)
            xc1, hc2 = _merge(xc, mod, bsz, (ya_c, yb_c, yc_c, yd_c), pw, l, TM)
            xc = _ffn(hc2, xc1, mod, bsz, pw, l, TM)
    return x
```

```python
import functools
import math

import numpy as np
import jax
import jax.numpy as jnp
from jax import lax
from jax.experimental import pallas as pl
from jax.experimental.pallas import tpu as pltpu

F32 = jnp.float32
BF16 = jnp.bfloat16

D_MODEL = 1024
GRID_W = 64
N_BRANCH = 4
BRANCH_W = 256
MLA_HEADS = 4
MLA_NOPE = 64
MLA_ROPE = 32
MLA_V = 64
MLA_Q_RANK = 256
MLA_KV_RANK = 256
ML_HEADS = 4
ML_HEAD_DIM = 64
ML_CHUNK = 128
WG_HEADS = 4
WG_KV_HEADS = 2
HEAD_DIM = 64
WINDOW = 128
FN_GROUPS = 4
FN_GROUP_W = 64
D_FF = 2816
ROPE_BASE = 10000.0
EPS = 1e-6
IN_SPLITS = (MLA_Q_RANK, MLA_KV_RANK, MLA_ROPE, 256, 256, 256, 256, 16, 256, 128, 128, 256)

LANES = 128
NEG = -1e30
LOG2E = 1.4426950408889634
VMEM_LIMIT = 56 * 1024 * 1024

C_CQ, C_CKV, C_KR, C_WQ, C_WK, C_WV, C_UF, C_UQ, C_UK, C_END = (
    0, 256, 512, 640, 896, 1152, 1408, 1664, 1920, 2176)
IN_HALO = 16
R_UV, R_UO, R_GT, R_END = 0, 256, 512, 544
GT_ROWS = 32
T_QA_COS, T_QA_SIN, T_KA_COS, T_KA_SIN, T_QW_COS, T_QW_SIN, T_KW_COS, T_KW_SIN = range(8)


def _cparams(sem):
    return pltpu.CompilerParams(dimension_semantics=sem, vmem_limit_bytes=VMEM_LIMIT)


def _layer_spec(arr, l):
    nd = arr.ndim - 1
    return pl.BlockSpec((1,) + arr.shape[1:], lambda *_: (l,) + (0,) * nd, pipeline_mode=pl.Buffered(1))


def _const_spec(shape):
    nd = len(shape)
    return pl.BlockSpec(shape, lambda *_: (0,) * nd, pipeline_mode=pl.Buffered(1))


def _mod_spec(l, ctx_row):
    if ctx_row is None:
        return pl.BlockSpec((1, 1, 6, D_MODEL), lambda bi, *_: (l, bi, 0, 0))
    return pl.BlockSpec((1, 1, 6, D_MODEL), lambda *_: (l, ctx_row, 0, 0))


def _rms(x, g):
    ms = jnp.mean(x * x, axis=-1, keepdims=True)
    return x * lax.rsqrt(ms + EPS) * g


def _dot(a, b):
    return jnp.dot(a, b, preferred_element_type=F32)


def _dot_nt(a, b):
    return lax.dot_general(a, b, (((1,), (1,)), ((), ())), preferred_element_type=F32)


def _dot_tn(a, b):
    return lax.dot_general(a, b, (((0,), (0,)), ((), ())), preferred_element_type=F32)


def _sigmoid(x):
    return 1.0 / (1.0 + jnp.exp(-x))


def _mod_kernel(c_ref, w_ref, b_ref, o_ref):
    c = c_ref[...]
    s = (c * _sigmoid(c)).astype(BF16)
    o_ref[0] = _dot(s, w_ref[0].astype(BF16)) + b_ref[0]


def _modulation(c_all, w_mod, b_mod):
    depth, d, n = w_mod.shape
    rows = c_all.shape[0]
    tn = 1536
    return pl.pallas_call(
        _mod_kernel,
        grid=(depth, n // tn),
        in_specs=[pl.BlockSpec((rows, d), lambda l, j: (0, 0)),
                  pl.BlockSpec((1, d, tn), lambda l, j: (l, 0, j)),
                  pl.BlockSpec((1, 1, tn), lambda l, j: (l, 0, j))],
        out_specs=pl.BlockSpec((1, rows, tn), lambda l, j: (l, 0, j)),
        out_shape=jax.ShapeDtypeStruct((depth, rows, n), F32),
        compiler_params=_cparams(("parallel", "parallel")),
        name="modulation",
    )(c_all, w_mod, b_mod.reshape(depth, 1, n))


def _inproj_kernel(x_ref, xp_ref, xn_ref, mod_ref, gpre_ref, w1_ref, w1t_ref, gqa_ref, wq_ref, gkva_ref, wkv_ref,
                   wconv_ref, tab_ref,
                   qa_ref, ka_ref, va_ref, qs_ref, ks_ref, vt_ref, ot_ref, gt_ref,
                   qw_ref, kw_ref, vw_ref, uf_ref):
    i = pl.program_id(1)
    last = pl.num_programs(1) - 1
    tm = x_ref.shape[1]
    m = mod_ref[0, 0]

    def modulated(xv):
        return (_rms(xv, gpre_ref[0]) * (1.0 + m[1:2]) + m[0:1]).astype(BF16)

    hb = modulated(x_ref[0])
    u = _dot(hb, w1_ref[0, :, 0:C_UQ])

    hp = modulated(xp_ref[0])
    hn = modulated(xn_ref[0])
    hext = jnp.concatenate([jnp.where(i > 0, hp, jnp.zeros_like(hp)), hb,
                            jnp.where(i < last, hn, jnp.zeros_like(hn))], axis=0)
    uqk = _dot(hext, w1_ref[0, :, C_UQ:C_END])
    ne = tm + 2 * IN_HALO
    wc = wconv_ref[0]
    a = (pltpu.roll(uqk, 1, axis=0)[IN_HALO:IN_HALO + tm] * wc[0:1, :] + uqk[IN_HALO:IN_HALO + tm] * wc[1:2, :]
         + pltpu.roll(uqk, ne - 1, axis=0)[IN_HALO:IN_HALO + tm] * wc[2:3, :])
    a = a * _sigmoid(a)
    qs_ref[0] = a[:, :256].astype(BF16)
    ks_ref[0] = (a[:, 256:] * ML_HEAD_DIM ** -0.5).astype(BF16)

    def proj(lo, hi):
        return u[:, lo:hi]

    def proj_t(lo, hi, out_ref):
        r = _dot_nt(w1t_ref[0, lo:hi, :], hb)
        for j in range(out_ref.shape[1]):
            out_ref[0, j] = r[:, LANES * j:LANES * (j + 1)].astype(out_ref.dtype)

    def tab(j):
        return tab_ref[:, LANES * j:LANES * (j + 1)]

    lane = lax.broadcasted_iota(jnp.int32, (tm, LANES), 1)

    def rope(xv, cos_slot, sin_slot, nf):
        first = (lane & (2 * nf - 1)) < nf
        blocks = []
        for j in range(xv.shape[1] // LANES):
            xb = xv[:, LANES * j:LANES * (j + 1)]
            partner = jnp.where(first, pltpu.roll(xb, LANES - nf, axis=1), pltpu.roll(xb, nf, axis=1))
            blocks.append(xb * tab(cos_slot) + partner * tab(sin_slot))
        return blocks[0] if len(blocks) == 1 else jnp.concatenate(blocks, axis=-1)

    cq = _rms(proj(C_CQ, C_CKV), gqa_ref[0]).astype(BF16)
    qa_ref[0] = rope(_dot(cq, wq_ref[0]), T_QA_COS, T_QA_SIN, MLA_ROPE // 4).astype(BF16)
    ckv = _rms(proj(C_CKV, C_KR), gkva_ref[0]).astype(BF16)
    kv = _dot(ckv, wkv_ref[0])
    kr = rope(proj(C_KR, C_WQ), T_KA_COS, T_KA_SIN, MLA_ROPE // 4)
    ka_ref[0] = (kv[:, :512] + jnp.concatenate([kr] * MLA_HEADS, axis=-1)).astype(BF16)
    va_ref[0] = kv[:, 512:].astype(BF16)
    proj_t(R_UV, R_UO, vt_ref)
    proj_t(R_UO, R_GT, ot_ref)
    proj_t(R_GT, R_END, gt_ref)
    qw_ref[0] = rope(proj(C_WQ, C_WK), T_QW_COS, T_QW_SIN, HEAD_DIM // 4).astype(BF16)
    kw_ref[0] = rope(proj(C_WK, C_WV), T_KW_COS, T_KW_SIN, HEAD_DIM // 4).astype(BF16)
    vw_ref[0] = proj(C_WV, C_UF).astype(BF16)
    uf_ref[0] = proj(C_UF, C_UQ).astype(BF16)


def _inproj(x, mod, ctx_row, pw, l, tab, tm):
    b, n, d = x.shape
    tm = min(tm, n)
    row = lambda bi, i: (bi, i, 0)
    cpt = tm // LANES

    def out(w, dt):
        return jax.ShapeDtypeStruct((b, n, w), dt), pl.BlockSpec((1, tm, w), row)

    def out_t(w, dt):
        return (jax.ShapeDtypeStruct((b, n // LANES, w, LANES), dt),
                pl.BlockSpec((1, cpt, w, LANES), lambda bi, i: (bi, i, 0, 0)))

    outs = [out(512, BF16), out(512, BF16), out(256, BF16),
            out(256, BF16), out(256, BF16), out_t(256, BF16), out_t(256, BF16), out_t(GT_ROWS, F32),
            out(256, BF16), out(256, BF16), out(256, BF16), out(256, BF16)]
    names = ("g_pre_mix", "w1", "w1t", "g_qa", "wq", "g_kva", "wkv", "w_ml_conv")
    r = tm // IN_HALO
    nblk = n // IN_HALO
    return pl.pallas_call(
        _inproj_kernel,
        grid=(b, n // tm),
        in_specs=[pl.BlockSpec((1, tm, d), row),
                  pl.BlockSpec((1, IN_HALO, d), lambda bi, i: (bi, jnp.maximum(i * r - 1, 0), 0)),
                  pl.BlockSpec((1, IN_HALO, d), lambda bi, i: (bi, jnp.minimum((i + 1) * r, nblk - 1), 0)),
                  _mod_spec(l, ctx_row)]
                 + [_layer_spec(pw[k], l) for k in names]
                 + [pl.BlockSpec((tm, 8 * LANES), lambda bi, i: (i, 0))],
        out_specs=[o[1] for o in outs],
        out_shape=[o[0] for o in outs],
        compiler_params=_cparams(("parallel", "parallel")),
        name="inproj",
    )(x, x, x, mod, *[pw[k] for k in names], tab)


def _mla_kernel(*refs, has_lat):
    if has_lat:
        q_ref, kc_ref, vc_ref, kl_ref, vl_ref, o_ref, kcat, vaug = refs
    else:
        q_ref, kc_ref, vc_ref, o_ref, kcat, vaug = refs
    nc = kc_ref.shape[1]
    nk = kcat.shape[0]

    @pl.when(pl.program_id(1) == 0)
    def _():
        kcat[0:nc, :] = kc_ref[0]
        if has_lat:
            kcat[nc:nk, :] = kl_ref[0]
        for hh in range(MLA_HEADS):
            vsl = slice(LANES * (hh // 2), LANES * (hh // 2 + 1))

            def keep(v):
                lane = lax.broadcasted_iota(jnp.int32, v.shape, 1)
                own = (lane < MLA_V) if hh % 2 == 0 else (lane >= MLA_V)
                return jnp.where(own, v, jnp.zeros_like(v))

            vaug[hh, 0:nc, 0:LANES] = keep(vc_ref[0, :, vsl])
            if has_lat:
                vaug[hh, nc:nk, 0:LANES] = keep(vl_ref[0, :, vsl])
            vaug[hh, :, LANES:2 * LANES] = jnp.ones((nk, LANES), BF16)

    def logits(hh):
        hsl = slice(LANES * hh, LANES * (hh + 1))
        return _dot_nt(q_ref[0, :, hsl], kcat[:, hsl])

    def attend(hh, s):
        mx = jnp.max(s, axis=-1, keepdims=True)
        o2 = _dot(jnp.exp2(s - mx).astype(BF16), vaug[hh])
        return o2[:, :LANES] / o2[:, LANES:]

    outs = []
    s_next = logits(0)
    for hh in range(MLA_HEADS):
        s_cur = s_next
        if hh + 1 < MLA_HEADS:
            s_next = logits(hh + 1)
        outs.append(attend(hh, s_cur))
    o_ref[0] = jnp.concatenate([outs[0] + outs[1], outs[2] + outs[3]], axis=-1).astype(o_ref.dtype)


def _mla_attend(q, kc, vc, kl=None, vl=None, tq=512):
    b, n, _ = q.shape
    tq = min(tq, n)
    has_lat = kl is not None
    nk = kc.shape[1] + (kl.shape[1] if has_lat else 0)
    whole = lambda a: pl.BlockSpec((1,) + a.shape[1:], lambda bi, i: (bi, 0, 0))
    ins = [q, kc, vc] + ([kl, vl] if has_lat else [])
    specs = [pl.BlockSpec((1, tq, 512), lambda bi, i: (bi, i, 0))] + [whole(a) for a in ins[1:]]
    return pl.pallas_call(
        functools.partial(_mla_kernel, has_lat=has_lat),
        grid=(b, n // tq),
        in_specs=specs,
        out_specs=pl.BlockSpec((1, tq, 256), lambda bi, i: (bi, i, 0)),
        out_shape=jax.ShapeDtypeStruct((b, n, 256), BF16),
        scratch_shapes=[pltpu.VMEM((nk, 512), BF16), pltpu.VMEM((MLA_HEADS, nk, 2 * LANES), BF16)],
        compiler_params=_cparams(("parallel", "arbitrary")),
        name="mla_attend",
    )(*ins)


def _win_kernel(*refs, has_lat, n_lat, layer):
    if has_lat:
        sink_ref, q_ref, kc_ref, vc_ref, kl_ref, vl_ref, band_ref, o_ref = refs
    else:
        sink_ref, q_ref, kc_ref, vc_ref, o_ref = refs
    tq = q_ref.shape[1]
    wk = tq + 2 * WINDOW
    lane = lax.broadcasted_iota(jnp.int32, (tq, LANES), 1)
    if has_lat:
        t0 = pl.program_id(1) * tq
        start = pl.multiple_of(jnp.clip(t0 - WINDOW, 0, n_lat - wk), LANES)
        band = band_ref[0]

    def aug(v, e):
        vlane = lax.broadcasted_iota(jnp.int32, v.shape, 1)
        own = (vlane < HEAD_DIM) if e == 0 else (vlane >= HEAD_DIM)
        return jnp.concatenate([jnp.where(own, v, jnp.zeros_like(v)), jnp.ones(v.shape, BF16)], axis=1)

    heads = [(pr, e) for pr in range(WG_KV_HEADS) for e in range(2)]
    psl = [slice(LANES * pr, LANES * (pr + 1)) for pr in range(WG_KV_HEADS)]
    sink = [sink_ref[layer, hq] * LOG2E for hq in range(WG_HEADS)]
    s_c, s_l, mx = [], [], []
    for pr, e in heads:
        qp = q_ref[0, :, psl[pr]]
        own = (lane < HEAD_DIM) if e == 0 else (lane >= HEAD_DIM)
        q = jnp.where(own, qp, jnp.zeros_like(qp))
        s_c.append(_dot_nt(q, kc_ref[0, :, psl[pr]]))
        if has_lat:
            s_l.append(_dot_nt(q, kl_ref[0, pl.ds(start, wk), psl[pr]]) + band)
    for i in range(WG_HEADS):
        m = jnp.maximum(jnp.max(s_c[i], axis=-1, keepdims=True), sink[i])
        if has_lat:
            m = jnp.maximum(m, jnp.max(s_l[i], axis=-1, keepdims=True))
        mx.append(m)
    o2 = []
    for i, (pr, e) in enumerate(heads):
        o = _dot(jnp.exp2(s_c[i] - mx[i]).astype(BF16), aug(vc_ref[0, :, psl[pr]], e))
        if has_lat:
            o = o + _dot(jnp.exp2(s_l[i] - mx[i]).astype(BF16), aug(vl_ref[0, pl.ds(start, wk), psl[pr]], e))
        o2.append(o)
    outs = [o2[i][:, :LANES] / (o2[i][:, LANES:] + jnp.exp2(sink[i] - mx[i])) for i in range(WG_HEADS)]
    o_ref[0] = jnp.concatenate([outs[0] + outs[1], outs[2] + outs[3]], axis=-1).astype(o_ref.dtype)


def _win_attend(sink, layer, q, kc, vc, kl=None, vl=None, tq=256):
    b, n, _ = q.shape
    tq = min(tq, n)
    has_lat = kl is not None
    whole = lambda a: pl.BlockSpec((1,) + a.shape[1:], lambda bi, i: (bi, 0, 0))
    ins = [q, kc, vc] + ([kl, vl] if has_lat else [])
    specs = ([pl.BlockSpec(memory_space=pltpu.SMEM),
              pl.BlockSpec((1, tq, 256), lambda bi, i: (bi, i, 0))] + [whole(a) for a in ins[1:]])
    if has_lat:
        nt = n // tq
        wk = tq + 2 * WINDOW
        assert nt >= 2 and wk <= n
        rel = np.arange(wk)[None, :] - np.arange(tq)[:, None]
        shift = (0, WINDOW, 2 * WINDOW)
        band = np.stack([np.where(np.abs(rel - sh) <= WINDOW, 0.0, NEG) for sh in shift]).astype(np.float32)
        ins.append(jnp.asarray(band))
        specs.append(pl.BlockSpec((1, tq, wk), lambda bi, i: (jnp.where(i == 0, 0, jnp.where(i == nt - 1, 2, 1)), 0, 0)))
    return pl.pallas_call(
        functools.partial(_win_kernel, has_lat=has_lat, n_lat=n, layer=layer),
        grid=(b, n // tq),
        in_specs=specs,
        out_specs=pl.BlockSpec((1, tq, 256), lambda bi, i: (bi, i, 0)),
        out_shape=jax.ShapeDtypeStruct((b, n, 256), BF16),
        compiler_params=_cparams(("parallel", "parallel")),
        name="win_attend",
    )(sink, *ins)


def _fourier_kernel(u_ref, ur_ref, cs_ref, dft_ref, o_ref, ab_ref, *, tr):
    n = u_ref.shape[1]
    half = n // 2
    ab_a = _dot(u_ref[0, 0:half, :], cs_ref[...])
    ab_r = _dot(ur_ref[0], cs_ref[...])
    a_mid = _dot(u_ref[0, half:half + 16, :], cs_ref[:, 0:256])[0:1, :]
    first = lax.broadcasted_iota(jnp.int32, (half, 256), 0) == 0
    ab_ref[0:half, :] = (ab_a[:, :256] + ab_r[:, :256]).astype(BF16)
    ab_ref[half:n, :] = jnp.where(first, a_mid, ab_a[:, 256:] - ab_r[:, 256:]).astype(BF16)
    scale = 1.0 / math.sqrt(n * FN_GROUP_W)

    def body(r, carry):
        r0 = pl.multiple_of(r * tr, tr)
        y = _dot(dft_ref[pl.ds(r0, tr), :], ab_ref[...])
        o_ref[0, pl.ds(r0, tr), :] = (y * scale).astype(o_ref.dtype)
        return carry

    lax.fori_loop(0, n // tr, body, 0)


def _fourier(u, cs, dft):
    b, n, w = u.shape
    tr = min(512, n)
    half = n // 2
    u_rev = jnp.concatenate([jnp.zeros((b, 1, w), u.dtype), jnp.flip(u[:, half + 1:], axis=1)], axis=1)
    return pl.pallas_call(
        functools.partial(_fourier_kernel, tr=tr),
        grid=(b,),
        in_specs=[pl.BlockSpec((1, n, w), lambda bi: (bi, 0, 0)),
                  pl.BlockSpec((1, half, w), lambda bi: (bi, 0, 0)),
                  _const_spec(cs.shape),
                  _const_spec(dft.shape)],
        out_specs=pl.BlockSpec((1, n, w), lambda bi: (bi, 0, 0)),
        out_shape=jax.ShapeDtypeStruct((b, n, w), BF16),
        scratch_shapes=[pltpu.VMEM((n, w), BF16)],
        compiler_params=_cparams(("parallel",)),
        name="fourier",
    )(u, u_rev, cs, dft)


ML_ROWS = 2


def _split3(x):
    hi = x.astype(BF16).astype(F32)
    r1 = x - hi
    mid = r1.astype(BF16).astype(F32)
    lo = (r1 - mid).astype(BF16).astype(F32)
    return hi, mid, lo


def _mlstm_kernel(*refs, ctx_out):
    (qs_l, ks_l, vt_l, ot_l, gt_l, qs_c, ks_c, vt_c, ot_c, gt_c, bgt_ref) = refs[:11]
    if ctx_out:
        y_l, y_c = refs[11:13]
        scratch = refs[13:]
    else:
        y_l = refs[11]
        y_c = None
        scratch = refs[12:]
    nrows = qs_l.shape[0]
    ns = 2 * nrows
    hacc_l, hacc_c = scratch[0:ns], scratch[ns:2 * ns]
    cst = scratch[2 * ns:2 * ns + ns * ML_HEADS]
    mst = scratch[2 * ns + ns * ML_HEADS:]
    L = ML_CHUNK
    row = lax.broadcasted_iota(jnp.int32, (L, L), 0)
    col = lax.broadcasted_iota(jnp.int32, (L, L), 1)
    triu_b = (row <= col).astype(BF16)
    krow = lax.broadcasted_iota(jnp.int32, (32, L), 0)
    zeros8 = jnp.zeros((8, L), F32)

    for ref in list(cst) + list(mst):
        ref[...] = jnp.zeros_like(ref)

    chains = [(si, hh) for si in range(ns) for hh in range(ML_HEADS)]
    own_l = [(col < ML_HEAD_DIM) if hh % 2 == 0 else (col >= ML_HEAD_DIM) for hh in range(ML_HEADS)]
    own_r = [(row < ML_HEAD_DIM) if hh % 2 == 0 else (row >= ML_HEAD_DIM) for hh in range(ML_HEADS)]
    mask = [row <= col, row >= col]
    sel = [jnp.where((krow & 7) == hh, 1.0, 0.0).astype(BF16) for hh in range(ML_HEADS)]

    def chunk_pair(qs, ks, vt, gt, hacc, cs, need_h):
        r0 = [pl.multiple_of(c * L, L) for c in cs]
        cum, tot, b, m_st, wk, decay, lhs = [], [], [], [], [], [], []
        for d in range(ns):
            rr, dd = d // 2, d % 2
            g = gt[rr, cs[dd]] + bgt_ref[0]
            li = g[16 * dd:16 * dd + 8]
            gf = g[16 * dd + 8:16 * dd + 16]
            ls = jnp.minimum(gf, 0.0) - jnp.log(1.0 + jnp.exp(-jnp.abs(gf)))
            cu = sum(_dot(piece.astype(BF16), triu_b) for piece in _split3(ls))
            tt = cu[:, L - 1:L]
            if dd == 1:
                cu = tt - cu + ls
            bb = li - cu
            ms = mst[d][...]
            lw = tt + bb
            m_new = jnp.maximum(tt + ms, jnp.max(lw, axis=1, keepdims=True))
            wk.append(jnp.exp(lw - m_new))
            decay.append(jnp.exp(tt + ms - m_new))
            mst[d][...] = m_new
            cum.append(cu)
            tot.append(tt)
            b.append(bb)
            m_st.append(ms)
            if need_h:
                lhs.append(jnp.concatenate(list(_split3(bb)) + [zeros8], axis=0).astype(BF16))
        kh, v_aug, c_aug, qh, kq, cq, bm = [], [], [], [], [], [], []
        for d, hh in chains:
            rr, dd = d // 2, d % 2
            psl = slice(LANES * (hh // 2), LANES * (hh // 2 + 1))
            kh.append(ks[rr, pl.ds(r0[dd], L), psl])
            vb = vt[rr, cs[dd], psl, :]
            v_aug.append(jnp.where(own_r[hh], vb, jnp.ones_like(vb)))
            c_aug.append(cst[d * ML_HEADS + hh][...])
            if need_h:
                qb = qs[rr, pl.ds(r0[dd], L), psl]
                qh.append(jnp.where(own_l[hh], qb, jnp.zeros_like(qb)))
        if need_h:
            for i, (d, hh) in enumerate(chains):
                bm.append(jnp.where(mask[d % 2], _dot_tn(lhs[d], sel[hh]), NEG))
            for i in range(len(chains)):
                kq.append(_dot_nt(kh[i], qh[i]))
            for i in range(len(chains)):
                cq.append(_dot_nt(c_aug[i].astype(BF16), qh[i]))
            mu, st = [], []
            for i, (d, hh) in enumerate(chains):
                mu.append(jnp.maximum(m_st[d][hh:hh + 1, :], jnp.max(bm[i], axis=0, keepdims=True)))
            for i in range(len(chains)):
                st.append((kq[i] * jnp.exp(bm[i] - mu[i])).astype(BF16))
            r = []
            for i, (d, hh) in enumerate(chains):
                r.append(_dot(v_aug[i], st[i]) + cq[i] * jnp.exp(m_st[d][hh:hh + 1, :] - mu[i]))
            hs = []
            for i, (d, hh) in enumerate(chains):
                e_negm = jnp.exp(-(cum[d][hh:hh + 1, :] + mu[i]))
                den = pltpu.roll(r[i], ML_HEAD_DIM, axis=0)
                hs.append(r[i] / jnp.maximum(jnp.abs(den), e_negm))
            for d in range(ns):
                for p in range(2):
                    i = d * ML_HEADS + 2 * p
                    hacc[d][cs[d % 2], LANES * p:LANES * (p + 1), :] = jnp.where(row < ML_HEAD_DIM, hs[i], hs[i + 1])
        dc = []
        for i, (d, hh) in enumerate(chains):
            vw = (v_aug[i].astype(F32) * wk[d][hh:hh + 1, :]).astype(BF16)
            dc.append(_dot(vw, kh[i]))
        for i, (d, hh) in enumerate(chains):
            cst[i][...] = decay[d][hh:hh + 1, :] * c_aug[i] + dc[i]

    def scan(qs, ks, vt, gt, hacc, need_h):
        nc = gt.shape[1]

        def body(j, carry):
            chunk_pair(qs, ks, vt, gt, hacc, (j, nc - 1 - j), need_h)
            return carry

        lax.fori_loop(0, nc, body, 0, unroll=2)

    scan(qs_c, ks_c, vt_c, gt_c, hacc_c, ctx_out)
    scan(qs_l, ks_l, vt_l, gt_l, hacc_l, True)

    def finish(ot, hacc, y):
        def body(c, carry):
            for rr in range(nrows):
                gate = _sigmoid(ot[rr, c].astype(F32))
                y[rr, c] = (gate * (hacc[2 * rr][c] + hacc[2 * rr + 1][c])).astype(y.dtype)
            return carry

        lax.fori_loop(0, ot.shape[1], body, 0)

    finish(ot_l, hacc_l, y_l)
    if ctx_out:
        finish(ot_c, hacc_c, y_c)


def _mlstm(lat, ctx, pw, l, ctx_out):
    b, n, _ = lat[0].shape
    nc = ctx[0].shape[1]
    nr = ML_ROWS if b % ML_ROWS == 0 else 1
    ns = 2 * nr
    whole = lambda a: pl.BlockSpec((nr,) + a.shape[1:], lambda bi: (bi,) + (0,) * (a.ndim - 1))
    ins = list(lat) + list(ctx)
    tshape = lambda m: (b, m // LANES, 256, LANES)
    out_shape = [jax.ShapeDtypeStruct(tshape(n), BF16)]
    out_specs = [pl.BlockSpec((nr,) + tshape(n)[1:], lambda bi: (bi, 0, 0, 0))]
    if ctx_out:
        out_shape.append(jax.ShapeDtypeStruct(tshape(nc), BF16))
        out_specs.append(pl.BlockSpec((nr,) + tshape(nc)[1:], lambda bi: (bi, 0, 0, 0)))
    res = pl.pallas_call(
        functools.partial(_mlstm_kernel, ctx_out=ctx_out),
        grid=(b // nr,),
        in_specs=[whole(a) for a in ins] + [_layer_spec(pw["bgt"], l)],
        out_specs=out_specs,
        out_shape=out_shape,
        scratch_shapes=[pltpu.VMEM((n // LANES, 256, LANES), F32)] * ns
                       + [pltpu.VMEM((nc // LANES, 256, LANES), F32)] * ns
                       + [pltpu.VMEM((LANES, LANES), F32)] * (ns * ML_HEADS)
                       + [pltpu.VMEM((8, LANES), F32)] * ns,
        compiler_params=_cparams(("parallel",)),
        name="mlstm",
    )(*ins, pw["bgt"])
    return (res[0], res[1]) if ctx_out else (res[0], None)


MERGE_GROUPS = 2


def _merge_kernel(x_ref, mod_ref, ya_ref, yb_ref, yc_ref, yd_ref, gpre_ref, gpost_ref, gffn_ref,
                  wg_ref, bgate_ref, wb_ref, wo_ref, x1_ref, h2_ref):
    m = mod_ref[0, 0]
    tm = x_ref.shape[1]
    groups = min(MERGE_GROUPS, tm // LANES)
    th = tm // groups
    rows = [slice(a * th, (a + 1) * th) for a in range(groups)]
    xs = [x_ref[0, r, :] for r in rows]
    hbs = [(_rms(x, gpre_ref[0]) * (1.0 + m[1:2]) + m[0:1]).astype(BF16) for x in xs]
    cpg = th // LANES
    accs = [None] * groups
    for s, y_ref in enumerate((ya_ref, yb_ref, yc_ref, yd_ref)):
        for a, r in enumerate(rows):
            gate = _sigmoid(_dot(hbs[a], wg_ref[0, s]) + bgate_ref[0, s:s + 1, :])
            if s == 1:
                yb_t = jnp.concatenate([yb_ref[0, a * cpg + j] for j in range(cpg)], axis=1)
                branch = _dot_tn(yb_t, wb_ref[0, s])
            else:
                branch = _dot(y_ref[0, r, :], wb_ref[0, s])
            term = gate * branch
            accs[a] = term if accs[a] is None else accs[a] + term
    ys = [_dot(acc.astype(BF16), wo_ref[0]) for acc in accs]
    for a, r in enumerate(rows):
        x1 = xs[a] + m[2:3] * _rms(ys[a], gpost_ref[0])
        x1_ref[0, r, :] = x1
        h2_ref[0, r, :] = (_rms(x1, gffn_ref[0]) * (1.0 + m[4:5]) + m[3:4]).astype(BF16)


def _merge(x, mod, ctx_row, ys, pw, l, tm):
    b, n, d = x.shape
    tm = min(tm, n)
    row = lambda bi, i: (bi, i, 0)
    names = ("g_pre_mix", "g_post_mix", "g_pre_ffn", "wg", "b_gate", "wb", "wo")
    return pl.pallas_call(
        _merge_kernel,
        grid=(b, n // tm),
        in_specs=[pl.BlockSpec((1, tm, d), row), _mod_spec(l, ctx_row)]
                 + [pl.BlockSpec((1, tm, BRANCH_W), row),
                    pl.BlockSpec((1, tm // LANES, BRANCH_W, LANES), lambda bi, i: (bi, i, 0, 0)),
                    pl.BlockSpec((1, tm, BRANCH_W), row), pl.BlockSpec((1, tm, BRANCH_W), row)]
                 + [_layer_spec(pw[k], l) for k in names],
        out_specs=[pl.BlockSpec((1, tm, d), row), pl.BlockSpec((1, tm, d), row)],
        out_shape=[jax.ShapeDtypeStruct((b, n, d), F32), jax.ShapeDtypeStruct((b, n, d), BF16)],
        compiler_params=_cparams(("parallel", "parallel")),
        name="merge",
    )(x, mod, *ys, *[pw[k] for k in names])


FFN_HALO = 16
FFN_COL_CHUNKS = ((0, 1536), (1536, 2816))


def _ffn_kernel(h_ref, hp_ref, hn_ref, x1_ref, mod_ref, gpost_ref, wup_ref, wconv_ref, bconv_ref, wdown_ref,
                o_ref):
    i = pl.program_id(1)
    last = pl.num_programs(1) - 1
    tm = h_ref.shape[1]
    hm = h_ref[0]
    hp = jnp.where(i > 0, hp_ref[0], jnp.zeros_like(hp_ref[0]))
    hn = jnp.where(i < last, hn_ref[0], jnp.zeros_like(hn_ref[0]))
    hext = jnp.concatenate([hp, hm, hn], axis=0)
    ne = tm + 2 * FFN_HALO
    acc = None
    for lo, hi in FFN_COL_CHUNKS:
        a = _dot(hext, wup_ref[0, :, lo:hi])
        ap = pltpu.roll(a, 1, axis=0)[FFN_HALO:FFN_HALO + tm]
        an = pltpu.roll(a, ne - 1, axis=0)[FFN_HALO:FFN_HALO + tm]
        ac = a[FFN_HALO:FFN_HALO + tm]
        a = (ap * wconv_ref[0, 0:1, lo:hi] + ac * wconv_ref[0, 1:2, lo:hi] + an * wconv_ref[0, 2:3, lo:hi]
             + bconv_ref[0, :, lo:hi])
        v = _dot(hm, wup_ref[0, :, D_FF + lo:D_FF + hi])
        act = (a * _sigmoid(a) * v).astype(BF16)
        term = _dot(act, wdown_ref[0, lo:hi, :])
        acc = term if acc is None else acc + term
    m = mod_ref[0, 0]
    o_ref[0] = x1_ref[0] + m[5:6] * _rms(acc, gpost_ref[0])


def _ffn(h2, x1, mod, ctx_row, pw, l, tm):
    b, n, d = x1.shape
    tm = min(tm, n)
    row = lambda bi, i: (bi, i, 0)
    r = tm // FFN_HALO
    nblk = n // FFN_HALO
    names = ("g_post_ffn", "wup", "w_ffn_conv", "b_ffn_conv", "wdown")
    return pl.pallas_call(
        _ffn_kernel,
        grid=(b, n // tm),
        in_specs=[pl.BlockSpec((1, tm, d), row),
                  pl.BlockSpec((1, FFN_HALO, d), lambda bi, i: (bi, jnp.maximum(i * r - 1, 0), 0)),
                  pl.BlockSpec((1, FFN_HALO, d), lambda bi, i: (bi, jnp.minimum((i + 1) * r, nblk - 1), 0)),
                  pl.BlockSpec((1, tm, d), row),
                  _mod_spec(l, ctx_row)]
                 + [_layer_spec(pw[k], l) for k in names],
        out_specs=pl.BlockSpec((1, tm, d), row),
        out_shape=jax.ShapeDtypeStruct((b, n, d), F32),
        compiler_params=_cparams(("parallel", "parallel")),
        name="conv_ffn",
    )(h2, h2, h2, x1, mod, *[pw[k] for k in names])


def _rope_cos_sin(n, d, identity):
    if identity:
        return jnp.ones((n, d), F32), jnp.zeros((n, d), F32)
    half, nf = d // 2, d // 4
    t = jnp.arange(n, dtype=jnp.int32)
    row = (t // GRID_W).astype(F32)[:, None]
    colp = (t % GRID_W).astype(F32)[:, None]
    inv = ROPE_BASE ** (-jnp.arange(nf, dtype=F32) / nf)
    ang = jnp.concatenate([row * inv, row * inv, colp * inv, colp * inv], axis=-1)
    sign = jnp.asarray(np.where(np.arange(d) % half < nf, -1.0, 1.0), F32)
    return jnp.cos(ang), jnp.sin(ang) * sign


def _rope_table(n, identity):
    ca, sa = _rope_cos_sin(n, MLA_ROPE, identity)
    cw, sw = _rope_cos_sin(n, HEAD_DIM, identity)
    scale_a = (MLA_NOPE + MLA_ROPE) ** -0.5 * LOG2E
    scale_w = HEAD_DIM ** -0.5 * LOG2E
    z32 = jnp.zeros((n, LANES - MLA_NOPE - MLA_ROPE), F32)
    one64 = jnp.ones((n, MLA_NOPE), F32)
    zero64 = jnp.zeros((n, MLA_NOPE), F32)
    qa_cos = jnp.concatenate([one64, ca, z32], -1) * scale_a
    qa_sin = jnp.concatenate([zero64, sa, z32], -1) * scale_a
    ka_cos = jnp.concatenate([zero64, ca, z32], -1)
    ka_sin = jnp.concatenate([zero64, sa, z32], -1)
    cw2 = jnp.concatenate([cw, cw], -1)
    sw2 = jnp.concatenate([sw, sw], -1)
    return jnp.concatenate([qa_cos, qa_sin, ka_cos, ka_sin, cw2 * scale_w, sw2 * scale_w, cw2, sw2], axis=-1)


def _dft_tables(n):
    n1 = n // FN_GROUP_W
    k = jnp.arange(n, dtype=jnp.int32)[:, None]
    ang1 = ((k * jnp.arange(n1, dtype=jnp.int32)[None, :]) % n1).astype(F32) * (2.0 * math.pi / n1)
    ang0 = ((k * jnp.arange(FN_GROUP_W, dtype=jnp.int32)[None, :]) % n).astype(F32) * (2.0 * math.pi / n)
    c1, s1 = jnp.cos(ang1)[:, :, None], jnp.sin(ang1)[:, :, None]
    c0, s0 = jnp.cos(ang0)[:, None, :], jnp.sin(ang0)[:, None, :]
    cos_kt = (c1 * c0 - s1 * s0).reshape(n, n)
    sin_kt = (s1 * c0 + c1 * s0).reshape(n, n)
    half = n // 2
    col = jnp.arange(half, dtype=jnp.int32)[None, :]
    dft = jnp.concatenate([cos_kt[:, :half], jnp.where(col == 0, cos_kt[:, half:half + 1], -sin_kt[:, :half])],
                          axis=-1).astype(BF16)
    j = np.arange(FN_GROUP_W)
    a64 = (np.outer(j, j) % FN_GROUP_W) * (2.0 * np.pi / FN_GROUP_W)
    eye = np.eye(FN_GROUPS)
    cs = np.concatenate([np.kron(eye, np.cos(a64)), np.kron(eye, np.sin(a64))], axis=-1)
    return dft, jnp.asarray(cs, F32).astype(BF16)


def _w1_index():
    off = np.concatenate([[0], np.cumsum(IN_SPLITS)])
    z = int(off[-1])
    seg = lambda i: np.arange(off[i], off[i + 1])
    zeros = lambda w: np.full((w,), z)
    head = lambda i, h: off[i] + HEAD_DIM * h + np.arange(HEAD_DIM)
    dup = lambda i: np.concatenate([head(i, h) for h in (0, 0, 1, 1)])
    idx = np.concatenate([
        seg(0), seg(1), zeros(MLA_NOPE), seg(2), zeros(LANES - MLA_NOPE - MLA_ROPE),
        seg(8), dup(9), dup(10), seg(11),
        seg(3), seg(4)])
    assert idx.shape == (C_END,)
    return idx.astype(np.int32)


def _w1t_index():
    off = np.concatenate([[0], np.cumsum(IN_SPLITS)])
    z = int(off[-1])
    gates = np.concatenate([np.concatenate([off[7] + ML_HEADS * k + np.arange(ML_HEADS), np.full((8 - ML_HEADS,), z)])
                            for k in range(4)])
    idx = np.concatenate([np.arange(off[5], off[7]), gates])
    assert idx.shape == (R_END,)
    return idx.astype(np.int32)


def _gate_bias_index():
    return np.concatenate([np.concatenate([ML_HEADS * k + np.arange(ML_HEADS), np.full((8 - ML_HEADS,), 4 * ML_HEADS)])
                           for k in range(4)]).astype(np.int32)


def _wq_index():
    dq = MLA_NOPE + MLA_ROPE
    z = MLA_HEADS * dq
    pad = np.full((LANES - dq,), z)
    return np.concatenate([np.concatenate([dq * h + np.arange(dq), pad]) for h in range(MLA_HEADS)]).astype(np.int32)


def _wkv_index():
    dkv = MLA_NOPE + MLA_V
    z = MLA_HEADS * dkv
    kn = [np.concatenate([dkv * h + np.arange(MLA_NOPE), np.full((LANES - MLA_NOPE,), z)]) for h in range(MLA_HEADS)]
    vv = [dkv * h + MLA_NOPE + np.arange(MLA_V) for h in range(MLA_HEADS)]
    return np.concatenate(kn + vv).astype(np.int32)


def _take_cols(w, idx):
    z = w.shape[-1]
    idx = np.asarray(idx)
    cuts = [0] + [i for i in range(1, len(idx)) if not ((idx[i] == idx[i - 1] + 1 and idx[i] != z)
                                                        or (idx[i] == z and idx[i - 1] == z))] + [len(idx)]
    parts = []
    for lo, hi in zip(cuts[:-1], cuts[1:]):
        if idx[lo] == z:
            parts.append(jnp.zeros(w.shape[:-1] + (hi - lo,), w.dtype))
        else:
            parts.append(w[..., int(idx[lo]):int(idx[hi - 1]) + 1])
    return jnp.concatenate(parts, axis=-1)


def _prep_weights(p):
    depth = p["w_in"].shape[0]
    vec = lambda a: a.reshape(depth, 1, a.shape[-1])
    bgt = jnp.broadcast_to(_take_cols(p["b_ml_gates"], _gate_bias_index())[:, :, None], (depth, GT_ROWS, LANES))
    return {
        "w1": _take_cols(p["w_in"], _w1_index()).astype(BF16),
        "w1t": jnp.swapaxes(_take_cols(p["w_in"], _w1t_index()), 1, 2).astype(BF16),
        "wq": _take_cols(p["w_uq"], _wq_index()).astype(BF16),
        "wkv": _take_cols(p["w_ukv"], _wkv_index()).astype(BF16),
        "g_qa": vec(p["g_qa"]), "g_kva": vec(p["g_kva"]),
        "g_pre_mix": vec(p["g_pre_mix"]), "g_post_mix": vec(p["g_post_mix"]),
        "g_pre_ffn": vec(p["g_pre_ffn"]), "g_post_ffn": vec(p["g_post_ffn"]),
        "w_ml_conv": p["w_ml_conv"], "bgt": bgt, "wg_sink": p["wg_sink"],
        "wg": p["w_gate"].astype(BF16), "b_gate": p["b_gate"],
        "wb": p["w_branch"].astype(BF16), "wo": p["w_out"].astype(BF16),
        "wup": p["w_up"].astype(BF16), "w_ffn_conv": p["w_ffn_conv"],
        "b_ffn_conv": vec(p["b_ffn_conv"]), "wdown": p["w_down"].astype(BF16),
    }


TM = 512


def kernel(x, c, ctx, c_ctx, w_mod, b_mod, g_pre_mix, g_post_mix, g_pre_ffn, g_post_ffn, w_in, g_qa, w_uq, g_kva,
           w_ukv, w_ml_conv, b_ml_gates, wg_sink, w_gate, b_gate, w_branch, w_out, w_up, w_ffn_conv, b_ffn_conv,
           w_down):
    p = dict(g_pre_mix=g_pre_mix, g_post_mix=g_post_mix, g_pre_ffn=g_pre_ffn, g_post_ffn=g_post_ffn, w_in=w_in,
             g_qa=g_qa, w_uq=w_uq, g_kva=g_kva, w_ukv=w_ukv, w_ml_conv=w_ml_conv, b_ml_gates=b_ml_gates,
             wg_sink=wg_sink, w_gate=w_gate, b_gate=b_gate, w_branch=w_branch, w_out=w_out, w_up=w_up,
             w_ffn_conv=w_ffn_conv, b_ffn_conv=b_ffn_conv, w_down=w_down)
    bsz, n, d = x.shape
    n_ctx = ctx.shape[1]
    depth = w_mod.shape[0]
    rows = -(-(bsz + 1) // 8) * 8
    c_all = jnp.concatenate([c, c_ctx[None, :], jnp.zeros((rows - bsz - 1, d), F32)], axis=0)
    mod = _modulation(c_all, w_mod, b_mod).reshape(depth, rows, 6, d)
    pw = _prep_weights(p)
    tab_l = _rope_table(n, identity=False)
    tab_c = _rope_table(n_ctx, identity=True)
    dft_l, cs64 = _dft_tables(n)
    dft_c, _ = _dft_tables(n_ctx)
    xc = ctx
    for l in range(depth):
        ctx_out = l < depth - 1
        qa, ka, va, uq, uk, vt, ot, gt, qw, kw, vw, uf = _inproj(x, mod, None, pw, l, tab_l, TM)
        qa_c, ka_c, va_c, uq_c, uk_c, vt_c, ot_c, gt_c, qw_c, kw_c, vw_c, uf_c = _inproj(
            xc, mod, bsz, pw, l, tab_c, TM)
        ya = _mla_attend(qa, ka_c, va_c, ka, va)
        yb, yb_c = _mlstm((uq, uk, vt, ot, gt), (uq_c, uk_c, vt_c, ot_c, gt_c), pw, l, ctx_out)
        yc = _win_attend(pw["wg_sink"], l, qw, kw_c, vw_c, kw, vw)
        yd = _fourier(uf, cs64, dft_l)
        x1, h2 = _merge(x, mod, None, (ya, yb, yc, yd), pw, l, TM)
        x = _ffn(h2, x1, mod, None, pw, l, TM)
        if ctx_out:
            ya_c = _mla_attend(qa_c, ka_c, va_c)
            yc_c = _win_attend(pw["wg_sink"], l, qw_c, kw_c, vw_c)
            yd_c = _fourier(uf_c, cs64, dft_c)
            xc1, hc2 = _merge(xc, mod, bsz, (ya_c, yb_c, yc_c, yd_c), pw, l, TM)
            xc = _ffn(hc2, xc1, mod, bsz, pw, l, TM)
    return x
```

```python
import functools
import math

import numpy as np
import jax
import jax.numpy as jnp
from jax import lax
from jax.experimental import pallas as pl
from jax.experimental.pallas import tpu as pltpu

F32 = jnp.float32
BF16 = jnp.bfloat16

D_MODEL = 1024
GRID_W = 64
N_BRANCH = 4
BRANCH_W = 256
MLA_HEADS = 4
MLA_NOPE = 64
MLA_ROPE = 32
MLA_V = 64
MLA_Q_RANK = 256
MLA_KV_RANK = 256
ML_HEADS = 4
ML_HEAD_DIM = 64
ML_CHUNK = 128
WG_HEADS = 4
WG_KV_HEADS = 2
HEAD_DIM = 64
WINDOW = 128
FN_GROUPS = 4
FN_GROUP_W = 64
D_FF = 2816
ROPE_BASE = 10000.0
EPS = 1e-6
IN_SPLITS = (MLA_Q_RANK, MLA_KV_RANK, MLA_ROPE, 256, 256, 256, 256, 16, 256, 128, 128, 256)

LANES = 128
NEG = -1e30
LOG2E = 1.4426950408889634
VMEM_LIMIT = 56 * 1024 * 1024

C_CQ, C_CKV, C_KR, C_WQ, C_WK, C_WV, C_UF, C_UQ, C_UK, C_END = (
    0, 256, 512, 640, 896, 1152, 1408, 1664, 1920, 2176)
IN_HALO = 16
R_UV, R_UO, R_GT, R_END = 0, 256, 512, 544
GT_ROWS = 32
T_QA_COS, T_QA_SIN, T_KA_COS, T_KA_SIN, T_QW_COS, T_QW_SIN, T_KW_COS, T_KW_SIN = range(8)


def _cparams(sem):
    return pltpu.CompilerParams(dimension_semantics=sem, vmem_limit_bytes=VMEM_LIMIT)


def _layer_spec(arr, l):
    nd = arr.ndim - 1
    return pl.BlockSpec((1,) + arr.shape[1:], lambda *_: (l,) + (0,) * nd, pipeline_mode=pl.Buffered(1))


def _const_spec(shape):
    nd = len(shape)
    return pl.BlockSpec(shape, lambda *_: (0,) * nd, pipeline_mode=pl.Buffered(1))


def _mod_spec(l, ctx_row):
    if ctx_row is None:
        return pl.BlockSpec((1, 1, 6, D_MODEL), lambda bi, *_: (l, bi, 0, 0))
    return pl.BlockSpec((1, 1, 6, D_MODEL), lambda *_: (l, ctx_row, 0, 0))


def _rms(x, g):
    ms = jnp.mean(x * x, axis=-1, keepdims=True)
    return x * lax.rsqrt(ms + EPS) * g


def _dot(a, b):
    return jnp.dot(a, b, preferred_element_type=F32)


def _dot_nt(a, b):
    return lax.dot_general(a, b, (((1,), (1,)), ((), ())), preferred_element_type=F32)


def _dot_tn(a, b):
    return lax.dot_general(a, b, (((0,), (0,)), ((), ())), preferred_element_type=F32)


def _sigmoid(x):
    return 1.0 / (1.0 + jnp.exp(-x))


def _mod_kernel(c_ref, w_ref, b_ref, o_ref):
    c = c_ref[...]
    s = (c * _sigmoid(c)).astype(BF16)
    o_ref[0] = _dot(s, w_ref[0].astype(BF16)) + b_ref[0]


def _modulation(c_all, w_mod, b_mod):
    depth, d, n = w_mod.shape
    rows = c_all.shape[0]
    tn = 1536
    return pl.pallas_call(
        _mod_kernel,
        grid=(depth, n // tn),
        in_specs=[pl.BlockSpec((rows, d), lambda l, j: (0, 0)),
                  pl.BlockSpec((1, d, tn), lambda l, j: (l, 0, j)),
                  pl.BlockSpec((1, 1, tn), lambda l, j: (l, 0, j))],
        out_specs=pl.BlockSpec((1, rows, tn), lambda l, j: (l, 0, j)),
        out_shape=jax.ShapeDtypeStruct((depth, rows, n), F32),
        compiler_params=_cparams(("parallel", "parallel")),
        name="modulation",
    )(c_all, w_mod, b_mod.reshape(depth, 1, n))


def _inproj_kernel(x_ref, xp_ref, xn_ref, mod_ref, gpre_ref, w1_ref, w1t_ref, gqa_ref, wq_ref, gkva_ref, wkv_ref,
                   wconv_ref, tab_ref,
                   qa_ref, ka_ref, va_ref, qs_ref, ks_ref, vt_ref, ot_ref, gt_ref,
                   qw_ref, kw_ref, vw_ref, uf_ref):
    i = pl.program_id(1)
    last = pl.num_programs(1) - 1
    tm = x_ref.shape[1]
    m = mod_ref[0, 0]

    def modulated(xv):
        return (_rms(xv, gpre_ref[0]) * (1.0 + m[1:2]) + m[0:1]).astype(BF16)

    hb = modulated(x_ref[0])
    u = _dot(hb, w1_ref[0, :, 0:C_UQ])

    hp = modulated(xp_ref[0])
    hn = modulated(xn_ref[0])
    hext = jnp.concatenate([jnp.where(i > 0, hp, jnp.zeros_like(hp)), hb,
                            jnp.where(i < last, hn, jnp.zeros_like(hn))], axis=0)
    uqk = _dot(hext, w1_ref[0, :, C_UQ:C_END])
    ne = tm + 2 * IN_HALO
    wc = wconv_ref[0]
    a = (pltpu.roll(uqk, 1, axis=0)[IN_HALO:IN_HALO + tm] * wc[0:1, :] + uqk[IN_HALO:IN_HALO + tm] * wc[1:2, :]
         + pltpu.roll(uqk, ne - 1, axis=0)[IN_HALO:IN_HALO + tm] * wc[2:3, :])
    a = a * _sigmoid(a)
    qs_ref[0] = a[:, :256].astype(BF16)
    ks_ref[0] = (a[:, 256:] * ML_HEAD_DIM ** -0.5).astype(BF16)

    def proj(lo, hi):
        return u[:, lo:hi]

    def proj_t(lo, hi, out_ref):
        r = _dot_nt(w1t_ref[0, lo:hi, :], hb)
        for j in range(out_ref.shape[1]):
            out_ref[0, j] = r[:, LANES * j:LANES * (j + 1)].astype(out_ref.dtype)

    def tab(j):
        return tab_ref[:, LANES * j:LANES * (j + 1)]

    lane = lax.broadcasted_iota(jnp.int32, (tm, LANES), 1)

    def rope(xv, cos_slot, sin_slot, nf):
        first = (lane & (2 * nf - 1)) < nf
        blocks = []
        for j in range(xv.shape[1] // LANES):
            xb = xv[:, LANES * j:LANES * (j + 1)]
            partner = jnp.where(first, pltpu.roll(xb, LANES - nf, axis=1), pltpu.roll(xb, nf, axis=1))
            blocks.append(xb * tab(cos_slot) + partner * tab(sin_slot))
        return blocks[0] if len(blocks) == 1 else jnp.concatenate(blocks, axis=-1)

    cq = _rms(proj(C_CQ, C_CKV), gqa_ref[0]).astype(BF16)
    qa_ref[0] = rope(_dot(cq, wq_ref[0]), T_QA_COS, T_QA_SIN, MLA_ROPE // 4).astype(BF16)
    ckv = _rms(proj(C_CKV, C_KR), gkva_ref[0]).astype(BF16)
    kv = _dot(ckv, wkv_ref[0])
    kr = rope(proj(C_KR, C_WQ), T_KA_COS, T_KA_SIN, MLA_ROPE // 4)
    ka_ref[0] = (kv[:, :512] + jnp.concatenate([kr] * MLA_HEADS, axis=-1)).astype(BF16)
    va_ref[0] = kv[:, 512:].astype(BF16)
    proj_t(R_UV, R_UO, vt_ref)
    proj_t(R_UO, R_GT, ot_ref)
    proj_t(R_GT, R_END, gt_ref)
    qw_ref[0] = rope(proj(C_WQ, C_WK), T_QW_COS, T_QW_SIN, HEAD_DIM // 4).astype(BF16)
    kw_ref[0] = rope(proj(C_WK, C_WV), T_KW_COS, T_KW_SIN, HEAD_DIM // 4).astype(BF16)
    vw_ref[0] = proj(C_WV, C_UF).astype(BF16)
    uf_ref[0] = proj(C_UF, C_UQ).astype(BF16)


def _inproj(x, mod, ctx_row, pw, l, tab, tm):
    b, n, d = x.shape
    tm = min(tm, n)
    row = lambda bi, i: (bi, i, 0)
    cpt = tm // LANES

    def out(w, dt):
        return jax.ShapeDtypeStruct((b, n, w), dt), pl.BlockSpec((1, tm, w), row)

    def out_t(w, dt):
        return (jax.ShapeDtypeStruct((b, n // LANES, w, LANES), dt),
                pl.BlockSpec((1, cpt, w, LANES), lambda bi, i: (bi, i, 0, 0)))

    outs = [out(512, BF16), out(512, BF16), out(256, BF16),
            out(256, BF16), out(256, BF16), out_t(256, BF16), out_t(256, BF16), out_t(GT_ROWS, F32),
            out(256, BF16), out(256, BF16), out(256, BF16), out(256, BF16)]
    names = ("g_pre_mix", "w1", "w1t", "g_qa", "wq", "g_kva", "wkv", "w_ml_conv")
    r = tm // IN_HALO
    nblk = n // IN_HALO
    return pl.pallas_call(
        _inproj_kernel,
        grid=(b, n // tm),
        in_specs=[pl.BlockSpec((1, tm, d), row),
                  pl.BlockSpec((1, IN_HALO, d), lambda bi, i: (bi, jnp.maximum(i * r - 1, 0), 0)),
                  pl.BlockSpec((1, IN_HALO, d), lambda bi, i: (bi, jnp.minimum((i + 1) * r, nblk - 1), 0)),
                  _mod_spec(l, ctx_row)]
                 + [_layer_spec(pw[k], l) for k in names]
                 + [pl.BlockSpec((tm, 8 * LANES), lambda bi, i: (i, 0))],
        out_specs=[o[1] for o in outs],
        out_shape=[o[0] for o in outs],
        compiler_params=_cparams(("parallel", "parallel")),
        name="inproj",
    )(x, x, x, mod, *[pw[k] for k in names], tab)


def _mla_kernel(*refs, has_lat):
    if has_lat:
        q_ref, kc_ref, vc_ref, kl_ref, vl_ref, o_ref, kcat, vaug = refs
    else:
        q_ref, kc_ref, vc_ref, o_ref, kcat, vaug = refs
    nc = kc_ref.shape[1]
    nk = kcat.shape[0]

    @pl.when(pl.program_id(1) == 0)
    def _():
        kcat[0:nc, :] = kc_ref[0]
        if has_lat:
            kcat[nc:nk, :] = kl_ref[0]
        for hh in range(MLA_HEADS):
            vsl = slice(LANES * (hh // 2), LANES * (hh // 2 + 1))

            def keep(v):
                lane = lax.broadcasted_iota(jnp.int32, v.shape, 1)
                own = (lane < MLA_V) if hh % 2 == 0 else (lane >= MLA_V)
                return jnp.where(own, v, jnp.zeros_like(v))

            vaug[hh, 0:nc, 0:LANES] = keep(vc_ref[0, :, vsl])
            if has_lat:
                vaug[hh, nc:nk, 0:LANES] = keep(vl_ref[0, :, vsl])
            vaug[hh, :, LANES:2 * LANES] = jnp.ones((nk, LANES), BF16)

    def logits(hh):
        hsl = slice(LANES * hh, LANES * (hh + 1))
        return _dot_nt(q_ref[0, :, hsl], kcat[:, hsl])

    def attend(hh, s):
        mx = jnp.max(s, axis=-1, keepdims=True)
        o2 = _dot(jnp.exp2(s - mx).astype(BF16), vaug[hh])
        return o2[:, :LANES] / o2[:, LANES:]

    outs = []
    s_next = logits(0)
    for hh in range(MLA_HEADS):
        s_cur = s_next
        if hh + 1 < MLA_HEADS:
            s_next = logits(hh + 1)
        outs.append(attend(hh, s_cur))
    o_ref[0] = jnp.concatenate([outs[0] + outs[1], outs[2] + outs[3]], axis=-1).astype(o_ref.dtype)


def _mla_attend(q, kc, vc, kl=None, vl=None, tq=512):
    b, n, _ = q.shape
    tq = min(tq, n)
    has_lat = kl is not None
    nk = kc.shape[1] + (kl.shape[1] if has_lat else 0)
    whole = lambda a: pl.BlockSpec((1,) + a.shape[1:], lambda bi, i: (bi, 0, 0))
    ins = [q, kc, vc] + ([kl, vl] if has_lat else [])
    specs = [pl.BlockSpec((1, tq, 512), lambda bi, i: (bi, i, 0))] + [whole(a) for a in ins[1:]]
    return pl.pallas_call(
        functools.partial(_mla_kernel, has_lat=has_lat),
        grid=(b, n // tq),
        in_specs=specs,
        out_specs=pl.BlockSpec((1, tq, 256), lambda bi, i: (bi, i, 0)),
        out_shape=jax.ShapeDtypeStruct((b, n, 256), BF16),
        scratch_shapes=[pltpu.VMEM((nk, 512), BF16), pltpu.VMEM((MLA_HEADS, nk, 2 * LANES), BF16)],
        compiler_params=_cparams(("parallel", "arbitrary")),
        name="mla_attend",
    )(*ins)


def _win_kernel(*refs, has_lat, n_lat, layer):
    if has_lat:
        sink_ref, q_ref, kc_ref, vc_ref, kl_ref, vl_ref, band_ref, o_ref = refs
    else:
        sink_ref, q_ref, kc_ref, vc_ref, o_ref = refs
    tq = q_ref.shape[1]
    wk = tq + 2 * WINDOW
    lane = lax.broadcasted_iota(jnp.int32, (tq, LANES), 1)
    if has_lat:
        t0 = pl.program_id(1) * tq
        start = pl.multiple_of(jnp.clip(t0 - WINDOW, 0, n_lat - wk), LANES)
        band = band_ref[0]

    def aug(v, e):
        vlane = lax.broadcasted_iota(jnp.int32, v.shape, 1)
        own = (vlane < HEAD_DIM) if e == 0 else (vlane >= HEAD_DIM)
        return jnp.concatenate([jnp.where(own, v, jnp.zeros_like(v)), jnp.ones(v.shape, BF16)], axis=1)

    heads = [(pr, e) for pr in range(WG_KV_HEADS) for e in range(2)]
    psl = [slice(LANES * pr, LANES * (pr + 1)) for pr in range(WG_KV_HEADS)]
    sink = [sink_ref[layer, hq] * LOG2E for hq in range(WG_HEADS)]
    s_c, s_l, mx = [], [], []
    for pr, e in heads:
        qp = q_ref[0, :, psl[pr]]
        own = (lane < HEAD_DIM) if e == 0 else (lane >= HEAD_DIM)
        q = jnp.where(own, qp, jnp.zeros_like(qp))
        s_c.append(_dot_nt(q, kc_ref[0, :, psl[pr]]))
        if has_lat:
            s_l.append(_dot_nt(q, kl_ref[0, pl.ds(start, wk), psl[pr]]) + band)
    for i in range(WG_HEADS):
        m = jnp.maximum(jnp.max(s_c[i], axis=-1, keepdims=True), sink[i])
        if has_lat:
            m = jnp.maximum(m, jnp.max(s_l[i], axis=-1, keepdims=True))
        mx.append(m)
    o2 = []
    for i, (pr, e) in enumerate(heads):
        o = _dot(jnp.exp2(s_c[i] - mx[i]).astype(BF16), aug(vc_ref[0, :, psl[pr]], e))
        if has_lat:
            o = o + _dot(jnp.exp2(s_l[i] - mx[i]).astype(BF16), aug(vl_ref[0, pl.ds(start, wk), psl[pr]], e))
        o2.append(o)
    outs = [o2[i][:, :LANES] / (o2[i][:, LANES:] + jnp.exp2(sink[i] - mx[i])) for i in range(WG_HEADS)]
    o_ref[0] = jnp.concatenate([outs[0] + outs[1], outs[2] + outs[3]], axis=-1).astype(o_ref.dtype)


def _win_attend(sink, layer, q, kc, vc, kl=None, vl=None, tq=256):
    b, n, _ = q.shape
    tq = min(tq, n)
    has_lat = kl is not None
    whole = lambda a: pl.BlockSpec((1,) + a.shape[1:], lambda bi, i: (bi, 0, 0))
    ins = [q, kc, vc] + ([kl, vl] if has_lat else [])
    specs = ([pl.BlockSpec(memory_space=pltpu.SMEM),
              pl.BlockSpec((1, tq, 256), lambda bi, i: (bi, i, 0))] + [whole(a) for a in ins[1:]])
    if has_lat:
        nt = n // tq
        wk = tq + 2 * WINDOW
        assert nt >= 2 and wk <= n
        rel = np.arange(wk)[None, :] - np.arange(tq)[:, None]
        shift = (0, WINDOW, 2 * WINDOW)
        band = np.stack([np.where(np.abs(rel - sh) <= WINDOW, 0.0, NEG) for sh in shift]).astype(np.float32)
        ins.append(jnp.asarray(band))
        specs.append(pl.BlockSpec((1, tq, wk), lambda bi, i: (jnp.where(i == 0, 0, jnp.where(i == nt - 1, 2, 1)), 0, 0)))
    return pl.pallas_call(
        functools.partial(_win_kernel, has_lat=has_lat, n_lat=n, layer=layer),
        grid=(b, n // tq),
        in_specs=specs,
        out_specs=pl.BlockSpec((1, tq, 256), lambda bi, i: (bi, i, 0)),
        out_shape=jax.ShapeDtypeStruct((b, n, 256), BF16),
        compiler_params=_cparams(("parallel", "parallel")),
        name="win_attend",
    )(sink, *ins)


def _fourier_kernel(u_ref, cs_ref, dft_ref, o_ref, ab_ref, *, tr):
    n = u_ref.shape[1]
    half = n // 2
    rev = (lax.broadcasted_iota(jnp.int32, (half, half), 0)
           + lax.broadcasted_iota(jnp.int32, (half, half), 1) == half).astype(BF16)
    u_rev = _dot(rev, u_ref[0, half:n, :]).astype(BF16)
    ab_a = _dot(u_ref[0, 0:half, :], cs_ref[...])
    ab_r = _dot(u_rev, cs_ref[...])
    a_mid = _dot(u_ref[0, half:half + 16, :], cs_ref[:, 0:256])[0:1, :]
    first = lax.broadcasted_iota(jnp.int32, (half, 256), 0) == 0
    ab_ref[0:half, :] = (ab_a[:, :256] + ab_r[:, :256]).astype(BF16)
    ab_ref[half:n, :] = jnp.where(first, a_mid, ab_a[:, 256:] - ab_r[:, 256:]).astype(BF16)
    scale = 1.0 / math.sqrt(n * FN_GROUP_W)

    def body(r, carry):
        r0 = pl.multiple_of(r * tr, tr)
        y = _dot(dft_ref[pl.ds(r0, tr), :], ab_ref[...])
        o_ref[0, pl.ds(r0, tr), :] = (y * scale).astype(o_ref.dtype)
        return carry

    lax.fori_loop(0, n // tr, body, 0)


def _fourier(u, cs, dft):
    b, n, w = u.shape
    tr = min(512, n)
    return pl.pallas_call(
        functools.partial(_fourier_kernel, tr=tr),
        grid=(b,),
        in_specs=[pl.BlockSpec((1, n, w), lambda bi: (bi, 0, 0)),
                  _const_spec(cs.shape),
                  _const_spec(dft.shape)],
        out_specs=pl.BlockSpec((1, n, w), lambda bi: (bi, 0, 0)),
        out_shape=jax.ShapeDtypeStruct((b, n, w), BF16),
        scratch_shapes=[pltpu.VMEM((n, w), BF16)],
        compiler_params=_cparams(("parallel",)),
        name="fourier",
    )(u, cs, dft)


ML_ROWS = 2


def _split3(x):
    hi = x.astype(BF16).astype(F32)
    r1 = x - hi
    mid = r1.astype(BF16).astype(F32)
    lo = (r1 - mid).astype(BF16).astype(F32)
    return hi, mid, lo


def _mlstm_kernel(*refs, ctx_out):
    (qs_l, ks_l, vt_l, ot_l, gt_l, qs_c, ks_c, vt_c, ot_c, gt_c, bgt_ref) = refs[:11]
    if ctx_out:
        y_l, y_c = refs[11:13]
        scratch = refs[13:]
    else:
        y_l = refs[11]
        y_c = None
        scratch = refs[12:]
    nrows = qs_l.shape[0]
    ns = 2 * nrows
    hacc_l, hacc_c = scratch[0:ns], scratch[ns:2 * ns]
    cst = scratch[2 * ns:2 * ns + ns * ML_HEADS]
    mst = scratch[2 * ns + ns * ML_HEADS:]
    L = ML_CHUNK
    row = lax.broadcasted_iota(jnp.int32, (L, L), 0)
    col = lax.broadcasted_iota(jnp.int32, (L, L), 1)
    triu_b = (row <= col).astype(BF16)
    krow = lax.broadcasted_iota(jnp.int32, (32, L), 0)
    zeros8 = jnp.zeros((8, L), F32)

    for ref in list(cst) + list(mst):
        ref[...] = jnp.zeros_like(ref)

    chains = [(si, hh) for si in range(ns) for hh in range(ML_HEADS)]
    own_l = [(col < ML_HEAD_DIM) if hh % 2 == 0 else (col >= ML_HEAD_DIM) for hh in range(ML_HEADS)]
    own_r = [(row < ML_HEAD_DIM) if hh % 2 == 0 else (row >= ML_HEAD_DIM) for hh in range(ML_HEADS)]
    mask = [row <= col, row >= col]
    sel = [jnp.where((krow & 7) == hh, 1.0, 0.0).astype(BF16) for hh in range(ML_HEADS)]

    def chunk_pair(qs, ks, vt, gt, hacc, cs, need_h):
        r0 = [pl.multiple_of(c * L, L) for c in cs]
        cum, tot, b, m_st, wk, decay, lhs = [], [], [], [], [], [], []
        for d in range(ns):
            rr, dd = d // 2, d % 2
            g = gt[rr, cs[dd]] + bgt_ref[0]
            li = g[16 * dd:16 * dd + 8]
            gf = g[16 * dd + 8:16 * dd + 16]
            ls = jnp.minimum(gf, 0.0) - jnp.log(1.0 + jnp.exp(-jnp.abs(gf)))
            cu = sum(_dot(piece.astype(BF16), triu_b) for piece in _split3(ls))
            tt = cu[:, L - 1:L]
            if dd == 1:
                cu = tt - cu + ls
            bb = li - cu
            ms = mst[d][...]
            lw = tt + bb
            m_new = jnp.maximum(tt + ms, jnp.max(lw, axis=1, keepdims=True))
            wk.append(jnp.exp(lw - m_new))
            decay.append(jnp.exp(tt + ms - m_new))
            mst[d][...] = m_new
            cum.append(cu)
            tot.append(tt)
            b.append(bb)
            m_st.append(ms)
            if need_h:
                lhs.append(jnp.concatenate(list(_split3(bb)) + [zeros8], axis=0).astype(BF16))
        kh, v_aug, c_aug, qh, kq, cq, bm = [], [], [], [], [], [], []
        for d, hh in chains:
            rr, dd = d // 2, d % 2
            psl = slice(LANES * (hh // 2), LANES * (hh // 2 + 1))
            kh.append(ks[rr, pl.ds(r0[dd], L), psl])
            vb = vt[rr, cs[dd], psl, :]
            v_aug.append(jnp.where(own_r[hh], vb, jnp.ones_like(vb)))
            c_aug.append(cst[d * ML_HEADS + hh][...])
            if need_h:
                qb = qs[rr, pl.ds(r0[dd], L), psl]
                qh.append(jnp.where(own_l[hh], qb, jnp.zeros_like(qb)))
        if need_h:
            for i, (d, hh) in enumerate(chains):
                bm.append(jnp.where(mask[d % 2], _dot_tn(lhs[d], sel[hh]), NEG))
            for i in range(len(chains)):
                kq.append(_dot_nt(kh[i], qh[i]))
            for i in range(len(chains)):
                cq.append(_dot_nt(c_aug[i].astype(BF16), qh[i]))
            mu, st = [], []
            for i, (d, hh) in enumerate(chains):
                mu.append(jnp.maximum(m_st[d][hh:hh + 1, :], jnp.max(bm[i], axis=0, keepdims=True)))
            for i in range(len(chains)):
                st.append((kq[i] * jnp.exp(bm[i] - mu[i])).astype(BF16))
            r = []
            for i, (d, hh) in enumerate(chains):
                r.append(_dot(v_aug[i], st[i]) + cq[i] * jnp.exp(m_st[d][hh:hh + 1, :] - mu[i]))
            hs = []
            for i, (d, hh) in enumerate(chains):
                e_negm = jnp.exp(-(cum[d][hh:hh + 1, :] + mu[i]))
                den = pltpu.roll(r[i], ML_HEAD_DIM, axis=0)
                hs.append(r[i] / jnp.maximum(jnp.abs(den), e_negm))
            for d in range(ns):
                for p in range(2):
                    i = d * ML_HEADS + 2 * p
                    hacc[d][cs[d % 2], LANES * p:LANES * (p + 1), :] = jnp.where(row < ML_HEAD_DIM, hs[i], hs[i + 1])
        dc = []
        for i, (d, hh) in enumerate(chains):
            vw = (v_aug[i].astype(F32) * wk[d][hh:hh + 1, :]).astype(BF16)
            dc.append(_dot(vw, kh[i]))
        for i, (d, hh) in enumerate(chains):
            cst[i][...] = decay[d][hh:hh + 1, :] * c_aug[i] + dc[i]

    def scan(qs, ks, vt, gt, hacc, need_h):
        nc = gt.shape[1]

        def body(j, carry):
            chunk_pair(qs, ks, vt, gt, hacc, (j, nc - 1 - j), need_h)
            return carry

        lax.fori_loop(0, nc, body, 0, unroll=2)

    scan(qs_c, ks_c, vt_c, gt_c, hacc_c, ctx_out)
    scan(qs_l, ks_l, vt_l, gt_l, hacc_l, True)

    def finish(ot, hacc, y):
        def body(c, carry):
            for rr in range(nrows):
                gate = _sigmoid(ot[rr, c].astype(F32))
                y[rr, c] = (gate * (hacc[2 * rr][c] + hacc[2 * rr + 1][c])).astype(y.dtype)
            return carry

        lax.fori_loop(0, ot.shape[1], body, 0)

    finish(ot_l, hacc_l, y_l)
    if ctx_out:
        finish(ot_c, hacc_c, y_c)


def _mlstm(lat, ctx, pw, l, ctx_out):
    b, n, _ = lat[0].shape
    nc = ctx[0].shape[1]
    nr = ML_ROWS if b % ML_ROWS == 0 else 1
    ns = 2 * nr
    whole = lambda a: pl.BlockSpec((nr,) + a.shape[1:], lambda bi: (bi,) + (0,) * (a.ndim - 1))
    ins = list(lat) + list(ctx)
    tshape = lambda m: (b, m // LANES, 256, LANES)
    out_shape = [jax.ShapeDtypeStruct(tshape(n), BF16)]
    out_specs = [pl.BlockSpec((nr,) + tshape(n)[1:], lambda bi: (bi, 0, 0, 0))]
    if ctx_out:
        out_shape.append(jax.ShapeDtypeStruct(tshape(nc), BF16))
        out_specs.append(pl.BlockSpec((nr,) + tshape(nc)[1:], lambda bi: (bi, 0, 0, 0)))
    res = pl.pallas_call(
        functools.partial(_mlstm_kernel, ctx_out=ctx_out),
        grid=(b // nr,),
        in_specs=[whole(a) for a in ins] + [_layer_spec(pw["bgt"], l)],
        out_specs=out_specs,
        out_shape=out_shape,
        scratch_shapes=[pltpu.VMEM((n // LANES, 256, LANES), F32)] * ns
                       + [pltpu.VMEM((nc // LANES, 256, LANES), F32)] * ns
                       + [pltpu.VMEM((LANES, LANES), F32)] * (ns * ML_HEADS)
                       + [pltpu.VMEM((8, LANES), F32)] * ns,
        compiler_params=_cparams(("parallel",)),
        name="mlstm",
    )(*ins, pw["bgt"])
    return (res[0], res[1]) if ctx_out else (res[0], None)


MERGE_GROUPS = 2


def _merge_kernel(x_ref, mod_ref, ya_ref, yb_ref, yc_ref, yd_ref, gpre_ref, gpost_ref, gffn_ref,
                  wg_ref, bgate_ref, wb_ref, wo_ref, x1_ref, h2_ref):
    m = mod_ref[0, 0]
    tm = x_ref.shape[1]
    groups = min(MERGE_GROUPS, tm // LANES)
    th = tm // groups
    rows = [slice(a * th, (a + 1) * th) for a in range(groups)]
    xs = [x_ref[0, r, :] for r in rows]
    hbs = [(_rms(x, gpre_ref[0]) * (1.0 + m[1:2]) + m[0:1]).astype(BF16) for x in xs]
    cpg = th // LANES
    accs = [None] * groups
    for s, y_ref in enumerate((ya_ref, yb_ref, yc_ref, yd_ref)):
        for a, r in enumerate(rows):
            gate = _sigmoid(_dot(hbs[a], wg_ref[0, s]) + bgate_ref[0, s:s + 1, :])
            if s == 1:
                yb_t = jnp.concatenate([yb_ref[0, a * cpg + j] for j in range(cpg)], axis=1)
                branch = _dot_tn(yb_t, wb_ref[0, s])
            else:
                branch = _dot(y_ref[0, r, :], wb_ref[0, s])
            term = gate * branch
            accs[a] = term if accs[a] is None else accs[a] + term
    ys = [_dot(acc.astype(BF16), wo_ref[0]) for acc in accs]
    for a, r in enumerate(rows):
        x1 = xs[a] + m[2:3] * _rms(ys[a], gpost_ref[0])
        x1_ref[0, r, :] = x1
        h2_ref[0, r, :] = (_rms(x1, gffn_ref[0]) * (1.0 + m[4:5]) + m[3:4]).astype(BF16)


def _merge(x, mod, ctx_row, ys, pw, l, tm):
    b, n, d = x.shape
    tm = min(tm, n)
    row = lambda bi, i: (bi, i, 0)
    names = ("g_pre_mix", "g_post_mix", "g_pre_ffn", "wg", "b_gate", "wb", "wo")
    return pl.pallas_call(
        _merge_kernel,
        grid=(b, n // tm),
        in_specs=[pl.BlockSpec((1, tm, d), row), _mod_spec(l, ctx_row)]
                 + [pl.BlockSpec((1, tm, BRANCH_W), row),
                    pl.BlockSpec((1, tm // LANES, BRANCH_W, LANES), lambda bi, i: (bi, i, 0, 0)),
                    pl.BlockSpec((1, tm, BRANCH_W), row), pl.BlockSpec((1, tm, BRANCH_W), row)]
                 + [_layer_spec(pw[k], l) for k in names],
        out_specs=[pl.BlockSpec((1, tm, d), row), pl.BlockSpec((1, tm, d), row)],
        out_shape=[jax.ShapeDtypeStruct((b, n, d), F32), jax.ShapeDtypeStruct((b, n, d), BF16)],
        compiler_params=_cparams(("parallel", "parallel")),
        name="merge",
    )(x, mod, *ys, *[pw[k] for k in names])


FFN_HALO = 16
FFN_COL_CHUNKS = ((0, 1536), (1536, 2816))


def _ffn_kernel(h_ref, hp_ref, hn_ref, x1_ref, mod_ref, gpost_ref, wup_ref, wconv_ref, bconv_ref, wdown_ref,
                o_ref):
    i = pl.program_id(1)
    last = pl.num_programs(1) - 1
    tm = h_ref.shape[1]
    hm = h_ref[0]
    hp = jnp.where(i > 0, hp_ref[0], jnp.zeros_like(hp_ref[0]))
    hn = jnp.where(i < last, hn_ref[0], jnp.zeros_like(hn_ref[0]))
    hext = jnp.concatenate([hp, hm, hn], axis=0)
    ne = tm + 2 * FFN_HALO
    acc = None
    for lo, hi in FFN_COL_CHUNKS:
        a = _dot(hext, wup_ref[0, :, lo:hi])
        ap = pltpu.roll(a, 1, axis=0)[FFN_HALO:FFN_HALO + tm]
        an = pltpu.roll(a, ne - 1, axis=0)[FFN_HALO:FFN_HALO + tm]
        ac = a[FFN_HALO:FFN_HALO + tm]
        a = (ap * wconv_ref[0, 0:1, lo:hi] + ac * wconv_ref[0, 1:2, lo:hi] + an * wconv_ref[0, 2:3, lo:hi]
             + bconv_ref[0, :, lo:hi])
        v = _dot(hm, wup_ref[0, :, D_FF + lo:D_FF + hi])
        act = (a * _sigmoid(a) * v).astype(BF16)
        term = _dot(act, wdown_ref[0, lo:hi, :])
        acc = term if acc is None else acc + term
    m = mod_ref[0, 0]
    o_ref[0] = x1_ref[0] + m[5:6] * _rms(acc, gpost_ref[0])


def _ffn(h2, x1, mod, ctx_row, pw, l, tm):
    b, n, d = x1.shape
    tm = min(tm, n)
    row = lambda bi, i: (bi, i, 0)
    r = tm // FFN_HALO
    nblk = n // FFN_HALO
    names = ("g_post_ffn", "wup", "w_ffn_conv", "b_ffn_conv", "wdown")
    return pl.pallas_call(
        _ffn_kernel,
        grid=(b, n // tm),
        in_specs=[pl.BlockSpec((1, tm, d), row),
                  pl.BlockSpec((1, FFN_HALO, d), lambda bi, i: (bi, jnp.maximum(i * r - 1, 0), 0)),
                  pl.BlockSpec((1, FFN_HALO, d), lambda bi, i: (bi, jnp.minimum((i + 1) * r, nblk - 1), 0)),
                  pl.BlockSpec((1, tm, d), row),
                  _mod_spec(l, ctx_row)]
                 + [_layer_spec(pw[k], l) for k in names],
        out_specs=pl.BlockSpec((1, tm, d), row),
        out_shape=jax.ShapeDtypeStruct((b, n, d), F32),
        compiler_params=_cparams(("parallel", "parallel")),
        name="conv_ffn",
    )(h2, h2, h2, x1, mod, *[pw[k] for k in names])


def _rope_cos_sin(n, d, identity):
    if identity:
        return jnp.ones((n, d), F32), jnp.zeros((n, d), F32)
    half, nf = d // 2, d // 4
    t = jnp.arange(n, dtype=jnp.int32)
    row = (t // GRID_W).astype(F32)[:, None]
    colp = (t % GRID_W).astype(F32)[:, None]
    inv = ROPE_BASE ** (-jnp.arange(nf, dtype=F32) / nf)
    ang = jnp.concatenate([row * inv, row * inv, colp * inv, colp * inv], axis=-1)
    sign = jnp.asarray(np.where(np.arange(d) % half < nf, -1.0, 1.0), F32)
    return jnp.cos(ang), jnp.sin(ang) * sign


def _rope_table(n, identity):
    ca, sa = _rope_cos_sin(n, MLA_ROPE, identity)
    cw, sw = _rope_cos_sin(n, HEAD_DIM, identity)
    scale_a = (MLA_NOPE + MLA_ROPE) ** -0.5 * LOG2E
    scale_w = HEAD_DIM ** -0.5 * LOG2E
    z32 = jnp.zeros((n, LANES - MLA_NOPE - MLA_ROPE), F32)
    one64 = jnp.ones((n, MLA_NOPE), F32)
    zero64 = jnp.zeros((n, MLA_NOPE), F32)
    qa_cos = jnp.concatenate([one64, ca, z32], -1) * scale_a
    qa_sin = jnp.concatenate([zero64, sa, z32], -1) * scale_a
    ka_cos = jnp.concatenate([zero64, ca, z32], -1)
    ka_sin = jnp.concatenate([zero64, sa, z32], -1)
    cw2 = jnp.concatenate([cw, cw], -1)
    sw2 = jnp.concatenate([sw, sw], -1)
    return jnp.concatenate([qa_cos, qa_sin, ka_cos, ka_sin, cw2 * scale_w, sw2 * scale_w, cw2, sw2], axis=-1)


def _dft_tables(n):
    n1 = n // FN_GROUP_W
    k = jnp.arange(n, dtype=jnp.int32)[:, None]
    ang1 = ((k * jnp.arange(n1, dtype=jnp.int32)[None, :]) % n1).astype(F32) * (2.0 * math.pi / n1)
    ang0 = ((k * jnp.arange(FN_GROUP_W, dtype=jnp.int32)[None, :]) % n).astype(F32) * (2.0 * math.pi / n)
    c1, s1 = jnp.cos(ang1)[:, :, None], jnp.sin(ang1)[:, :, None]
    c0, s0 = jnp.cos(ang0)[:, None, :], jnp.sin(ang0)[:, None, :]
    cos_kt = (c1 * c0 - s1 * s0).reshape(n, n)
    sin_kt = (s1 * c0 + c1 * s0).reshape(n, n)
    half = n // 2
    col = jnp.arange(half, dtype=jnp.int32)[None, :]
    dft = jnp.concatenate([cos_kt[:, :half], jnp.where(col == 0, cos_kt[:, half:half + 1], -sin_kt[:, :half])],
                          axis=-1).astype(BF16)
    j = np.arange(FN_GROUP_W)
    a64 = (np.outer(j, j) % FN_GROUP_W) * (2.0 * np.pi / FN_GROUP_W)
    eye = np.eye(FN_GROUPS)
    cs = np.concatenate([np.kron(eye, np.cos(a64)), np.kron(eye, np.sin(a64))], axis=-1)
    return dft, jnp.asarray(cs, F32).astype(BF16)


def _w1_index():
    off = np.concatenate([[0], np.cumsum(IN_SPLITS)])
    z = int(off[-1])
    seg = lambda i: np.arange(off[i], off[i + 1])
    zeros = lambda w: np.full((w,), z)
    head = lambda i, h: off[i] + HEAD_DIM * h + np.arange(HEAD_DIM)
    dup = lambda i: np.concatenate([head(i, h) for h in (0, 0, 1, 1)])
    idx = np.concatenate([
        seg(0), seg(1), zeros(MLA_NOPE), seg(2), zeros(LANES - MLA_NOPE - MLA_ROPE),
        seg(8), dup(9), dup(10), seg(11),
        seg(3), seg(4)])
    assert idx.shape == (C_END,)
    return idx.astype(np.int32)


def _w1t_index():
    off = np.concatenate([[0], np.cumsum(IN_SPLITS)])
    z = int(off[-1])
    gates = np.concatenate([np.concatenate([off[7] + ML_HEADS * k + np.arange(ML_HEADS), np.full((8 - ML_HEADS,), z)])
                            for k in range(4)])
    idx = np.concatenate([np.arange(off[5], off[7]), gates])
    assert idx.shape == (R_END,)
    return idx.astype(np.int32)


def _gate_bias_index():
    return np.concatenate([np.concatenate([ML_HEADS * k + np.arange(ML_HEADS), np.full((8 - ML_HEADS,), 4 * ML_HEADS)])
                           for k in range(4)]).astype(np.int32)


def _wq_index():
    dq = MLA_NOPE + MLA_ROPE
    z = MLA_HEADS * dq
    pad = np.full((LANES - dq,), z)
    return np.concatenate([np.concatenate([dq * h + np.arange(dq), pad]) for h in range(MLA_HEADS)]).astype(np.int32)


def _wkv_index():
    dkv = MLA_NOPE + MLA_V
    z = MLA_HEADS * dkv
    kn = [np.concatenate([dkv * h + np.arange(MLA_NOPE), np.full((LANES - MLA_NOPE,), z)]) for h in range(MLA_HEADS)]
    vv = [dkv * h + MLA_NOPE + np.arange(MLA_V) for h in range(MLA_HEADS)]
    return np.concatenate(kn + vv).astype(np.int32)


def _take_cols(w, idx):
    z = w.shape[-1]
    idx = np.asarray(idx)
    cuts = [0] + [i for i in range(1, len(idx)) if not ((idx[i] == idx[i - 1] + 1 and idx[i] != z)
                                                        or (idx[i] == z and idx[i - 1] == z))] + [len(idx)]
    parts = []
    for lo, hi in zip(cuts[:-1], cuts[1:]):
        if idx[lo] == z:
            parts.append(jnp.zeros(w.shape[:-1] + (hi - lo,), w.dtype))
        else:
            parts.append(w[..., int(idx[lo]):int(idx[hi - 1]) + 1])
    return jnp.concatenate(parts, axis=-1)


def _prep_weights(p):
    depth = p["w_in"].shape[0]
    vec = lambda a: a.reshape(depth, 1, a.shape[-1])
    bgt = jnp.broadcast_to(_take_cols(p["b_ml_gates"], _gate_bias_index())[:, :, None], (depth, GT_ROWS, LANES))
    return {
        "w1": _take_cols(p["w_in"], _w1_index()).astype(BF16),
        "w1t": jnp.swapaxes(_take_cols(p["w_in"], _w1t_index()), 1, 2).astype(BF16),
        "wq": _take_cols(p["w_uq"], _wq_index()).astype(BF16),
        "wkv": _take_cols(p["w_ukv"], _wkv_index()).astype(BF16),
        "g_qa": vec(p["g_qa"]), "g_kva": vec(p["g_kva"]),
        "g_pre_mix": vec(p["g_pre_mix"]), "g_post_mix": vec(p["g_post_mix"]),
        "g_pre_ffn": vec(p["g_pre_ffn"]), "g_post_ffn": vec(p["g_post_ffn"]),
        "w_ml_conv": p["w_ml_conv"], "bgt": bgt, "wg_sink": p["wg_sink"],
        "wg": p["w_gate"].astype(BF16), "b_gate": p["b_gate"],
        "wb": p["w_branch"].astype(BF16), "wo": p["w_out"].astype(BF16),
        "wup": p["w_up"].astype(BF16), "w_ffn_conv": p["w_ffn_conv"],
        "b_ffn_conv": vec(p["b_ffn_conv"]), "wdown": p["w_down"].astype(BF16),
    }


TM = 512


def kernel(x, c, ctx, c_ctx, w_mod, b_mod, g_pre_mix, g_post_mix, g_pre_ffn, g_post_ffn, w_in, g_qa, w_uq, g_kva,
           w_ukv, w_ml_conv, b_ml_gates, wg_sink, w_gate, b_gate, w_branch, w_out, w_up, w_ffn_conv, b_ffn_conv,
           w_down):
    p = dict(g_pre_mix=g_pre_mix, g_post_mix=g_post_mix, g_pre_ffn=g_pre_ffn, g_post_ffn=g_post_ffn, w_in=w_in,
             g_qa=g_qa, w_uq=w_uq, g_kva=g_kva, w_ukv=w_ukv, w_ml_conv=w_ml_conv, b_ml_gates=b_ml_gates,
             wg_sink=wg_sink, w_gate=w_gate, b_gate=b_gate, w_branch=w_branch, w_out=w_out, w_up=w_up,
             w_ffn_conv=w_ffn_conv, b_ffn_conv=b_ffn_conv, w_down=w_down)
    bsz, n, d = x.shape
    n_ctx = ctx.shape[1]
    depth = w_mod.shape[0]
    rows = -(-(bsz + 1) // 8) * 8
    c_all = jnp.concatenate([c, c_ctx[None, :], jnp.zeros((rows - bsz - 1, d), F32)], axis=0)
    mod = _modulation(c_all, w_mod, b_mod).reshape(depth, rows, 6, d)
    pw = _prep_weights(p)
    tab_l = _rope_table(n, identity=False)
    tab_c = _rope_table(n_ctx, identity=True)
    dft_l, cs64 = _dft_tables(n)
    dft_c, _ = _dft_tables(n_ctx)
    xc = ctx
    for l in range(depth):
        ctx_out = l < depth - 1
        qa, ka, va, uq, uk, vt, ot, gt, qw, kw, vw, uf = _inproj(x, mod, None, pw, l, tab_l, TM)
        qa_c, ka_c, va_c, uq_c, uk_c, vt_c, ot_c, gt_c, qw_c, kw_c, vw_c, uf_c = _inproj(
            xc, mod, bsz, pw, l, tab_c, TM)
        ya = _mla_attend(qa, ka_c, va_c, ka, va)
        yb, yb_c = _mlstm((uq, uk, vt, ot, gt), (uq_c, uk_c, vt_c, ot_c, gt_c), pw, l, ctx_out)
        yc = _win_attend(pw["wg_sink"], l, qw, kw_c, vw_c, kw, vw)
        yd = _fourier(uf, cs64, dft_l)
        x1, h2 = _merge(x, mod, None, (ya, yb, yc, yd), pw, l, TM)
        x = _ffn(h2, x1, mod, None, pw, l, TM)
        if ctx_out:
            ya_c = _mla_attend(qa_c, ka_c, va_c)
            yc_c = _win_attend(pw["wg_sink"], l, qw_c, kw_c, vw_c)
            yd_c = _fourier(uf_c, cs64, dft_c)
            xc1, hc2 = _merge(xc, mod, bsz, (ya_c, yb_c, yc_c, yd_c), pw, l, TM)
            xc = _ffn(hc2, xc1, mod, bsz, pw, l, TM)
    return x
```

```python
import functools
import math

import numpy as np
import jax
import jax.numpy as jnp
from jax import lax
from jax.experimental import pallas as pl
from jax.experimental.pallas import tpu as pltpu

F32 = jnp.float32
BF16 = jnp.bfloat16

D_MODEL = 1024
GRID_W = 64
N_BRANCH = 4
BRANCH_W = 256
MLA_HEADS = 4
MLA_NOPE = 64
MLA_ROPE = 32
MLA_V = 64
MLA_Q_RANK = 256
MLA_KV_RANK = 256
ML_HEADS = 4
ML_HEAD_DIM = 64
ML_CHUNK = 128
WG_HEADS = 4
WG_KV_HEADS = 2
HEAD_DIM = 64
WINDOW = 128
FN_GROUPS = 4
FN_GROUP_W = 64
D_FF = 2816
ROPE_BASE = 10000.0
EPS = 1e-6
IN_SPLITS = (MLA_Q_RANK, MLA_KV_RANK, MLA_ROPE, 256, 256, 256, 256, 16, 256, 128, 128, 256)

LANES = 128
NEG = -1e30
LOG2E = 1.4426950408889634
VMEM_LIMIT = 56 * 1024 * 1024

C_CQ, C_CKV, C_KR, C_WQ, C_WK, C_WV, C_UF, C_UQ, C_UK, C_END = (
    0, 256, 512, 640, 896, 1152, 1408, 1664, 1920, 2176)
IN_HALO = 16
R_UV, R_UO, R_GT, R_END = 0, 256, 512, 544
GT_ROWS = 32
T_QA_COS, T_QA_SIN, T_KA_COS, T_KA_SIN, T_QW_COS, T_QW_SIN, T_KW_COS, T_KW_SIN = range(8)


def _cparams(sem):
    return pltpu.CompilerParams(dimension_semantics=sem, vmem_limit_bytes=VMEM_LIMIT)


def _layer_spec(arr, l):
    nd = arr.ndim - 1
    return pl.BlockSpec((1,) + arr.shape[1:], lambda *_: (l,) + (0,) * nd, pipeline_mode=pl.Buffered(1))


def _const_spec(shape):
    nd = len(shape)
    return pl.BlockSpec(shape, lambda *_: (0,) * nd, pipeline_mode=pl.Buffered(1))


def _mod_spec(l, ctx_row):
    if ctx_row is None:
        return pl.BlockSpec((1, 1, 6, D_MODEL), lambda bi, *_: (l, bi, 0, 0))
    return pl.BlockSpec((1, 1, 6, D_MODEL), lambda *_: (l, ctx_row, 0, 0))


def _rms(x, g):
    ms = jnp.mean(x * x, axis=-1, keepdims=True)
    return x * lax.rsqrt(ms + EPS) * g


def _dot(a, b):
    return jnp.dot(a, b, preferred_element_type=F32)


def _dot_nt(a, b):
    return lax.dot_general(a, b, (((1,), (1,)), ((), ())), preferred_element_type=F32)


def _dot_tn(a, b):
    return lax.dot_general(a, b, (((0,), (0,)), ((), ())), preferred_element_type=F32)


def _sigmoid(x):
    return 1.0 / (1.0 + jnp.exp(-x))


def _mod_kernel(c_ref, w_ref, b_ref, o_ref):
    c = c_ref[...]
    s = (c * _sigmoid(c)).astype(BF16)
    o_ref[0] = _dot(s, w_ref[0].astype(BF16)) + b_ref[0]


def _modulation(c_all, w_mod, b_mod):
    depth, d, n = w_mod.shape
    rows = c_all.shape[0]
    tn = 1536
    return pl.pallas_call(
        _mod_kernel,
        grid=(depth, n // tn),
        in_specs=[pl.BlockSpec((rows, d), lambda l, j: (0, 0)),
                  pl.BlockSpec((1, d, tn), lambda l, j: (l, 0, j)),
                  pl.BlockSpec((1, 1, tn), lambda l, j: (l, 0, j))],
        out_specs=pl.BlockSpec((1, rows, tn), lambda l, j: (l, 0, j)),
        out_shape=jax.ShapeDtypeStruct((depth, rows, n), F32),
        compiler_params=_cparams(("parallel", "parallel")),
        name="modulation",
    )(c_all, w_mod, b_mod.reshape(depth, 1, n))


def _inproj_kernel(x_ref, xp_ref, xn_ref, mod_ref, gpre_ref, w1_ref, w1t_ref, gqa_ref, wq_ref, gkva_ref, wkv_ref,
                   wconv_ref, tab_ref,
                   qa_ref, ka_ref, va_ref, qs_ref, ks_ref, vt_ref, ot_ref, gt_ref,
                   qw_ref, kw_ref, vw_ref, uf_ref):
    i = pl.program_id(1)
    last = pl.num_programs(1) - 1
    tm = x_ref.shape[1]
    m = mod_ref[0, 0]

    def modulated(xv):
        return (_rms(xv, gpre_ref[0]) * (1.0 + m[1:2]) + m[0:1]).astype(BF16)

    def tab(j):
        return tab_ref[:, LANES * j:LANES * (j + 1)]

    lane = lax.broadcasted_iota(jnp.int32, (tm, LANES), 1)

    def rope(xv, cos_slot, sin_slot, nf):
        first = (lane & (2 * nf - 1)) < nf
        blocks = []
        for j in range(xv.shape[1] // LANES):
            xb = xv[:, LANES * j:LANES * (j + 1)]
            partner = jnp.where(first, pltpu.roll(xb, LANES - nf, axis=1), pltpu.roll(xb, nf, axis=1))
            blocks.append(xb * tab(cos_slot) + partner * tab(sin_slot))
        return blocks[0] if len(blocks) == 1 else jnp.concatenate(blocks, axis=-1)

    hb = modulated(x_ref[0])
    u = _dot(hb, w1_ref[0, :, 0:C_UQ])

    def proj(lo, hi):
        return u[:, lo:hi]

    cq = _rms(proj(C_CQ, C_CKV), gqa_ref[0]).astype(BF16)
    ckv = _rms(proj(C_CKV, C_KR), gkva_ref[0]).astype(BF16)

    hp = modulated(xp_ref[0])
    hn = modulated(xn_ref[0])
    hext = jnp.concatenate([jnp.where(i > 0, hp, jnp.zeros_like(hp)), hb,
                            jnp.where(i < last, hn, jnp.zeros_like(hn))], axis=0)
    uqk = _dot(hext, w1_ref[0, :, C_UQ:C_END])
    vt = _dot_nt(w1t_ref[0, R_UV:R_UO, :], hb)
    ot = _dot_nt(w1t_ref[0, R_UO:R_GT, :], hb)
    gt = _dot_nt(w1t_ref[0, R_GT:R_END, :], hb)
    q2 = _dot(cq, wq_ref[0])
    kv = _dot(ckv, wkv_ref[0])

    qa_ref[0] = rope(q2, T_QA_COS, T_QA_SIN, MLA_ROPE // 4).astype(BF16)
    kr = rope(proj(C_KR, C_WQ), T_KA_COS, T_KA_SIN, MLA_ROPE // 4)
    ka_ref[0] = (kv[:, :512] + jnp.concatenate([kr] * MLA_HEADS, axis=-1)).astype(BF16)
    va_ref[0] = kv[:, 512:].astype(BF16)
    qw_ref[0] = rope(proj(C_WQ, C_WK), T_QW_COS, T_QW_SIN, HEAD_DIM // 4).astype(BF16)
    kw_ref[0] = rope(proj(C_WK, C_WV), T_KW_COS, T_KW_SIN, HEAD_DIM // 4).astype(BF16)
    vw_ref[0] = proj(C_WV, C_UF).astype(BF16)
    uf_ref[0] = proj(C_UF, C_UQ).astype(BF16)
    ne = tm + 2 * IN_HALO
    wc = wconv_ref[0]
    a = (pltpu.roll(uqk, 1, axis=0)[IN_HALO:IN_HALO + tm] * wc[0:1, :] + uqk[IN_HALO:IN_HALO + tm] * wc[1:2, :]
         + pltpu.roll(uqk, ne - 1, axis=0)[IN_HALO:IN_HALO + tm] * wc[2:3, :])
    a = a * _sigmoid(a)
    qs_ref[0] = a[:, :256].astype(BF16)
    ks_ref[0] = (a[:, 256:] * ML_HEAD_DIM ** -0.5).astype(BF16)
    for r, out_ref in ((vt, vt_ref), (ot, ot_ref), (gt, gt_ref)):
        for j in range(out_ref.shape[1]):
            out_ref[0, j] = r[:, LANES * j:LANES * (j + 1)].astype(out_ref.dtype)


def _inproj(x, mod, ctx_row, pw, l, tab, tm):
    b, n, d = x.shape
    tm = min(tm, n)
    row = lambda bi, i: (bi, i, 0)
    cpt = tm // LANES

    def out(w, dt):
        return jax.ShapeDtypeStruct((b, n, w), dt), pl.BlockSpec((1, tm, w), row)

    def out_t(w, dt):
        return (jax.ShapeDtypeStruct((b, n // LANES, w, LANES), dt),
                pl.BlockSpec((1, cpt, w, LANES), lambda bi, i: (bi, i, 0, 0)))

    outs = [out(512, BF16), out(512, BF16), out(256, BF16),
            out(256, BF16), out(256, BF16), out_t(256, BF16), out_t(256, BF16), out_t(GT_ROWS, F32),
            out(256, BF16), out(256, BF16), out(256, BF16), out(256, BF16)]
    names = ("g_pre_mix", "w1", "w1t", "g_qa", "wq", "g_kva", "wkv", "w_ml_conv")
    r = tm // IN_HALO
    nblk = n // IN_HALO
    return pl.pallas_call(
        _inproj_kernel,
        grid=(b, n // tm),
        in_specs=[pl.BlockSpec((1, tm, d), row),
                  pl.BlockSpec((1, IN_HALO, d), lambda bi, i: (bi, jnp.maximum(i * r - 1, 0), 0)),
                  pl.BlockSpec((1, IN_HALO, d), lambda bi, i: (bi, jnp.minimum((i + 1) * r, nblk - 1), 0)),
                  _mod_spec(l, ctx_row)]
                 + [_layer_spec(pw[k], l) for k in names]
                 + [pl.BlockSpec((tm, 8 * LANES), lambda bi, i: (i, 0))],
        out_specs=[o[1] for o in outs],
        out_shape=[o[0] for o in outs],
        compiler_params=_cparams(("parallel", "parallel")),
        name="inproj",
    )(x, x, x, mod, *[pw[k] for k in names], tab)


def _mla_kernel(*refs, has_lat):
    if has_lat:
        q_ref, kc_ref, vc_ref, kl_ref, vl_ref, o_ref, kcat, vaug = refs
    else:
        q_ref, kc_ref, vc_ref, o_ref, kcat, vaug = refs
    nc = kc_ref.shape[1]
    nk = kcat.shape[0]

    @pl.when(pl.program_id(1) == 0)
    def _():
        kcat[0:nc, :] = kc_ref[0]
        if has_lat:
            kcat[nc:nk, :] = kl_ref[0]
        for hh in range(MLA_HEADS):
            vsl = slice(LANES * (hh // 2), LANES * (hh // 2 + 1))

            def keep(v):
                lane = lax.broadcasted_iota(jnp.int32, v.shape, 1)
                own = (lane < MLA_V) if hh % 2 == 0 else (lane >= MLA_V)
                return jnp.where(own, v, jnp.zeros_like(v))

            vaug[hh, 0:nc, 0:LANES] = keep(vc_ref[0, :, vsl])
            if has_lat:
                vaug[hh, nc:nk, 0:LANES] = keep(vl_ref[0, :, vsl])
            vaug[hh, :, LANES:2 * LANES] = jnp.ones((nk, LANES), BF16)

    def logits(hh):
        hsl = slice(LANES * hh, LANES * (hh + 1))
        return _dot_nt(q_ref[0, :, hsl], kcat[:, hsl])

    def attend(hh, s):
        mx = jnp.max(s, axis=-1, keepdims=True)
        o2 = _dot(jnp.exp2(s - mx).astype(BF16), vaug[hh])
        return o2[:, :LANES] / o2[:, LANES:]

    outs = []
    s_next = logits(0)
    for hh in range(MLA_HEADS):
        s_cur = s_next
        if hh + 1 < MLA_HEADS:
            s_next = logits(hh + 1)
        outs.append(attend(hh, s_cur))
    o_ref[0] = jnp.concatenate([outs[0] + outs[1], outs[2] + outs[3]], axis=-1).astype(o_ref.dtype)


def _mla_attend(q, kc, vc, kl=None, vl=None, tq=512):
    b, n, _ = q.shape
    tq = min(tq, n)
    has_lat = kl is not None
    nk = kc.shape[1] + (kl.shape[1] if has_lat else 0)
    whole = lambda a: pl.BlockSpec((1,) + a.shape[1:], lambda bi, i: (bi, 0, 0))
    ins = [q, kc, vc] + ([kl, vl] if has_lat else [])
    specs = [pl.BlockSpec((1, tq, 512), lambda bi, i: (bi, i, 0))] + [whole(a) for a in ins[1:]]
    return pl.pallas_call(
        functools.partial(_mla_kernel, has_lat=has_lat),
        grid=(b, n // tq),
        in_specs=specs,
        out_specs=pl.BlockSpec((1, tq, 256), lambda bi, i: (bi, i, 0)),
        out_shape=jax.ShapeDtypeStruct((b, n, 256), BF16),
        scratch_shapes=[pltpu.VMEM((nk, 512), BF16), pltpu.VMEM((MLA_HEADS, nk, 2 * LANES), BF16)],
        compiler_params=_cparams(("parallel", "arbitrary")),
        name="mla_attend",
    )(*ins)


def _win_kernel(*refs, has_lat, n_lat, layer):
    if has_lat:
        sink_ref, q_ref, kc_ref, vc_ref, kl_ref, vl_ref, band_ref, o_ref = refs
    else:
        sink_ref, q_ref, kc_ref, vc_ref, o_ref = refs
    tq = q_ref.shape[1]
    wk = tq + 2 * WINDOW
    lane = lax.broadcasted_iota(jnp.int32, (tq, LANES), 1)
    if has_lat:
        t0 = pl.program_id(1) * tq
        start = pl.multiple_of(jnp.clip(t0 - WINDOW, 0, n_lat - wk), LANES)
        band = band_ref[0]

    def aug(v, e):
        vlane = lax.broadcasted_iota(jnp.int32, v.shape, 1)
        own = (vlane < HEAD_DIM) if e == 0 else (vlane >= HEAD_DIM)
        return jnp.concatenate([jnp.where(own, v, jnp.zeros_like(v)), jnp.ones(v.shape, BF16)], axis=1)

    heads = [(pr, e) for pr in range(WG_KV_HEADS) for e in range(2)]
    psl = [slice(LANES * pr, LANES * (pr + 1)) for pr in range(WG_KV_HEADS)]
    sink = [sink_ref[layer, hq] * LOG2E for hq in range(WG_HEADS)]
    s_c, s_l, mx = [], [], []
    for pr, e in heads:
        qp = q_ref[0, :, psl[pr]]
        own = (lane < HEAD_DIM) if e == 0 else (lane >= HEAD_DIM)
        q = jnp.where(own, qp, jnp.zeros_like(qp))
        s_c.append(_dot_nt(q, kc_ref[0, :, psl[pr]]))
        if has_lat:
            s_l.append(_dot_nt(q, kl_ref[0, pl.ds(start, wk), psl[pr]]) + band)
    for i in range(WG_HEADS):
        m = jnp.maximum(jnp.max(s_c[i], axis=-1, keepdims=True), sink[i])
        if has_lat:
            m = jnp.maximum(m, jnp.max(s_l[i], axis=-1, keepdims=True))
        mx.append(m)
    o2 = []
    for i, (pr, e) in enumerate(heads):
        o = _dot(jnp.exp2(s_c[i] - mx[i]).astype(BF16), aug(vc_ref[0, :, psl[pr]], e))
        if has_lat:
            o = o + _dot(jnp.exp2(s_l[i] - mx[i]).astype(BF16), aug(vl_ref[0, pl.ds(start, wk), psl[pr]], e))
        o2.append(o)
    outs = [o2[i][:, :LANES] / (o2[i][:, LANES:] + jnp.exp2(sink[i] - mx[i])) for i in range(WG_HEADS)]
    o_ref[0] = jnp.concatenate([outs[0] + outs[1], outs[2] + outs[3]], axis=-1).astype(o_ref.dtype)


def _win_attend(sink, layer, q, kc, vc, kl=None, vl=None, tq=256):
    b, n, _ = q.shape
    tq = min(tq, n)
    has_lat = kl is not None
    whole = lambda a: pl.BlockSpec((1,) + a.shape[1:], lambda bi, i: (bi, 0, 0))
    ins = [q, kc, vc] + ([kl, vl] if has_lat else [])
    specs = ([pl.BlockSpec(memory_space=pltpu.SMEM),
              pl.BlockSpec((1, tq, 256), lambda bi, i: (bi, i, 0))] + [whole(a) for a in ins[1:]])
    if has_lat:
        nt = n // tq
        wk = tq + 2 * WINDOW
        assert nt >= 2 and wk <= n
        rel = np.arange(wk)[None, :] - np.arange(tq)[:, None]
        shift = (0, WINDOW, 2 * WINDOW)
        band = np.stack([np.where(np.abs(rel - sh) <= WINDOW, 0.0, NEG) for sh in shift]).astype(np.float32)
        ins.append(jnp.asarray(band))
        specs.append(pl.BlockSpec((1, tq, wk), lambda bi, i: (jnp.where(i == 0, 0, jnp.where(i == nt - 1, 2, 1)), 0, 0)))
    return pl.pallas_call(
        functools.partial(_win_kernel, has_lat=has_lat, n_lat=n, layer=layer),
        grid=(b, n // tq),
        in_specs=specs,
        out_specs=pl.BlockSpec((1, tq, 256), lambda bi, i: (bi, i, 0)),
        out_shape=jax.ShapeDtypeStruct((b, n, 256), BF16),
        compiler_params=_cparams(("parallel", "parallel")),
        name="win_attend",
    )(sink, *ins)


def _fourier_kernel(u_ref, cs_ref, dft_ref, o_ref, ab_ref, *, tr):
    n = u_ref.shape[1]
    half = n // 2
    rev = (lax.broadcasted_iota(jnp.int32, (half, half), 0)
           + lax.broadcasted_iota(jnp.int32, (half, half), 1) == half).astype(BF16)
    u_rev = _dot(rev, u_ref[0, half:n, :]).astype(BF16)
    ab_a = _dot(u_ref[0, 0:half, :], cs_ref[...])
    ab_r = _dot(u_rev, cs_ref[...])
    a_mid = _dot(u_ref[0, half:half + 16, :], cs_ref[:, 0:256])[0:1, :]
    first = lax.broadcasted_iota(jnp.int32, (half, 256), 0) == 0
    ab_ref[0:half, :] = (ab_a[:, :256] + ab_r[:, :256]).astype(BF16)
    ab_ref[half:n, :] = jnp.where(first, a_mid, ab_a[:, 256:] - ab_r[:, 256:]).astype(BF16)
    scale = 1.0 / math.sqrt(n * FN_GROUP_W)

    def body(r, carry):
        r0 = pl.multiple_of(r * tr, tr)
        y = _dot(dft_ref[pl.ds(r0, tr), :], ab_ref[...])
        o_ref[0, pl.ds(r0, tr), :] = (y * scale).astype(o_ref.dtype)
        return carry

    lax.fori_loop(0, n // tr, body, 0)


def _fourier(u, cs, dft):
    b, n, w = u.shape
    tr = min(512, n)
    return pl.pallas_call(
        functools.partial(_fourier_kernel, tr=tr),
        grid=(b,),
        in_specs=[pl.BlockSpec((1, n, w), lambda bi: (bi, 0, 0)),
                  _const_spec(cs.shape),
                  _const_spec(dft.shape)],
        out_specs=pl.BlockSpec((1, n, w), lambda bi: (bi, 0, 0)),
        out_shape=jax.ShapeDtypeStruct((b, n, w), BF16),
        scratch_shapes=[pltpu.VMEM((n, w), BF16)],
        compiler_params=_cparams(("parallel",)),
        name="fourier",
    )(u, cs, dft)


ML_ROWS = 2


def _split3(x):
    hi = x.astype(BF16).astype(F32)
    r1 = x - hi
    mid = r1.astype(BF16).astype(F32)
    lo = (r1 - mid).astype(BF16).astype(F32)
    return hi, mid, lo


def _mlstm_kernel(*refs, ctx_out):
    (qs_l, ks_l, vt_l, ot_l, gt_l, qs_c, ks_c, vt_c, ot_c, gt_c, bgt_ref) = refs[:11]
    if ctx_out:
        y_l, y_c = refs[11:13]
        scratch = refs[13:]
    else:
        y_l = refs[11]
        y_c = None
        scratch = refs[12:]
    nrows = qs_l.shape[0]
    ns = 2 * nrows
    hacc_l, hacc_c = scratch[0:ns], scratch[ns:2 * ns]
    cst = scratch[2 * ns:2 * ns + ns * ML_HEADS]
    mst = scratch[2 * ns + ns * ML_HEADS:]
    L = ML_CHUNK
    row = lax.broadcasted_iota(jnp.int32, (L, L), 0)
    col = lax.broadcasted_iota(jnp.int32, (L, L), 1)
    triu_b = (row <= col).astype(BF16)
    krow = lax.broadcasted_iota(jnp.int32, (32, L), 0)
    zeros8 = jnp.zeros((8, L), F32)

    for ref in list(cst) + list(mst):
        ref[...] = jnp.zeros_like(ref)

    chains = [(si, hh) for si in range(ns) for hh in range(ML_HEADS)]
    own_l = [(col < ML_HEAD_DIM) if hh % 2 == 0 else (col >= ML_HEAD_DIM) for hh in range(ML_HEADS)]
    own_r = [(row < ML_HEAD_DIM) if hh % 2 == 0 else (row >= ML_HEAD_DIM) for hh in range(ML_HEADS)]
    mask = [row <= col, row >= col]
    sel = [jnp.where((krow & 7) == hh, 1.0, 0.0).astype(BF16) for hh in range(ML_HEADS)]

    def chunk_pair(qs, ks, vt, gt, hacc, cs, need_h):
        r0 = [pl.multiple_of(c * L, L) for c in cs]
        cum, tot, b, m_st, wk, decay, lhs = [], [], [], [], [], [], []
        for d in range(ns):
            rr, dd = d // 2, d % 2
            g = gt[rr, cs[dd]] + bgt_ref[0]
            li = g[16 * dd:16 * dd + 8]
            gf = g[16 * dd + 8:16 * dd + 16]
            ls = jnp.minimum(gf, 0.0) - jnp.log(1.0 + jnp.exp(-jnp.abs(gf)))
            cu = sum(_dot(piece.astype(BF16), triu_b) for piece in _split3(ls))
            tt = cu[:, L - 1:L]
            if dd == 1:
                cu = tt - cu + ls
            bb = li - cu
            ms = mst[d][...]
            lw = tt + bb
            m_new = jnp.maximum(tt + ms, jnp.max(lw, axis=1, keepdims=True))
            wk.append(jnp.exp(lw - m_new))
            decay.append(jnp.exp(tt + ms - m_new))
            mst[d][...] = m_new
            cum.append(cu)
            tot.append(tt)
            b.append(bb)
            m_st.append(ms)
            if need_h:
                lhs.append(jnp.concatenate(list(_split3(bb)) + [zeros8], axis=0).astype(BF16))
        kh, v_aug, c_aug, qh, kq, cq, bm = [], [], [], [], [], [], []
        for d, hh in chains:
            rr, dd = d // 2, d % 2
            psl = slice(LANES * (hh // 2), LANES * (hh // 2 + 1))
            kh.append(ks[rr, pl.ds(r0[dd], L), psl])
            vb = vt[rr, cs[dd], psl, :]
            v_aug.append(jnp.where(own_r[hh], vb, jnp.ones_like(vb)))
            c_aug.append(cst[d * ML_HEADS + hh][...])
            if need_h:
                qb = qs[rr, pl.ds(r0[dd], L), psl]
                qh.append(jnp.where(own_l[hh], qb, jnp.zeros_like(qb)))
        if need_h:
            for i, (d, hh) in enumerate(chains):
                bm.append(jnp.where(mask[d % 2], _dot_tn(lhs[d], sel[hh]), NEG))
            for i in range(len(chains)):
                kq.append(_dot_nt(kh[i], qh[i]))
            for i in range(len(chains)):
                cq.append(_dot_nt(c_aug[i].astype(BF16), qh[i]))
            mu, st = [], []
            for i, (d, hh) in enumerate(chains):
                mu.append(jnp.maximum(m_st[d][hh:hh + 1, :], jnp.max(bm[i], axis=0, keepdims=True)))
            for i in range(len(chains)):
                st.append((kq[i] * jnp.exp(bm[i] - mu[i])).astype(BF16))
            r = []
            for i, (d, hh) in enumerate(chains):
                r.append(_dot(v_aug[i], st[i]) + cq[i] * jnp.exp(m_st[d][hh:hh + 1, :] - mu[i]))
            hs = []
            for i, (d, hh) in enumerate(chains):
                e_negm = jnp.exp(-(cum[d][hh:hh + 1, :] + mu[i]))
                den = pltpu.roll(r[i], ML_HEAD_DIM, axis=0)
                hs.append(r[i] / jnp.maximum(jnp.abs(den), e_negm))
            for d in range(ns):
                for p in range(2):
                    i = d * ML_HEADS + 2 * p
                    hacc[d][cs[d % 2], LANES * p:LANES * (p + 1), :] = jnp.where(row < ML_HEAD_DIM, hs[i], hs[i + 1])
        dc = []
        for i, (d, hh) in enumerate(chains):
            vw = (v_aug[i].astype(F32) * wk[d][hh:hh + 1, :]).astype(BF16)
            dc.append(_dot(vw, kh[i]))
        for i, (d, hh) in enumerate(chains):
            cst[i][...] = decay[d][hh:hh + 1, :] * c_aug[i] + dc[i]

    def scan(qs, ks, vt, gt, hacc, need_h):
        nc = gt.shape[1]

        def body(j, carry):
            chunk_pair(qs, ks, vt, gt, hacc, (j, nc - 1 - j), need_h)
            return carry

        lax.fori_loop(0, nc, body, 0, unroll=2)

    scan(qs_c, ks_c, vt_c, gt_c, hacc_c, ctx_out)
    scan(qs_l, ks_l, vt_l, gt_l, hacc_l, True)

    def finish(ot, hacc, y):
        def body(c, carry):
            for rr in range(nrows):
                gate = _sigmoid(ot[rr, c].astype(F32))
                y[rr, c] = (gate * (hacc[2 * rr][c] + hacc[2 * rr + 1][c])).astype(y.dtype)
            return carry

        lax.fori_loop(0, ot.shape[1], body, 0)

    finish(ot_l, hacc_l, y_l)
    if ctx_out:
        finish(ot_c, hacc_c, y_c)


def _mlstm(lat, ctx, pw, l, ctx_out):
    b, n, _ = lat[0].shape
    nc = ctx[0].shape[1]
    nr = ML_ROWS if b % ML_ROWS == 0 else 1
    ns = 2 * nr
    whole = lambda a: pl.BlockSpec((nr,) + a.shape[1:], lambda bi: (bi,) + (0,) * (a.ndim - 1))
    ins = list(lat) + list(ctx)
    tshape = lambda m: (b, m // LANES, 256, LANES)
    out_shape = [jax.ShapeDtypeStruct(tshape(n), BF16)]
    out_specs = [pl.BlockSpec((nr,) + tshape(n)[1:], lambda bi: (bi, 0, 0, 0))]
    if ctx_out:
        out_shape.append(jax.ShapeDtypeStruct(tshape(nc), BF16))
        out_specs.append(pl.BlockSpec((nr,) + tshape(nc)[1:], lambda bi: (bi, 0, 0, 0)))
    res = pl.pallas_call(
        functools.partial(_mlstm_kernel, ctx_out=ctx_out),
        grid=(b // nr,),
        in_specs=[whole(a) for a in ins] + [_layer_spec(pw["bgt"], l)],
        out_specs=out_specs,
        out_shape=out_shape,
        scratch_shapes=[pltpu.VMEM((n // LANES, 256, LANES), F32)] * ns
                       + [pltpu.VMEM((nc // LANES, 256, LANES), F32)] * ns
                       + [pltpu.VMEM((LANES, LANES), F32)] * (ns * ML_HEADS)
                       + [pltpu.VMEM((8, LANES), F32)] * ns,
        compiler_params=_cparams(("parallel",)),
        name="mlstm",
    )(*ins, pw["bgt"])
    return (res[0], res[1]) if ctx_out else (res[0], None)


MERGE_GROUPS = 2


def _merge_kernel(x_ref, mod_ref, ya_ref, yb_ref, yc_ref, yd_ref, gpre_ref, gpost_ref, gffn_ref,
                  wg_ref, bgate_ref, wb_ref, wo_ref, x1_ref, h2_ref):
    m = mod_ref[0, 0]
    tm = x_ref.shape[1]
    groups = min(MERGE_GROUPS, tm // LANES)
    th = tm // groups
    rows = [slice(a * th, (a + 1) * th) for a in range(groups)]
    xs = [x_ref[0, r, :] for r in rows]
    hbs = [(_rms(x, gpre_ref[0]) * (1.0 + m[1:2]) + m[0:1]).astype(BF16) for x in xs]
    cpg = th // LANES
    accs = [None] * groups
    for s, y_ref in enumerate((ya_ref, yb_ref, yc_ref, yd_ref)):
        for a, r in enumerate(rows):
            gate = _sigmoid(_dot(hbs[a], wg_ref[0, s]) + bgate_ref[0, s:s + 1, :])
            if s == 1:
                yb_t = jnp.concatenate([yb_ref[0, a * cpg + j] for j in range(cpg)], axis=1)
                branch = _dot_tn(yb_t, wb_ref[0, s])
            else:
                branch = _dot(y_ref[0, r, :], wb_ref[0, s])
            term = gate * branch
            accs[a] = term if accs[a] is None else accs[a] + term
    ys = [_dot(acc.astype(BF16), wo_ref[0]) for acc in accs]
    for a, r in enumerate(rows):
        x1 = xs[a] + m[2:3] * _rms(ys[a], gpost_ref[0])
        x1_ref[0, r, :] = x1
        h2_ref[0, r, :] = (_rms(x1, gffn_ref[0]) * (1.0 + m[4:5]) + m[3:4]).astype(BF16)


def _merge(x, mod, ctx_row, ys, pw, l, tm):
    b, n, d = x.shape
    tm = min(tm, n)
    row = lambda bi, i: (bi, i, 0)
    names = ("g_pre_mix", "g_post_mix", "g_pre_ffn", "wg", "b_gate", "wb", "wo")
    return pl.pallas_call(
        _merge_kernel,
        grid=(b, n // tm),
        in_specs=[pl.BlockSpec((1, tm, d), row), _mod_spec(l, ctx_row)]
                 + [pl.BlockSpec((1, tm, BRANCH_W), row),
                    pl.BlockSpec((1, tm // LANES, BRANCH_W, LANES), lambda bi, i: (bi, i, 0, 0)),
                    pl.BlockSpec((1, tm, BRANCH_W), row), pl.BlockSpec((1, tm, BRANCH_W), row)]
                 + [_layer_spec(pw[k], l) for k in names],
        out_specs=[pl.BlockSpec((1, tm, d), row), pl.BlockSpec((1, tm, d), row)],
        out_shape=[jax.ShapeDtypeStruct((b, n, d), F32), jax.ShapeDtypeStruct((b, n, d), BF16)],
        compiler_params=_cparams(("parallel", "parallel")),
        name="merge",
    )(x, mod, *ys, *[pw[k] for k in names])


FFN_HALO = 16
FFN_COL_CHUNKS = ((0, 1536), (1536, 2816))


def _ffn_kernel(h_ref, hp_ref, hn_ref, x1_ref, mod_ref, gpost_ref, wup_ref, wconv_ref, bconv_ref, wdown_ref,
                o_ref):
    i = pl.program_id(1)
    last = pl.num_programs(1) - 1
    tm = h_ref.shape[1]
    hm = h_ref[0]
    hp = jnp.where(i > 0, hp_ref[0], jnp.zeros_like(hp_ref[0]))
    hn = jnp.where(i < last, hn_ref[0], jnp.zeros_like(hn_ref[0]))
    hext = jnp.concatenate([hp, hm, hn], axis=0)
    ne = tm + 2 * FFN_HALO
    acc = None
    for lo, hi in FFN_COL_CHUNKS:
        a = _dot(hext, wup_ref[0, :, lo:hi])
        ap = pltpu.roll(a, 1, axis=0)[FFN_HALO:FFN_HALO + tm]
        an = pltpu.roll(a, ne - 1, axis=0)[FFN_HALO:FFN_HALO + tm]
        ac = a[FFN_HALO:FFN_HALO + tm]
        a = (ap * wconv_ref[0, 0:1, lo:hi] + ac * wconv_ref[0, 1:2, lo:hi] + an * wconv_ref[0, 2:3, lo:hi]
             + bconv_ref[0, :, lo:hi])
        v = _dot(hm, wup_ref[0, :, D_FF + lo:D_FF + hi])
        act = (a * _sigmoid(a) * v).astype(BF16)
        term = _dot(act, wdown_ref[0, lo:hi, :])
        acc = term if acc is None else acc + term
    m = mod_ref[0, 0]
    o_ref[0] = x1_ref[0] + m[5:6] * _rms(acc, gpost_ref[0])


def _ffn(h2, x1, mod, ctx_row, pw, l, tm):
    b, n, d = x1.shape
    tm = min(tm, n)
    row = lambda bi, i: (bi, i, 0)
    r = tm // FFN_HALO
    nblk = n // FFN_HALO
    names = ("g_post_ffn", "wup", "w_ffn_conv", "b_ffn_conv", "wdown")
    return pl.pallas_call(
        _ffn_kernel,
        grid=(b, n // tm),
        in_specs=[pl.BlockSpec((1, tm, d), row),
                  pl.BlockSpec((1, FFN_HALO, d), lambda bi, i: (bi, jnp.maximum(i * r - 1, 0), 0)),
                  pl.BlockSpec((1, FFN_HALO, d), lambda bi, i: (bi, jnp.minimum((i + 1) * r, nblk - 1), 0)),
                  pl.BlockSpec((1, tm, d), row),
                  _mod_spec(l, ctx_row)]
                 + [_layer_spec(pw[k], l) for k in names],
        out_specs=pl.BlockSpec((1, tm, d), row),
        out_shape=jax.ShapeDtypeStruct((b, n, d), F32),
        compiler_params=_cparams(("parallel", "parallel")),
        name="conv_ffn",
    )(h2, h2, h2, x1, mod, *[pw[k] for k in names])


def _rope_cos_sin(n, d, identity):
    if identity:
        return jnp.ones((n, d), F32), jnp.zeros((n, d), F32)
    half, nf = d // 2, d // 4
    t = jnp.arange(n, dtype=jnp.int32)
    row = (t // GRID_W).astype(F32)[:, None]
    colp = (t % GRID_W).astype(F32)[:, None]
    inv = ROPE_BASE ** (-jnp.arange(nf, dtype=F32) / nf)
    ang = jnp.concatenate([row * inv, row * inv, colp * inv, colp * inv], axis=-1)
    sign = jnp.asarray(np.where(np.arange(d) % half < nf, -1.0, 1.0), F32)
    return jnp.cos(ang), jnp.sin(ang) * sign


def _rope_table(n, identity):
    ca, sa = _rope_cos_sin(n, MLA_ROPE, identity)
    cw, sw = _rope_cos_sin(n, HEAD_DIM, identity)
    scale_a = (MLA_NOPE + MLA_ROPE) ** -0.5 * LOG2E
    scale_w = HEAD_DIM ** -0.5 * LOG2E
    z32 = jnp.zeros((n, LANES - MLA_NOPE - MLA_ROPE), F32)
    one64 = jnp.ones((n, MLA_NOPE), F32)
    zero64 = jnp.zeros((n, MLA_NOPE), F32)
    qa_cos = jnp.concatenate([one64, ca, z32], -1) * scale_a
    qa_sin = jnp.concatenate([zero64, sa, z32], -1) * scale_a
    ka_cos = jnp.concatenate([zero64, ca, z32], -1)
    ka_sin = jnp.concatenate([zero64, sa, z32], -1)
    cw2 = jnp.concatenate([cw, cw], -1)
    sw2 = jnp.concatenate([sw, sw], -1)
    return jnp.concatenate([qa_cos, qa_sin, ka_cos, ka_sin, cw2 * scale_w, sw2 * scale_w, cw2, sw2], axis=-1)


def _dft_tables(n):
    n1 = n // FN_GROUP_W
    k = jnp.arange(n, dtype=jnp.int32)[:, None]
    ang1 = ((k * jnp.arange(n1, dtype=jnp.int32)[None, :]) % n1).astype(F32) * (2.0 * math.pi / n1)
    ang0 = ((k * jnp.arange(FN_GROUP_W, dtype=jnp.int32)[None, :]) % n).astype(F32) * (2.0 * math.pi / n)
    c1, s1 = jnp.cos(ang1)[:, :, None], jnp.sin(ang1)[:, :, None]
    c0, s0 = jnp.cos(ang0)[:, None, :], jnp.sin(ang0)[:, None, :]
    cos_kt = (c1 * c0 - s1 * s0).reshape(n, n)
    sin_kt = (s1 * c0 + c1 * s0).reshape(n, n)
    half = n // 2
    col = jnp.arange(half, dtype=jnp.int32)[None, :]
    dft = jnp.concatenate([cos_kt[:, :half], jnp.where(col == 0, cos_kt[:, half:half + 1], -sin_kt[:, :half])],
                          axis=-1).astype(BF16)
    j = np.arange(FN_GROUP_W)
    a64 = (np.outer(j, j) % FN_GROUP_W) * (2.0 * np.pi / FN_GROUP_W)
    eye = np.eye(FN_GROUPS)
    cs = np.concatenate([np.kron(eye, np.cos(a64)), np.kron(eye, np.sin(a64))], axis=-1)
    return dft, jnp.asarray(cs, F32).astype(BF16)


def _pad_last(a, width):
    return jnp.pad(a, [(0, 0)] * (a.ndim - 1) + [(0, width - a.shape[-1])])


def _prep_weights(p):
    w_in = p["w_in"]
    depth, d, _ = w_in.shape
    vec = lambda a: a.reshape(depth, 1, a.shape[-1])
    off = [int(o) for o in np.concatenate([[0], np.cumsum(IN_SPLITS)])]
    seg = lambda i, j=None: w_in[:, :, off[i]:off[i + 1 if j is None else j]]
    dup = lambda t: jnp.broadcast_to(t.reshape(depth, d, WG_KV_HEADS, 1, HEAD_DIM),
                                     (depth, d, WG_KV_HEADS, 2, HEAD_DIM)).reshape(depth, d, 4 * HEAD_DIM)
    zeros = lambda w: jnp.zeros((depth, d, w), F32)
    w1 = jnp.concatenate([seg(0, 2), zeros(MLA_NOPE), seg(2), zeros(LANES - MLA_NOPE - MLA_ROPE),
                          seg(8), dup(seg(9)), dup(seg(10)), seg(11), seg(3, 5)], axis=-1)
    assert w1.shape[-1] == C_END
    gates = _pad_last(seg(7).reshape(depth, d, 4, ML_HEADS), 8).reshape(depth, d, GT_ROWS)
    w1t = jnp.swapaxes(jnp.concatenate([seg(5, 7), gates], axis=-1), 1, 2)
    assert w1t.shape[1] == R_END
    bgt = _pad_last(p["b_ml_gates"].reshape(depth, 4, ML_HEADS), 8).reshape(depth, GT_ROWS)
    bgt = jnp.broadcast_to(bgt[:, :, None], (depth, GT_ROWS, LANES))
    rq, rkv = p["w_uq"].shape[1], p["w_ukv"].shape[1]
    wq = _pad_last(p["w_uq"].reshape(depth, rq, MLA_HEADS, MLA_NOPE + MLA_ROPE), LANES).reshape(depth, rq, -1)
    kv = p["w_ukv"].reshape(depth, rkv, MLA_HEADS, MLA_NOPE + MLA_V)
    wkv = jnp.concatenate([_pad_last(kv[..., :MLA_NOPE], LANES).reshape(depth, rkv, -1),
                           kv[..., MLA_NOPE:].reshape(depth, rkv, -1)], axis=-1)
    return {
        "w1": w1.astype(BF16),
        "w1t": w1t.astype(BF16),
        "wq": wq.astype(BF16),
        "wkv": wkv.astype(BF16),
        "g_qa": vec(p["g_qa"]), "g_kva": vec(p["g_kva"]),
        "g_pre_mix": vec(p["g_pre_mix"]), "g_post_mix": vec(p["g_post_mix"]),
        "g_pre_ffn": vec(p["g_pre_ffn"]), "g_post_ffn": vec(p["g_post_ffn"]),
        "w_ml_conv": p["w_ml_conv"], "bgt": bgt, "wg_sink": p["wg_sink"],
        "wg": p["w_gate"].astype(BF16), "b_gate": p["b_gate"],
        "wb": p["w_branch"].astype(BF16), "wo": p["w_out"].astype(BF16),
        "wup": p["w_up"].astype(BF16), "w_ffn_conv": p["w_ffn_conv"],
        "b_ffn_conv": vec(p["b_ffn_conv"]), "wdown": p["w_down"].astype(BF16),
    }


TM = 512


def kernel(x, c, ctx, c_ctx, w_mod, b_mod, g_pre_mix, g_post_mix, g_pre_ffn, g_post_ffn, w_in, g_qa, w_uq, g_kva,
           w_ukv, w_ml_conv, b_ml_gates, wg_sink, w_gate, b_gate, w_branch, w_out, w_up, w_ffn_conv, b_ffn_conv,
           w_down):
    p = dict(g_pre_mix=g_pre_mix, g_post_mix=g_post_mix, g_pre_ffn=g_pre_ffn, g_post_ffn=g_post_ffn, w_in=w_in,
             g_qa=g_qa, w_uq=w_uq, g_kva=g_kva, w_ukv=w_ukv, w_ml_conv=w_ml_conv, b_ml_gates=b_ml_gates,
             wg_sink=wg_sink, w_gate=w_gate, b_gate=b_gate, w_branch=w_branch, w_out=w_out, w_up=w_up,
             w_ffn_conv=w_ffn_conv, b_ffn_conv=b_ffn_conv, w_down=w_down)
    bsz, n, d = x.shape
    n_ctx = ctx.shape[1]
    depth = w_mod.shape[0]
    rows = -(-(bsz + 1) // 8) * 8
    c_all = jnp.concatenate([c, c_ctx[None, :], jnp.zeros((rows - bsz - 1, d), F32)], axis=0)
    mod = _modulation(c_all, w_mod, b_mod).reshape(depth, rows, 6, d)
    pw = _prep_weights(p)
    tab_l = _rope_table(n, identity=False)
    tab_c = _rope_table(n_ctx, identity=True)
    dft_l, cs64 = _dft_tables(n)
    dft_c, _ = _dft_tables(n_ctx)
    xc = ctx
    for l in range(depth):
        ctx_out = l < depth - 1
        qa, ka, va, uq, uk, vt, ot, gt, qw, kw, vw, uf = _inproj(x, mod, None, pw, l, tab_l, TM)
        qa_c, ka_c, va_c, uq_c, uk_c, vt_c, ot_c, gt_c, qw_c, kw_c, vw_c, uf_c = _inproj(
            xc, mod, bsz, pw, l, tab_c, TM)
        ya = _mla_attend(qa, ka_c, va_c, ka, va)
        yb, yb_c = _mlstm((uq, uk, vt, ot, gt), (uq_c, uk_c, vt_c, ot_c, gt_c), pw, l, ctx_out)
        yc = _win_attend(pw["wg_sink"], l, qw, kw_c, vw_c, kw, vw)
        yd = _fourier(uf, cs64, dft_l)
        x1, h2 = _merge(x, mod, None, (ya, yb, yc, yd), pw, l, TM)
        x = _ffn(h2, x1, mod, None, pw, l, TM)
        if ctx_out:
            ya_c = _mla_attend(qa_c, ka_c, va_c)
            yc_c = _win_attend(pw["wg_sink"], l, qw_c, kw_c, vw_c)
            yd_c = _fourier(uf_c, cs64, dft_c)
            xc1, hc2 = _merge(xc, mod, bsz, (ya_c, yb_c, yc_c, yd_c), pw, l, TM)
            xc = _ffn(hc2, xc1, mod, bsz, pw, l, TM)
    return x
```

```python
import functools
import math

import numpy as np
import jax
import jax.numpy as jnp
from jax import lax
from jax.experimental import pallas as pl
from jax.experimental.pallas import tpu as pltpu

F32 = jnp.float32
BF16 = jnp.bfloat16

D_MODEL = 1024
GRID_W = 64
N_BRANCH = 4
BRANCH_W = 256
MLA_HEADS = 4
MLA_NOPE = 64
MLA_ROPE = 32
MLA_V = 64
MLA_Q_RANK = 256
MLA_KV_RANK = 256
ML_HEADS = 4
ML_HEAD_DIM = 64
ML_CHUNK = 128
WG_HEADS = 4
WG_KV_HEADS = 2
HEAD_DIM = 64
WINDOW = 128
FN_GROUPS = 4
FN_GROUP_W = 64
D_FF = 2816
ROPE_BASE = 10000.0
EPS = 1e-6
IN_SPLITS = (MLA_Q_RANK, MLA_KV_RANK, MLA_ROPE, 256, 256, 256, 256, 16, 256, 128, 128, 256)

LANES = 128
NEG = -1e30
LOG2E = 1.4426950408889634
VMEM_LIMIT = 56 * 1024 * 1024

C_CQ, C_CKV, C_KR, C_WQ, C_WK, C_WV, C_UF, C_UQ, C_UK, C_END = (
    0, 256, 512, 640, 896, 1152, 1408, 1664, 1920, 2176)
IN_HALO = 16
R_UV, R_UO, R_GT, R_END = 0, 256, 512, 544
GT_ROWS = 32
T_QA_COS, T_QA_SIN, T_KA_COS, T_KA_SIN, T_QW_COS, T_QW_SIN, T_KW_COS, T_KW_SIN = range(8)


def _cparams(sem):
    return pltpu.CompilerParams(dimension_semantics=sem, vmem_limit_bytes=VMEM_LIMIT)


def _layer_spec(arr, l):
    nd = arr.ndim - 1
    return pl.BlockSpec((1,) + arr.shape[1:], lambda *_: (l,) + (0,) * nd, pipeline_mode=pl.Buffered(1))


def _const_spec(shape):
    nd = len(shape)
    return pl.BlockSpec(shape, lambda *_: (0,) * nd, pipeline_mode=pl.Buffered(1))


def _mod_spec(l, ctx_row):
    if ctx_row is None:
        return pl.BlockSpec((1, 1, 6, D_MODEL), lambda bi, *_: (l, bi, 0, 0))
    return pl.BlockSpec((1, 1, 6, D_MODEL), lambda *_: (l, ctx_row, 0, 0))


def _rms(x, g):
    ms = jnp.mean(x * x, axis=-1, keepdims=True)
    return x * lax.rsqrt(ms + EPS) * g


def _dot(a, b):
    return jnp.dot(a, b, preferred_element_type=F32)


def _dot_nt(a, b):
    return lax.dot_general(a, b, (((1,), (1,)), ((), ())), preferred_element_type=F32)


def _dot_tn(a, b):
    return lax.dot_general(a, b, (((0,), (0,)), ((), ())), preferred_element_type=F32)


def _sigmoid(x):
    return 1.0 / (1.0 + jnp.exp(-x))


def _mod_kernel(c_ref, w_ref, b_ref, o_ref):
    c = c_ref[...]
    s = (c * _sigmoid(c)).astype(BF16)
    o_ref[0] = _dot(s, w_ref[0].astype(BF16)) + b_ref[0]


def _modulation(c_all, w_mod, b_mod):
    depth, d, n = w_mod.shape
    rows = c_all.shape[0]
    tn = 1536
    return pl.pallas_call(
        _mod_kernel,
        grid=(depth, n // tn),
        in_specs=[pl.BlockSpec((rows, d), lambda l, j: (0, 0)),
                  pl.BlockSpec((1, d, tn), lambda l, j: (l, 0, j)),
                  pl.BlockSpec((1, 1, tn), lambda l, j: (l, 0, j))],
        out_specs=pl.BlockSpec((1, rows, tn), lambda l, j: (l, 0, j)),
        out_shape=jax.ShapeDtypeStruct((depth, rows, n), F32),
        compiler_params=_cparams(("parallel", "parallel")),
        name="modulation",
    )(c_all, w_mod, b_mod.reshape(depth, 1, n))


def _inproj_kernel(x_ref, xp_ref, xn_ref, mod_ref, gpre_ref, w1_ref, w1t_ref, gqa_ref, wq_ref, gkva_ref, wkv_ref,
                   wconv_ref, tab_ref,
                   qa_ref, ka_ref, va_ref, qs_ref, ks_ref, vt_ref, ot_ref, gt_ref,
                   qw_ref, kw_ref, vw_ref, uf_ref):
    i = pl.program_id(1)
    last = pl.num_programs(1) - 1
    tm = x_ref.shape[1]
    m = mod_ref[0, 0]

    def modulated(xv):
        return (_rms(xv, gpre_ref[0]) * (1.0 + m[1:2]) + m[0:1]).astype(BF16)

    def tab(j):
        return tab_ref[:, LANES * j:LANES * (j + 1)]

    lane = lax.broadcasted_iota(jnp.int32, (tm, LANES), 1)

    def rope(xv, cos_slot, sin_slot, nf):
        first = (lane & (2 * nf - 1)) < nf
        blocks = []
        for j in range(xv.shape[1] // LANES):
            xb = xv[:, LANES * j:LANES * (j + 1)]
            partner = jnp.where(first, pltpu.roll(xb, LANES - nf, axis=1), pltpu.roll(xb, nf, axis=1))
            blocks.append(xb * tab(cos_slot) + partner * tab(sin_slot))
        return blocks[0] if len(blocks) == 1 else jnp.concatenate(blocks, axis=-1)

    hb = modulated(x_ref[0])
    u = _dot(hb, w1_ref[0, :, 0:C_UQ])

    def proj(lo, hi):
        return u[:, lo:hi]

    def proj_t(lo, hi, out_ref):
        r = _dot_nt(w1t_ref[0, lo:hi, :], hb)
        for j in range(out_ref.shape[1]):
            out_ref[0, j] = r[:, LANES * j:LANES * (j + 1)].astype(out_ref.dtype)

    hp = modulated(xp_ref[0])
    hn = modulated(xn_ref[0])
    hext = jnp.concatenate([jnp.where(i > 0, hp, jnp.zeros_like(hp)), hb,
                            jnp.where(i < last, hn, jnp.zeros_like(hn))], axis=0)
    uqk = _dot(hext, w1_ref[0, :, C_UQ:C_END])
    ne = tm + 2 * IN_HALO
    wc = wconv_ref[0]
    a = (pltpu.roll(uqk, 1, axis=0)[IN_HALO:IN_HALO + tm] * wc[0:1, :] + uqk[IN_HALO:IN_HALO + tm] * wc[1:2, :]
         + pltpu.roll(uqk, ne - 1, axis=0)[IN_HALO:IN_HALO + tm] * wc[2:3, :])
    a = a * _sigmoid(a)
    qs_ref[0] = a[:, :256].astype(BF16)
    ks_ref[0] = (a[:, 256:] * ML_HEAD_DIM ** -0.5).astype(BF16)
    cq = _rms(proj(C_CQ, C_CKV), gqa_ref[0]).astype(BF16)
    qa_ref[0] = rope(_dot(cq, wq_ref[0]), T_QA_COS, T_QA_SIN, MLA_ROPE // 4).astype(BF16)
    ckv = _rms(proj(C_CKV, C_KR), gkva_ref[0]).astype(BF16)
    kv = _dot(ckv, wkv_ref[0])
    kr = rope(proj(C_KR, C_WQ), T_KA_COS, T_KA_SIN, MLA_ROPE // 4)
    ka_ref[0] = (kv[:, :512] + jnp.concatenate([kr] * MLA_HEADS, axis=-1)).astype(BF16)
    va_ref[0] = kv[:, 512:].astype(BF16)
    proj_t(R_UV, R_UO, vt_ref)
    proj_t(R_UO, R_GT, ot_ref)
    proj_t(R_GT, R_END, gt_ref)
    qw_ref[0] = rope(proj(C_WQ, C_WK), T_QW_COS, T_QW_SIN, HEAD_DIM // 4).astype(BF16)
    kw_ref[0] = rope(proj(C_WK, C_WV), T_KW_COS, T_KW_SIN, HEAD_DIM // 4).astype(BF16)
    vw_ref[0] = proj(C_WV, C_UF).astype(BF16)
    uf_ref[0] = proj(C_UF, C_UQ).astype(BF16)


def _inproj(x, mod, ctx_row, pw, l, tab, tm):
    b, n, d = x.shape
    tm = min(tm, n)
    row = lambda bi, i: (bi, i, 0)
    cpt = tm // LANES

    def out(w, dt):
        return jax.ShapeDtypeStruct((b, n, w), dt), pl.BlockSpec((1, tm, w), row)

    def out_t(w, dt):
        return (jax.ShapeDtypeStruct((b, n // LANES, w, LANES), dt),
                pl.BlockSpec((1, cpt, w, LANES), lambda bi, i: (bi, i, 0, 0)))

    outs = [out(512, BF16), out(512, BF16), out(256, BF16),
            out(256, BF16), out(256, BF16), out_t(256, BF16), out_t(256, BF16), out_t(GT_ROWS, F32),
            out(256, BF16), out(256, BF16), out(256, BF16), out(256, BF16)]
    names = ("g_pre_mix", "w1", "w1t", "g_qa", "wq", "g_kva", "wkv", "w_ml_conv")
    r = tm // IN_HALO
    nblk = n // IN_HALO
    return pl.pallas_call(
        _inproj_kernel,
        grid=(b, n // tm),
        in_specs=[pl.BlockSpec((1, tm, d), row),
                  pl.BlockSpec((1, IN_HALO, d), lambda bi, i: (bi, jnp.maximum(i * r - 1, 0), 0)),
                  pl.BlockSpec((1, IN_HALO, d), lambda bi, i: (bi, jnp.minimum((i + 1) * r, nblk - 1), 0)),
                  _mod_spec(l, ctx_row)]
                 + [_layer_spec(pw[k], l) for k in names]
                 + [pl.BlockSpec((tm, 8 * LANES), lambda bi, i: (i, 0))],
        out_specs=[o[1] for o in outs],
        out_shape=[o[0] for o in outs],
        compiler_params=_cparams(("parallel", "parallel")),
        name="inproj",
    )(x, x, x, mod, *[pw[k] for k in names], tab)


def _mla_kernel(*refs, has_lat):
    if has_lat:
        q_ref, kc_ref, vc_ref, kl_ref, vl_ref, o_ref, kcat, vaug = refs
    else:
        q_ref, kc_ref, vc_ref, o_ref, kcat, vaug = refs
    nc = kc_ref.shape[1]
    nk = kcat.shape[0]

    @pl.when(pl.program_id(1) == 0)
    def _():
        kcat[0:nc, :] = kc_ref[0]
        if has_lat:
            kcat[nc:nk, :] = kl_ref[0]
        for hh in range(MLA_HEADS):
            vsl = slice(LANES * (hh // 2), LANES * (hh // 2 + 1))

            def keep(v):
                lane = lax.broadcasted_iota(jnp.int32, v.shape, 1)
                own = (lane < MLA_V) if hh % 2 == 0 else (lane >= MLA_V)
                return jnp.where(own, v, jnp.zeros_like(v))

            vaug[hh, 0:nc, 0:LANES] = keep(vc_ref[0, :, vsl])
            if has_lat:
                vaug[hh, nc:nk, 0:LANES] = keep(vl_ref[0, :, vsl])
            vaug[hh, :, LANES:2 * LANES] = jnp.ones((nk, LANES), BF16)

    def logits(hh):
        hsl = slice(LANES * hh, LANES * (hh + 1))
        return _dot_nt(q_ref[0, :, hsl], kcat[:, hsl])

    def attend(hh, s):
        mx = jnp.max(s, axis=-1, keepdims=True)
        o2 = _dot(jnp.exp2(s - mx).astype(BF16), vaug[hh])
        return o2[:, :LANES] / o2[:, LANES:]

    outs = []
    s_next = logits(0)
    for hh in range(MLA_HEADS):
        s_cur = s_next
        if hh + 1 < MLA_HEADS:
            s_next = logits(hh + 1)
        outs.append(attend(hh, s_cur))
    o_ref[0] = jnp.concatenate([outs[0] + outs[1], outs[2] + outs[3]], axis=-1).astype(o_ref.dtype)


def _mla_attend(q, kc, vc, kl=None, vl=None, tq=512):
    b, n, _ = q.shape
    tq = min(tq, n)
    has_lat = kl is not None
    nk = kc.shape[1] + (kl.shape[1] if has_lat else 0)
    whole = lambda a: pl.BlockSpec((1,) + a.shape[1:], lambda bi, i: (bi, 0, 0))
    ins = [q, kc, vc] + ([kl, vl] if has_lat else [])
    specs = [pl.BlockSpec((1, tq, 512), lambda bi, i: (bi, i, 0))] + [whole(a) for a in ins[1:]]
    return pl.pallas_call(
        functools.partial(_mla_kernel, has_lat=has_lat),
        grid=(b, n // tq),
        in_specs=specs,
        out_specs=pl.BlockSpec((1, tq, 256), lambda bi, i: (bi, i, 0)),
        out_shape=jax.ShapeDtypeStruct((b, n, 256), BF16),
        scratch_shapes=[pltpu.VMEM((nk, 512), BF16), pltpu.VMEM((MLA_HEADS, nk, 2 * LANES), BF16)],
        compiler_params=_cparams(("parallel", "arbitrary")),
        name="mla_attend",
    )(*ins)


def _win_kernel(*refs, has_lat, n_lat, layer):
    if has_lat:
        sink_ref, q_ref, kc_ref, vc_ref, kl_ref, vl_ref, band_ref, o_ref = refs
    else:
        sink_ref, q_ref, kc_ref, vc_ref, o_ref = refs
    tq = q_ref.shape[1]
    wk = tq + 2 * WINDOW
    lane = lax.broadcasted_iota(jnp.int32, (tq, LANES), 1)
    if has_lat:
        t0 = pl.program_id(1) * tq
        start = pl.multiple_of(jnp.clip(t0 - WINDOW, 0, n_lat - wk), LANES)
        band = band_ref[0]

    def aug(v, e):
        vlane = lax.broadcasted_iota(jnp.int32, v.shape, 1)
        own = (vlane < HEAD_DIM) if e == 0 else (vlane >= HEAD_DIM)
        return jnp.concatenate([jnp.where(own, v, jnp.zeros_like(v)), jnp.ones(v.shape, BF16)], axis=1)

    heads = [(pr, e) for pr in range(WG_KV_HEADS) for e in range(2)]
    psl = [slice(LANES * pr, LANES * (pr + 1)) for pr in range(WG_KV_HEADS)]
    sink = [sink_ref[layer, hq] * LOG2E for hq in range(WG_HEADS)]
    s_c, s_l, mx = [], [], []
    for pr, e in heads:
        qp = q_ref[0, :, psl[pr]]
        own = (lane < HEAD_DIM) if e == 0 else (lane >= HEAD_DIM)
        q = jnp.where(own, qp, jnp.zeros_like(qp))
        s_c.append(_dot_nt(q, kc_ref[0, :, psl[pr]]))
        if has_lat:
            s_l.append(_dot_nt(q, kl_ref[0, pl.ds(start, wk), psl[pr]]) + band)
    for i in range(WG_HEADS):
        m = jnp.maximum(jnp.max(s_c[i], axis=-1, keepdims=True), sink[i])
        if has_lat:
            m = jnp.maximum(m, jnp.max(s_l[i], axis=-1, keepdims=True))
        mx.append(m)
    o2 = []
    for i, (pr, e) in enumerate(heads):
        o = _dot(jnp.exp2(s_c[i] - mx[i]).astype(BF16), aug(vc_ref[0, :, psl[pr]], e))
        if has_lat:
            o = o + _dot(jnp.exp2(s_l[i] - mx[i]).astype(BF16), aug(vl_ref[0, pl.ds(start, wk), psl[pr]], e))
        o2.append(o)
    outs = [o2[i][:, :LANES] / (o2[i][:, LANES:] + jnp.exp2(sink[i] - mx[i])) for i in range(WG_HEADS)]
    o_ref[0] = jnp.concatenate([outs[0] + outs[1], outs[2] + outs[3]], axis=-1).astype(o_ref.dtype)


def _win_attend(sink, layer, q, kc, vc, kl=None, vl=None, tq=256):
    b, n, _ = q.shape
    tq = min(tq, n)
    has_lat = kl is not None
    whole = lambda a: pl.BlockSpec((1,) + a.shape[1:], lambda bi, i: (bi, 0, 0))
    ins = [q, kc, vc] + ([kl, vl] if has_lat else [])
    specs = ([pl.BlockSpec(memory_space=pltpu.SMEM),
              pl.BlockSpec((1, tq, 256), lambda bi, i: (bi, i, 0))] + [whole(a) for a in ins[1:]])
    if has_lat:
        nt = n // tq
        wk = tq + 2 * WINDOW
        assert nt >= 2 and wk <= n
        rel = np.arange(wk)[None, :] - np.arange(tq)[:, None]
        shift = (0, WINDOW, 2 * WINDOW)
        band = np.stack([np.where(np.abs(rel - sh) <= WINDOW, 0.0, NEG) for sh in shift]).astype(np.float32)
        ins.append(jnp.asarray(band))
        specs.append(pl.BlockSpec((1, tq, wk), lambda bi, i: (jnp.where(i == 0, 0, jnp.where(i == nt - 1, 2, 1)), 0, 0)))
    return pl.pallas_call(
        functools.partial(_win_kernel, has_lat=has_lat, n_lat=n, layer=layer),
        grid=(b, n // tq),
        in_specs=specs,
        out_specs=pl.BlockSpec((1, tq, 256), lambda bi, i: (bi, i, 0)),
        out_shape=jax.ShapeDtypeStruct((b, n, 256), BF16),
        compiler_params=_cparams(("parallel", "parallel")),
        name="win_attend",
    )(sink, *ins)


def _fourier_kernel(u_ref, rev_ref, cs_ref, dft_ref, o_ref, ab_ref, *, tr):
    n = u_ref.shape[1]
    half = n // 2
    u_rev = _dot(rev_ref[...], u_ref[0, half:n, :]).astype(BF16)
    ab_a = _dot(u_ref[0, 0:half, :], cs_ref[...])
    ab_r = _dot(u_rev, cs_ref[...])
    a_mid = _dot(u_ref[0, half:half + 16, :], cs_ref[:, 0:256])[0:1, :]
    first = lax.broadcasted_iota(jnp.int32, (half, 256), 0) == 0
    ab_ref[0:half, :] = (ab_a[:, :256] + ab_r[:, :256]).astype(BF16)
    ab_ref[half:n, :] = jnp.where(first, a_mid, ab_a[:, 256:] - ab_r[:, 256:]).astype(BF16)
    scale = 1.0 / math.sqrt(n * FN_GROUP_W)

    def body(r, carry):
        r0 = pl.multiple_of(r * tr, tr)
        y = _dot(dft_ref[pl.ds(r0, tr), :], ab_ref[...])
        o_ref[0, pl.ds(r0, tr), :] = (y * scale).astype(o_ref.dtype)
        return carry

    lax.fori_loop(0, n // tr, body, 0)


def _fourier(u, cs, dft):
    b, n, w = u.shape
    tr = min(512, n)
    half = n // 2
    idx = jnp.arange(half, dtype=jnp.int32)
    rev = (idx[:, None] + idx[None, :] == half).astype(BF16)
    return pl.pallas_call(
        functools.partial(_fourier_kernel, tr=tr),
        grid=(b,),
        in_specs=[pl.BlockSpec((1, n, w), lambda bi: (bi, 0, 0)),
                  _const_spec((half, half)),
                  _const_spec(cs.shape),
                  _const_spec(dft.shape)],
        out_specs=pl.BlockSpec((1, n, w), lambda bi: (bi, 0, 0)),
        out_shape=jax.ShapeDtypeStruct((b, n, w), BF16),
        scratch_shapes=[pltpu.VMEM((n, w), BF16)],
        compiler_params=_cparams(("parallel",)),
        name="fourier",
    )(u, rev, cs, dft)


ML_ROWS = 2


def _split3(x):
    hi = x.astype(BF16).astype(F32)
    r1 = x - hi
    mid = r1.astype(BF16).astype(F32)
    lo = (r1 - mid).astype(BF16).astype(F32)
    return hi, mid, lo


def _mlstm_kernel(*refs, ctx_out):
    (qs_l, ks_l, vt_l, ot_l, gt_l, qs_c, ks_c, vt_c, ot_c, gt_c, bgt_ref) = refs[:11]
    if ctx_out:
        y_l, y_c = refs[11:13]
        scratch = refs[13:]
    else:
        y_l = refs[11]
        y_c = None
        scratch = refs[12:]
    nrows = qs_l.shape[0]
    ns = 2 * nrows
    hacc_l, hacc_c = scratch[0:ns], scratch[ns:2 * ns]
    cst = scratch[2 * ns:2 * ns + ns * ML_HEADS]
    mst = scratch[2 * ns + ns * ML_HEADS:]
    L = ML_CHUNK
    row = lax.broadcasted_iota(jnp.int32, (L, L), 0)
    col = lax.broadcasted_iota(jnp.int32, (L, L), 1)
    triu_b = (row <= col).astype(BF16)
    krow = lax.broadcasted_iota(jnp.int32, (32, L), 0)
    zeros8 = jnp.zeros((8, L), F32)

    for ref in list(cst) + list(mst):
        ref[...] = jnp.zeros_like(ref)

    chains = [(si, hh) for si in range(ns) for hh in range(ML_HEADS)]
    own_l = [(col < ML_HEAD_DIM) if hh % 2 == 0 else (col >= ML_HEAD_DIM) for hh in range(ML_HEADS)]
    own_r = [(row < ML_HEAD_DIM) if hh % 2 == 0 else (row >= ML_HEAD_DIM) for hh in range(ML_HEADS)]
    mask = [row <= col, row >= col]
    sel = [jnp.where((krow & 7) == hh, 1.0, 0.0).astype(BF16) for hh in range(ML_HEADS)]

    def chunk_pair(qs, ks, vt, gt, hacc, cs, need_h):
        r0 = [pl.multiple_of(c * L, L) for c in cs]
        cum, tot, b, m_st, wk, decay, lhs = [], [], [], [], [], [], []
        for d in range(ns):
            rr, dd = d // 2, d % 2
            g = gt[rr, cs[dd]] + bgt_ref[0]
            li = g[16 * dd:16 * dd + 8]
            gf = g[16 * dd + 8:16 * dd + 16]
            ls = jnp.minimum(gf, 0.0) - jnp.log(1.0 + jnp.exp(-jnp.abs(gf)))
            cu = sum(_dot(piece.astype(BF16), triu_b) for piece in _split3(ls))
            tt = cu[:, L - 1:L]
            if dd == 1:
                cu = tt - cu + ls
            bb = li - cu
            ms = mst[d][...]
            lw = tt + bb
            m_new = jnp.maximum(tt + ms, jnp.max(lw, axis=1, keepdims=True))
            wk.append(jnp.exp(lw - m_new))
            decay.append(jnp.exp(tt + ms - m_new))
            mst[d][...] = m_new
            cum.append(cu)
            tot.append(tt)
            b.append(bb)
            m_st.append(ms)
            if need_h:
                lhs.append(jnp.concatenate(list(_split3(bb)) + [zeros8], axis=0).astype(BF16))
        kh, v_aug, c_aug, qh, kq, cq, bm = [], [], [], [], [], [], []
        for d, hh in chains:
            rr, dd = d // 2, d % 2
            psl = slice(LANES * (hh // 2), LANES * (hh // 2 + 1))
            kh.append(ks[rr, pl.ds(r0[dd], L), psl])
            vb = vt[rr, cs[dd], psl, :]
            v_aug.append(jnp.where(own_r[hh], vb, jnp.ones_like(vb)))
            c_aug.append(cst[d * ML_HEADS + hh][...])
            if need_h:
                qb = qs[rr, pl.ds(r0[dd], L), psl]
                qh.append(jnp.where(own_l[hh], qb, jnp.zeros_like(qb)))
        if need_h:
            for i, (d, hh) in enumerate(chains):
                bm.append(jnp.where(mask[d % 2], _dot_tn(lhs[d], sel[hh]), NEG))
            for i in range(len(chains)):
                kq.append(_dot_nt(kh[i], qh[i]))
            for i in range(len(chains)):
                cq.append(_dot_nt(c_aug[i].astype(BF16), qh[i]))
            mu, st = [], []
            for i, (d, hh) in enumerate(chains):
                mu.append(jnp.maximum(m_st[d][hh:hh + 1, :], jnp.max(bm[i], axis=0, keepdims=True)))
            for i in range(len(chains)):
                st.append((kq[i] * jnp.exp(bm[i] - mu[i])).astype(BF16))
            r = []
            for i, (d, hh) in enumerate(chains):
                r.append(_dot(v_aug[i], st[i]) + cq[i] * jnp.exp(m_st[d][hh:hh + 1, :] - mu[i]))
            hs = []
            for i, (d, hh) in enumerate(chains):
                e_negm = jnp.exp(-(cum[d][hh:hh + 1, :] + mu[i]))
                den = pltpu.roll(r[i], ML_HEAD_DIM, axis=0)
                hs.append(r[i] / jnp.maximum(jnp.abs(den), e_negm))
            for d in range(ns):
                for p in range(2):
                    i = d * ML_HEADS + 2 * p
                    hacc[d][cs[d % 2], LANES * p:LANES * (p + 1), :] = jnp.where(row < ML_HEAD_DIM, hs[i], hs[i + 1])
        dc = []
        for i, (d, hh) in enumerate(chains):
            vw = (v_aug[i].astype(F32) * wk[d][hh:hh + 1, :]).astype(BF16)
            dc.append(_dot(vw, kh[i]))
        for i, (d, hh) in enumerate(chains):
            cst[i][...] = decay[d][hh:hh + 1, :] * c_aug[i] + dc[i]

    def scan(qs, ks, vt, gt, hacc, need_h):
        nc = gt.shape[1]

        def body(j, carry):
            chunk_pair(qs, ks, vt, gt, hacc, (j, nc - 1 - j), need_h)
            return carry

        lax.fori_loop(0, nc, body, 0, unroll=2)

    scan(qs_c, ks_c, vt_c, gt_c, hacc_c, ctx_out)
    scan(qs_l, ks_l, vt_l, gt_l, hacc_l, True)

    def finish(ot, hacc, y):
        def body(c, carry):
            for rr in range(nrows):
                gate = _sigmoid(ot[rr, c].astype(F32))
                y[rr, c] = (gate * (hacc[2 * rr][c] + hacc[2 * rr + 1][c])).astype(y.dtype)
            return carry

        lax.fori_loop(0, ot.shape[1], body, 0)

    finish(ot_l, hacc_l, y_l)
    if ctx_out:
        finish(ot_c, hacc_c, y_c)


def _mlstm(lat, ctx, pw, l, ctx_out):
    b, n, _ = lat[0].shape
    nc = ctx[0].shape[1]
    nr = ML_ROWS if b % ML_ROWS == 0 else 1
    ns = 2 * nr
    whole = lambda a: pl.BlockSpec((nr,) + a.shape[1:], lambda bi: (bi,) + (0,) * (a.ndim - 1))
    ins = list(lat) + list(ctx)
    tshape = lambda m: (b, m // LANES, 256, LANES)
    out_shape = [jax.ShapeDtypeStruct(tshape(n), BF16)]
    out_specs = [pl.BlockSpec((nr,) + tshape(n)[1:], lambda bi: (bi, 0, 0, 0))]
    if ctx_out:
        out_shape.append(jax.ShapeDtypeStruct(tshape(nc), BF16))
        out_specs.append(pl.BlockSpec((nr,) + tshape(nc)[1:], lambda bi: (bi, 0, 0, 0)))
    res = pl.pallas_call(
        functools.partial(_mlstm_kernel, ctx_out=ctx_out),
        grid=(b // nr,),
        in_specs=[whole(a) for a in ins] + [_layer_spec(pw["bgt"], l)],
        out_specs=out_specs,
        out_shape=out_shape,
        scratch_shapes=[pltpu.VMEM((n // LANES, 256, LANES), F32)] * ns
                       + [pltpu.VMEM((nc // LANES, 256, LANES), F32)] * ns
                       + [pltpu.VMEM((LANES, LANES), F32)] * (ns * ML_HEADS)
                       + [pltpu.VMEM((8, LANES), F32)] * ns,
        compiler_params=_cparams(("parallel",)),
        name="mlstm",
    )(*ins, pw["bgt"])
    return (res[0], res[1]) if ctx_out else (res[0], None)


MERGE_GROUPS = 2


def _merge_kernel(x_ref, mod_ref, ya_ref, yb_ref, yc_ref, yd_ref, gpre_ref, gpost_ref, gffn_ref,
                  wg_ref, bgate_ref, wb_ref, wo_ref, x1_ref, h2_ref):
    m = mod_ref[0, 0]
    tm = x_ref.shape[1]
    groups = min(MERGE_GROUPS, tm // LANES)
    th = tm // groups
    rows = [slice(a * th, (a + 1) * th) for a in range(groups)]
    xs = [x_ref[0, r, :] for r in rows]
    hbs = [(_rms(x, gpre_ref[0]) * (1.0 + m[1:2]) + m[0:1]).astype(BF16) for x in xs]
    cpg = th // LANES
    accs = [None] * groups
    for s, y_ref in enumerate((ya_ref, yb_ref, yc_ref, yd_ref)):
        for a, r in enumerate(rows):
            gate = _sigmoid(_dot(hbs[a], wg_ref[0, s]) + bgate_ref[0, s:s + 1, :])
            if s == 1:
                yb_t = jnp.concatenate([yb_ref[0, a * cpg + j] for j in range(cpg)], axis=1)
                branch = _dot_tn(yb_t, wb_ref[0, s])
            else:
                branch = _dot(y_ref[0, r, :], wb_ref[0, s])
            term = gate * branch
            accs[a] = term if accs[a] is None else accs[a] + term
    ys = [_dot(acc.astype(BF16), wo_ref[0]) for acc in accs]
    for a, r in enumerate(rows):
        x1 = xs[a] + m[2:3] * _rms(ys[a], gpost_ref[0])
        x1_ref[0, r, :] = x1
        h2_ref[0, r, :] = (_rms(x1, gffn_ref[0]) * (1.0 + m[4:5]) + m[3:4]).astype(BF16)


def _merge(x, mod, ctx_row, ys, pw, l, tm):
    b, n, d = x.shape
    tm = min(tm, n)
    row = lambda bi, i: (bi, i, 0)
    names = ("g_pre_mix", "g_post_mix", "g_pre_ffn", "wg", "b_gate", "wb", "wo")
    return pl.pallas_call(
        _merge_kernel,
        grid=(b, n // tm),
        in_specs=[pl.BlockSpec((1, tm, d), row), _mod_spec(l, ctx_row)]
                 + [pl.BlockSpec((1, tm, BRANCH_W), row),
                    pl.BlockSpec((1, tm // LANES, BRANCH_W, LANES), lambda bi, i: (bi, i, 0, 0)),
                    pl.BlockSpec((1, tm, BRANCH_W), row), pl.BlockSpec((1, tm, BRANCH_W), row)]
                 + [_layer_spec(pw[k], l) for k in names],
        out_specs=[pl.BlockSpec((1, tm, d), row), pl.BlockSpec((1, tm, d), row)],
        out_shape=[jax.ShapeDtypeStruct((b, n, d), F32), jax.ShapeDtypeStruct((b, n, d), BF16)],
        compiler_params=_cparams(("parallel", "parallel")),
        name="merge",
    )(x, mod, *ys, *[pw[k] for k in names])


FFN_HALO = 16
FFN_COL_CHUNKS = ((0, D_FF),)


def _ffn_kernel(h_ref, hp_ref, hn_ref, x1_ref, mod_ref, gpost_ref, wup_ref, wconv_ref, bconv_ref, wdown_ref,
                o_ref):
    i = pl.program_id(1)
    last = pl.num_programs(1) - 1
    tm = h_ref.shape[1]
    hm = h_ref[0]
    hp = jnp.where(i > 0, hp_ref[0], jnp.zeros_like(hp_ref[0]))
    hn = jnp.where(i < last, hn_ref[0], jnp.zeros_like(hn_ref[0]))
    hext = jnp.concatenate([hp, hm, hn], axis=0)
    ne = tm + 2 * FFN_HALO
    acc = None
    for lo, hi in FFN_COL_CHUNKS:
        a = _dot(hext, wup_ref[0, :, lo:hi])
        ap = pltpu.roll(a, 1, axis=0)[FFN_HALO:FFN_HALO + tm]
        an = pltpu.roll(a, ne - 1, axis=0)[FFN_HALO:FFN_HALO + tm]
        ac = a[FFN_HALO:FFN_HALO + tm]
        a = (ap * wconv_ref[0, 0:1, lo:hi] + ac * wconv_ref[0, 1:2, lo:hi] + an * wconv_ref[0, 2:3, lo:hi]
             + bconv_ref[0, :, lo:hi])
        v = _dot(hm, wup_ref[0, :, D_FF + lo:D_FF + hi])
        act = (a * _sigmoid(a) * v).astype(BF16)
        term = _dot(act, wdown_ref[0, lo:hi, :])
        acc = term if acc is None else acc + term
    m = mod_ref[0, 0]
    o_ref[0] = x1_ref[0] + m[5:6] * _rms(acc, gpost_ref[0])


def _ffn(h2, x1, mod, ctx_row, pw, l, tm):
    b, n, d = x1.shape
    tm = min(tm, n)
    row = lambda bi, i: (bi, i, 0)
    r = tm // FFN_HALO
    nblk = n // FFN_HALO
    names = ("g_post_ffn", "wup", "w_ffn_conv", "b_ffn_conv", "wdown")
    return pl.pallas_call(
        _ffn_kernel,
        grid=(b, n // tm),
        in_specs=[pl.BlockSpec((1, tm, d), row),
                  pl.BlockSpec((1, FFN_HALO, d), lambda bi, i: (bi, jnp.maximum(i * r - 1, 0), 0)),
                  pl.BlockSpec((1, FFN_HALO, d), lambda bi, i: (bi, jnp.minimum((i + 1) * r, nblk - 1), 0)),
                  pl.BlockSpec((1, tm, d), row),
                  _mod_spec(l, ctx_row)]
                 + [_layer_spec(pw[k], l) for k in names],
        out_specs=pl.BlockSpec((1, tm, d), row),
        out_shape=jax.ShapeDtypeStruct((b, n, d), F32),
        compiler_params=_cparams(("parallel", "parallel")),
        name="conv_ffn",
    )(h2, h2, h2, x1, mod, *[pw[k] for k in names])


def _rope_cos_sin(n, d, identity):
    if identity:
        return jnp.ones((n, d), F32), jnp.zeros((n, d), F32)
    half, nf = d // 2, d // 4
    t = jnp.arange(n, dtype=jnp.int32)
    row = (t // GRID_W).astype(F32)[:, None]
    colp = (t % GRID_W).astype(F32)[:, None]
    inv = ROPE_BASE ** (-jnp.arange(nf, dtype=F32) / nf)
    ang = jnp.concatenate([row * inv, row * inv, colp * inv, colp * inv], axis=-1)
    sign = jnp.asarray(np.where(np.arange(d) % half < nf, -1.0, 1.0), F32)
    return jnp.cos(ang), jnp.sin(ang) * sign


def _rope_table(n, identity):
    ca, sa = _rope_cos_sin(n, MLA_ROPE, identity)
    cw, sw = _rope_cos_sin(n, HEAD_DIM, identity)
    scale_a = (MLA_NOPE + MLA_ROPE) ** -0.5 * LOG2E
    scale_w = HEAD_DIM ** -0.5 * LOG2E
    z32 = jnp.zeros((n, LANES - MLA_NOPE - MLA_ROPE), F32)
    one64 = jnp.ones((n, MLA_NOPE), F32)
    zero64 = jnp.zeros((n, MLA_NOPE), F32)
    qa_cos = jnp.concatenate([one64, ca, z32], -1) * scale_a
    qa_sin = jnp.concatenate([zero64, sa, z32], -1) * scale_a
    ka_cos = jnp.concatenate([zero64, ca, z32], -1)
    ka_sin = jnp.concatenate([zero64, sa, z32], -1)
    cw2 = jnp.concatenate([cw, cw], -1)
    sw2 = jnp.concatenate([sw, sw], -1)
    return jnp.concatenate([qa_cos, qa_sin, ka_cos, ka_sin, cw2 * scale_w, sw2 * scale_w, cw2, sw2], axis=-1)


def _dft_tables(n):
    n1 = n // FN_GROUP_W
    k = jnp.arange(n, dtype=jnp.int32)[:, None]
    ang1 = ((k * jnp.arange(n1, dtype=jnp.int32)[None, :]) % n1).astype(F32) * (2.0 * math.pi / n1)
    ang0 = ((k * jnp.arange(FN_GROUP_W, dtype=jnp.int32)[None, :]) % n).astype(F32) * (2.0 * math.pi / n)
    c1, s1 = jnp.cos(ang1)[:, :, None], jnp.sin(ang1)[:, :, None]
    c0, s0 = jnp.cos(ang0)[:, None, :], jnp.sin(ang0)[:, None, :]
    cos_kt = (c1 * c0 - s1 * s0).reshape(n, n)
    sin_kt = (s1 * c0 + c1 * s0).reshape(n, n)
    half = n // 2
    col = jnp.arange(half, dtype=jnp.int32)[None, :]
    dft = jnp.concatenate([cos_kt[:, :half], jnp.where(col == 0, cos_kt[:, half:half + 1], -sin_kt[:, :half])],
                          axis=-1).astype(BF16)
    j = np.arange(FN_GROUP_W)
    a64 = (np.outer(j, j) % FN_GROUP_W) * (2.0 * np.pi / FN_GROUP_W)
    eye = np.eye(FN_GROUPS)
    cs = np.concatenate([np.kron(eye, np.cos(a64)), np.kron(eye, np.sin(a64))], axis=-1)
    return dft, jnp.asarray(cs, F32).astype(BF16)


def _pad_last(a, width):
    return jnp.pad(a, [(0, 0)] * (a.ndim - 1) + [(0, width - a.shape[-1])])


def _prep_weights(p):
    w_in = p["w_in"]
    depth, d, _ = w_in.shape
    vec = lambda a: a.reshape(depth, 1, a.shape[-1])
    off = [int(o) for o in np.concatenate([[0], np.cumsum(IN_SPLITS)])]
    seg = lambda i, j=None: w_in[:, :, off[i]:off[i + 1 if j is None else j]]
    dup = lambda t: jnp.broadcast_to(t.reshape(depth, d, WG_KV_HEADS, 1, HEAD_DIM),
                                     (depth, d, WG_KV_HEADS, 2, HEAD_DIM)).reshape(depth, d, 4 * HEAD_DIM)
    zeros = lambda w: jnp.zeros((depth, d, w), F32)
    w1 = jnp.concatenate([seg(0, 2), zeros(MLA_NOPE), seg(2), zeros(LANES - MLA_NOPE - MLA_ROPE),
                          seg(8), dup(seg(9)), dup(seg(10)), seg(11), seg(3, 5)], axis=-1)
    assert w1.shape[-1] == C_END
    gates = _pad_last(seg(7).reshape(depth, d, 4, ML_HEADS), 8).reshape(depth, d, GT_ROWS)
    w1t = jnp.swapaxes(jnp.concatenate([seg(5, 7), gates], axis=-1), 1, 2)
    assert w1t.shape[1] == R_END
    bgt = _pad_last(p["b_ml_gates"].reshape(depth, 4, ML_HEADS), 8).reshape(depth, GT_ROWS)
    bgt = jnp.broadcast_to(bgt[:, :, None], (depth, GT_ROWS, LANES))
    rq, rkv = p["w_uq"].shape[1], p["w_ukv"].shape[1]
    wq = _pad_last(p["w_uq"].reshape(depth, rq, MLA_HEADS, MLA_NOPE + MLA_ROPE), LANES).reshape(depth, rq, -1)
    kv = p["w_ukv"].reshape(depth, rkv, MLA_HEADS, MLA_NOPE + MLA_V)
    wkv = jnp.concatenate([_pad_last(kv[..., :MLA_NOPE], LANES).reshape(depth, rkv, -1),
                           kv[..., MLA_NOPE:].reshape(depth, rkv, -1)], axis=-1)
    return {
        "w1": w1.astype(BF16),
        "w1t": w1t.astype(BF16),
        "wq": wq.astype(BF16),
        "wkv": wkv.astype(BF16),
        "g_qa": vec(p["g_qa"]), "g_kva": vec(p["g_kva"]),
        "g_pre_mix": vec(p["g_pre_mix"]), "g_post_mix": vec(p["g_post_mix"]),
        "g_pre_ffn": vec(p["g_pre_ffn"]), "g_post_ffn": vec(p["g_post_ffn"]),
        "w_ml_conv": p["w_ml_conv"], "bgt": bgt, "wg_sink": p["wg_sink"],
        "wg": p["w_gate"].astype(BF16), "b_gate": p["b_gate"],
        "wb": p["w_branch"].astype(BF16), "wo": p["w_out"].astype(BF16),
        "wup": p["w_up"].astype(BF16), "w_ffn_conv": p["w_ffn_conv"],
        "b_ffn_conv": vec(p["b_ffn_conv"]), "wdown": p["w_down"].astype(BF16),
    }


TM = 512


def kernel(x, c, ctx, c_ctx, w_mod, b_mod, g_pre_mix, g_post_mix, g_pre_ffn, g_post_ffn, w_in, g_qa, w_uq, g_kva,
           w_ukv, w_ml_conv, b_ml_gates, wg_sink, w_gate, b_gate, w_branch, w_out, w_up, w_ffn_conv, b_ffn_conv,
           w_down):
    p = dict(g_pre_mix=g_pre_mix, g_post_mix=g_post_mix, g_pre_ffn=g_pre_ffn, g_post_ffn=g_post_ffn, w_in=w_in,
             g_qa=g_qa, w_uq=w_uq, g_kva=g_kva, w_ukv=w_ukv, w_ml_conv=w_ml_conv, b_ml_gates=b_ml_gates,
             wg_sink=wg_sink, w_gate=w_gate, b_gate=b_gate, w_branch=w_branch, w_out=w_out, w_up=w_up,
             w_ffn_conv=w_ffn_conv, b_ffn_conv=b_ffn_conv, w_down=w_down)
    bsz, n, d = x.shape
    n_ctx = ctx.shape[1]
    depth = w_mod.shape[0]
    rows = -(-(bsz + 1) // 8) * 8
    c_all = jnp.concatenate([c, c_ctx[None, :], jnp.zeros((rows - bsz - 1, d), F32)], axis=0)
    mod = _modulation(c_all, w_mod, b_mod).reshape(depth, rows, 6, d)
    pw = _prep_weights(p)
    tab_l = _rope_table(n, identity=False)
    tab_c = _rope_table(n_ctx, identity=True)
    dft_l, cs64 = _dft_tables(n)
    dft_c, _ = _dft_tables(n_ctx)
    xc = ctx
    for l in range(depth):
        ctx_out = l < depth - 1
        qa, ka, va, uq, uk, vt, ot, gt, qw, kw, vw, uf = _inproj(x, mod, None, pw, l, tab_l, TM)
        qa_c, ka_c, va_c, uq_c, uk_c, vt_c, ot_c, gt_c, qw_c, kw_c, vw_c, uf_c = _inproj(
            xc, mod, bsz, pw, l, tab_c, TM)
        ya = _mla_attend(qa, ka_c, va_c, ka, va)
        yb, yb_c = _mlstm((uq, uk, vt, ot, gt), (uq_c, uk_c, vt_c, ot_c, gt_c), pw, l, ctx_out)
        yc = _win_attend(pw["wg_sink"], l, qw, kw_c, vw_c, kw, vw)
        yd = _fourier(uf, cs64, dft_l)
        x1, h2 = _merge(x, mod, None, (ya, yb, yc, yd), pw, l, TM)
        x = _ffn(h2, x1, mod, None, pw, l, TM)
        if ctx_out:
            ya_c = _mla_attend(qa_c, ka_c, va_c)
            yc_c = _win_attend(pw["wg_sink"], l, qw_c, kw_c, vw_c)
            yd_c = _fourier(uf_c, cs64, dft_c)
            xc1, hc2 = _merge(xc, mod, bsz, (ya_c, yb_c, yc_c, yd_c), pw, l, TM)
            xc = _ffn(hc2, xc1, mod, bsz, pw, l, TM)
    return x
```

```python
import functools
import math

import numpy as np
import jax
import jax.numpy as jnp
from jax import lax
from jax.experimental import pallas as pl
from jax.experimental.pallas import tpu as pltpu

F32 = jnp.float32
BF16 = jnp.bfloat16

D_MODEL = 1024
GRID_W = 64
N_BRANCH = 4
BRANCH_W = 256
MLA_HEADS = 4
MLA_NOPE = 64
MLA_ROPE = 32
MLA_V = 64
MLA_Q_RANK = 256
MLA_KV_RANK = 256
ML_HEADS = 4
ML_HEAD_DIM = 64
ML_CHUNK = 128
WG_HEADS = 4
WG_KV_HEADS = 2
HEAD_DIM = 64
WINDOW = 128
FN_GROUPS = 4
FN_GROUP_W = 64
D_FF = 2816
ROPE_BASE = 10000.0
EPS = 1e-6
IN_SPLITS = (MLA_Q_RANK, MLA_KV_RANK, MLA_ROPE, 256, 256, 256, 256, 16, 256, 128, 128, 256)

LANES = 128
NEG = -1e30
LOG2E = 1.4426950408889634
VMEM_LIMIT = 56 * 1024 * 1024

C_CQ, C_CKV, C_KR, C_WQ, C_WK, C_WV, C_UF, C_UQ, C_UK, C_END = (
    0, 256, 512, 640, 896, 1152, 1408, 1664, 1920, 2176)
IN_HALO = 16
R_UV, R_UO, R_GT, R_END = 0, 256, 512, 544
GT_ROWS = 32
T_QA_COS, T_QA_SIN, T_KA_COS, T_KA_SIN, T_QW_COS, T_QW_SIN, T_KW_COS, T_KW_SIN = range(8)


def _cparams(sem):
    return pltpu.CompilerParams(dimension_semantics=sem, vmem_limit_bytes=VMEM_LIMIT)


def _layer_spec(arr, l):
    nd = arr.ndim - 1
    return pl.BlockSpec((1,) + arr.shape[1:], lambda *_: (l,) + (0,) * nd, pipeline_mode=pl.Buffered(1))


def _const_spec(shape):
    nd = len(shape)
    return pl.BlockSpec(shape, lambda *_: (0,) * nd, pipeline_mode=pl.Buffered(1))


def _mod_spec(l, ctx_row):
    if ctx_row is None:
        return pl.BlockSpec((1, 1, 6, D_MODEL), lambda bi, *_: (l, bi, 0, 0))
    return pl.BlockSpec((1, 1, 6, D_MODEL), lambda *_: (l, ctx_row, 0, 0))


def _rms(x, g):
    ms = jnp.mean(x * x, axis=-1, keepdims=True)
    return x * lax.rsqrt(ms + EPS) * g


def _dot(a, b):
    return jnp.dot(a, b, preferred_element_type=F32)


def _dot_nt(a, b):
    return lax.dot_general(a, b, (((1,), (1,)), ((), ())), preferred_element_type=F32)


def _dot_tn(a, b):
    return lax.dot_general(a, b, (((0,), (0,)), ((), ())), preferred_element_type=F32)


def _sigmoid(x):
    return 1.0 / (1.0 + jnp.exp(-x))


def _mod_kernel(c_ref, w_ref, b_ref, o_ref):
    c = c_ref[...]
    s = (c * _sigmoid(c)).astype(BF16)
    o_ref[0] = _dot(s, w_ref[0].astype(BF16)) + b_ref[0]


def _modulation(c_all, w_mod, b_mod):
    depth, d, n = w_mod.shape
    rows = c_all.shape[0]
    tn = 1536
    return pl.pallas_call(
        _mod_kernel,
        grid=(depth, n // tn),
        in_specs=[pl.BlockSpec((rows, d), lambda l, j: (0, 0)),
                  pl.BlockSpec((1, d, tn), lambda l, j: (l, 0, j)),
                  pl.BlockSpec((1, 1, tn), lambda l, j: (l, 0, j))],
        out_specs=pl.BlockSpec((1, rows, tn), lambda l, j: (l, 0, j)),
        out_shape=jax.ShapeDtypeStruct((depth, rows, n), F32),
        compiler_params=_cparams(("parallel", "parallel")),
        name="modulation",
    )(c_all, w_mod, b_mod.reshape(depth, 1, n))


def _inproj_kernel(x_ref, xp_ref, xn_ref, mod_ref, gpre_ref, w1_ref, w1t_ref, gqa_ref, wq_ref, gkva_ref, wkv_ref,
                   wconv_ref, tab_ref,
                   qa_ref, ka_ref, va_ref, qs_ref, ks_ref, vt_ref, ot_ref, gt_ref,
                   qw_ref, kw_ref, vw_ref, uf_ref):
    i = pl.program_id(1)
    last = pl.num_programs(1) - 1
    tm = x_ref.shape[1]
    m = mod_ref[0, 0]

    def modulated(xv):
        return (_rms(xv, gpre_ref[0]) * (1.0 + m[1:2]) + m[0:1]).astype(BF16)

    def tab(j):
        return tab_ref[:, LANES * j:LANES * (j + 1)]

    lane = lax.broadcasted_iota(jnp.int32, (tm, LANES), 1)

    def rope(xv, cos_slot, sin_slot, nf):
        first = (lane & (2 * nf - 1)) < nf
        blocks = []
        for j in range(xv.shape[1] // LANES):
            xb = xv[:, LANES * j:LANES * (j + 1)]
            partner = jnp.where(first, pltpu.roll(xb, LANES - nf, axis=1), pltpu.roll(xb, nf, axis=1))
            blocks.append(xb * tab(cos_slot) + partner * tab(sin_slot))
        return blocks[0] if len(blocks) == 1 else jnp.concatenate(blocks, axis=-1)

    hb = modulated(x_ref[0])
    u = _dot(hb, w1_ref[0, :, 0:C_UQ])

    def proj(lo, hi):
        return u[:, lo:hi]

    def proj_t(lo, hi, out_ref):
        r = _dot_nt(w1t_ref[0, lo:hi, :], hb)
        for j in range(out_ref.shape[1]):
            out_ref[0, j] = r[:, LANES * j:LANES * (j + 1)].astype(out_ref.dtype)

    hp = modulated(xp_ref[0])
    hn = modulated(xn_ref[0])
    hext = jnp.concatenate([jnp.where(i > 0, hp, jnp.zeros_like(hp)), hb,
                            jnp.where(i < last, hn, jnp.zeros_like(hn))], axis=0)
    uqk = _dot(hext, w1_ref[0, :, C_UQ:C_END])
    ne = tm + 2 * IN_HALO
    wc = wconv_ref[0]
    a = (pltpu.roll(uqk, 1, axis=0)[IN_HALO:IN_HALO + tm] * wc[0:1, :] + uqk[IN_HALO:IN_HALO + tm] * wc[1:2, :]
         + pltpu.roll(uqk, ne - 1, axis=0)[IN_HALO:IN_HALO + tm] * wc[2:3, :])
    a = a * _sigmoid(a)
    qs_ref[0] = a[:, :256].astype(BF16)
    ks_ref[0] = (a[:, 256:] * ML_HEAD_DIM ** -0.5).astype(BF16)
    cq = _rms(proj(C_CQ, C_CKV), gqa_ref[0]).astype(BF16)
    qa_ref[0] = rope(_dot(cq, wq_ref[0]), T_QA_COS, T_QA_SIN, MLA_ROPE // 4).astype(BF16)
    ckv = _rms(proj(C_CKV, C_KR), gkva_ref[0]).astype(BF16)
    kv = _dot(ckv, wkv_ref[0])
    kr = rope(proj(C_KR, C_WQ), T_KA_COS, T_KA_SIN, MLA_ROPE // 4)
    ka_ref[0] = (kv[:, :512] + jnp.concatenate([kr] * MLA_HEADS, axis=-1)).astype(BF16)
    va_ref[0] = kv[:, 512:].astype(BF16)
    proj_t(R_UV, R_UO, vt_ref)
    proj_t(R_UO, R_GT, ot_ref)
    proj_t(R_GT, R_END, gt_ref)
    qw_ref[0] = rope(proj(C_WQ, C_WK), T_QW_COS, T_QW_SIN, HEAD_DIM // 4).astype(BF16)
    kw_ref[0] = rope(proj(C_WK, C_WV), T_KW_COS, T_KW_SIN, HEAD_DIM // 4).astype(BF16)
    vw_ref[0] = proj(C_WV, C_UF).astype(BF16)
    uf_ref[0] = proj(C_UF, C_UQ).astype(BF16)


def _inproj(x, mod, ctx_row, pw, l, tab, tm):
    b, n, d = x.shape
    tm = min(tm, n)
    row = lambda bi, i: (bi, i, 0)
    cpt = tm // LANES

    def out(w, dt):
        return jax.ShapeDtypeStruct((b, n, w), dt), pl.BlockSpec((1, tm, w), row)

    def out_t(w, dt):
        return (jax.ShapeDtypeStruct((b, n // LANES, w, LANES), dt),
                pl.BlockSpec((1, cpt, w, LANES), lambda bi, i: (bi, i, 0, 0)))

    outs = [out(512, BF16), out(512, BF16), out(256, BF16),
            out(256, BF16), out(256, BF16), out_t(256, BF16), out_t(256, BF16), out_t(GT_ROWS, F32),
            out(256, BF16), out(256, BF16), out(256, BF16), out(256, BF16)]
    names = ("g_pre_mix", "w1", "w1t", "g_qa", "wq", "g_kva", "wkv", "w_ml_conv")
    r = tm // IN_HALO
    nblk = n // IN_HALO
    return pl.pallas_call(
        _inproj_kernel,
        grid=(b, n // tm),
        in_specs=[pl.BlockSpec((1, tm, d), row),
                  pl.BlockSpec((1, IN_HALO, d), lambda bi, i: (bi, jnp.maximum(i * r - 1, 0), 0)),
                  pl.BlockSpec((1, IN_HALO, d), lambda bi, i: (bi, jnp.minimum((i + 1) * r, nblk - 1), 0)),
                  _mod_spec(l, ctx_row)]
                 + [_layer_spec(pw[k], l) for k in names]
                 + [pl.BlockSpec((tm, 8 * LANES), lambda bi, i: (i, 0))],
        out_specs=[o[1] for o in outs],
        out_shape=[o[0] for o in outs],
        compiler_params=_cparams(("parallel", "parallel")),
        name="inproj",
    )(x, x, x, mod, *[pw[k] for k in names], tab)


def _mla_kernel(*refs, has_lat):
    if has_lat:
        q_ref, kc_ref, vc_ref, kl_ref, vl_ref, o_ref, kcat, vaug = refs
    else:
        q_ref, kc_ref, vc_ref, o_ref, kcat, vaug = refs
    nc = kc_ref.shape[1]
    nk = kcat.shape[0]

    @pl.when(pl.program_id(1) == 0)
    def _():
        kcat[0:nc, :] = kc_ref[0]
        if has_lat:
            kcat[nc:nk, :] = kl_ref[0]
        for hh in range(MLA_HEADS):
            vsl = slice(LANES * (hh // 2), LANES * (hh // 2 + 1))

            def keep(v):
                lane = lax.broadcasted_iota(jnp.int32, v.shape, 1)
                own = (lane < MLA_V) if hh % 2 == 0 else (lane >= MLA_V)
                return jnp.where(own, v, jnp.zeros_like(v))

            vaug[hh, 0:nc, 0:LANES] = keep(vc_ref[0, :, vsl])
            if has_lat:
                vaug[hh, nc:nk, 0:LANES] = keep(vl_ref[0, :, vsl])
            vaug[hh, :, LANES:2 * LANES] = jnp.ones((nk, LANES), BF16)

    def logits(hh):
        hsl = slice(LANES * hh, LANES * (hh + 1))
        return _dot_nt(q_ref[0, :, hsl], kcat[:, hsl])

    def attend(hh, s):
        mx = jnp.max(s, axis=-1, keepdims=True)
        o2 = _dot(jnp.exp2(s - mx).astype(BF16), vaug[hh])
        return o2[:, :LANES] / o2[:, LANES:]

    outs = []
    s_next = logits(0)
    for hh in range(MLA_HEADS):
        s_cur = s_next
        if hh + 1 < MLA_HEADS:
            s_next = logits(hh + 1)
        outs.append(attend(hh, s_cur))
    o_ref[0] = jnp.concatenate([outs[0] + outs[1], outs[2] + outs[3]], axis=-1).astype(o_ref.dtype)


def _mla_attend(q, kc, vc, kl=None, vl=None, tq=512):
    b, n, _ = q.shape
    tq = min(tq, n)
    has_lat = kl is not None
    nk = kc.shape[1] + (kl.shape[1] if has_lat else 0)
    whole = lambda a: pl.BlockSpec((1,) + a.shape[1:], lambda bi, i: (bi, 0, 0))
    ins = [q, kc, vc] + ([kl, vl] if has_lat else [])
    specs = [pl.BlockSpec((1, tq, 512), lambda bi, i: (bi, i, 0))] + [whole(a) for a in ins[1:]]
    return pl.pallas_call(
        functools.partial(_mla_kernel, has_lat=has_lat),
        grid=(b, n // tq),
        in_specs=specs,
        out_specs=pl.BlockSpec((1, tq, 256), lambda bi, i: (bi, i, 0)),
        out_shape=jax.ShapeDtypeStruct((b, n, 256), BF16),
        scratch_shapes=[pltpu.VMEM((nk, 512), BF16), pltpu.VMEM((MLA_HEADS, nk, 2 * LANES), BF16)],
        compiler_params=_cparams(("parallel", "arbitrary")),
        name="mla_attend",
    )(*ins)


def _win_kernel(*refs, has_lat, n_lat, layer):
    if has_lat:
        sink_ref, q_ref, kc_ref, vc_ref, kl_ref, vl_ref, band_ref, o_ref = refs
    else:
        sink_ref, q_ref, kc_ref, vc_ref, o_ref = refs
    tq = q_ref.shape[1]
    wk = tq + 2 * WINDOW
    lane = lax.broadcasted_iota(jnp.int32, (tq, LANES), 1)
    if has_lat:
        t0 = pl.program_id(1) * tq
        start = pl.multiple_of(jnp.clip(t0 - WINDOW, 0, n_lat - wk), LANES)
        band = band_ref[0]

    def aug(v, e):
        vlane = lax.broadcasted_iota(jnp.int32, v.shape, 1)
        own = (vlane < HEAD_DIM) if e == 0 else (vlane >= HEAD_DIM)
        return jnp.concatenate([jnp.where(own, v, jnp.zeros_like(v)), jnp.ones(v.shape, BF16)], axis=1)

    heads = [(pr, e) for pr in range(WG_KV_HEADS) for e in range(2)]
    psl = [slice(LANES * pr, LANES * (pr + 1)) for pr in range(WG_KV_HEADS)]
    sink = [sink_ref[layer, hq] * LOG2E for hq in range(WG_HEADS)]
    s_c, s_l, mx = [], [], []
    for pr, e in heads:
        qp = q_ref[0, :, psl[pr]]
        own = (lane < HEAD_DIM) if e == 0 else (lane >= HEAD_DIM)
        q = jnp.where(own, qp, jnp.zeros_like(qp))
        s_c.append(_dot_nt(q, kc_ref[0, :, psl[pr]]))
        if has_lat:
            s_l.append(_dot_nt(q, kl_ref[0, pl.ds(start, wk), psl[pr]]) + band)
    for i in range(WG_HEADS):
        m = jnp.maximum(jnp.max(s_c[i], axis=-1, keepdims=True), sink[i])
        if has_lat:
            m = jnp.maximum(m, jnp.max(s_l[i], axis=-1, keepdims=True))
        mx.append(m)
    o2 = []
    for i, (pr, e) in enumerate(heads):
        o = _dot(jnp.exp2(s_c[i] - mx[i]).astype(BF16), aug(vc_ref[0, :, psl[pr]], e))
        if has_lat:
            o = o + _dot(jnp.exp2(s_l[i] - mx[i]).astype(BF16), aug(vl_ref[0, pl.ds(start, wk), psl[pr]], e))
        o2.append(o)
    outs = [o2[i][:, :LANES] / (o2[i][:, LANES:] + jnp.exp2(sink[i] - mx[i])) for i in range(WG_HEADS)]
    o_ref[0] = jnp.concatenate([outs[0] + outs[1], outs[2] + outs[3]], axis=-1).astype(o_ref.dtype)


def _win_attend(sink, layer, q, kc, vc, kl=None, vl=None, tq=256):
    b, n, _ = q.shape
    tq = min(tq, n)
    has_lat = kl is not None
    whole = lambda a: pl.BlockSpec((1,) + a.shape[1:], lambda bi, i: (bi, 0, 0))
    ins = [q, kc, vc] + ([kl, vl] if has_lat else [])
    specs = ([pl.BlockSpec(memory_space=pltpu.SMEM),
              pl.BlockSpec((1, tq, 256), lambda bi, i: (bi, i, 0))] + [whole(a) for a in ins[1:]])
    if has_lat:
        nt = n // tq
        wk = tq + 2 * WINDOW
        assert nt >= 2 and wk <= n
        rel = np.arange(wk)[None, :] - np.arange(tq)[:, None]
        shift = (0, WINDOW, 2 * WINDOW)
        band = np.stack([np.where(np.abs(rel - sh) <= WINDOW, 0.0, NEG) for sh in shift]).astype(np.float32)
        ins.append(jnp.asarray(band))
        specs.append(pl.BlockSpec((1, tq, wk), lambda bi, i: (jnp.where(i == 0, 0, jnp.where(i == nt - 1, 2, 1)), 0, 0)))
    return pl.pallas_call(
        functools.partial(_win_kernel, has_lat=has_lat, n_lat=n, layer=layer),
        grid=(b, n // tq),
        in_specs=specs,
        out_specs=pl.BlockSpec((1, tq, 256), lambda bi, i: (bi, i, 0)),
        out_shape=jax.ShapeDtypeStruct((b, n, 256), BF16),
        compiler_params=_cparams(("parallel", "parallel")),
        name="win_attend",
    )(sink, *ins)


def _fourier_kernel(u_ref, rev_ref, cs_ref, dft_ref, o_ref, ab_ref, *, tr):
    n = u_ref.shape[1]
    half = n // 2
    u_rev = _dot(rev_ref[...], u_ref[0, half:n, :]).astype(BF16)
    ab_a = _dot(u_ref[0, 0:half, :], cs_ref[...])
    ab_r = _dot(u_rev, cs_ref[...])
    a_mid = _dot(u_ref[0, half:half + 16, :], cs_ref[:, 0:256])[0:1, :]
    first = lax.broadcasted_iota(jnp.int32, (half, 256), 0) == 0
    ab_ref[0:half, :] = (ab_a[:, :256] + ab_r[:, :256]).astype(BF16)
    ab_ref[half:n, :] = jnp.where(first, a_mid, ab_a[:, 256:] - ab_r[:, 256:]).astype(BF16)
    scale = 1.0 / math.sqrt(n * FN_GROUP_W)

    def body(r, carry):
        r0 = pl.multiple_of(r * tr, tr)
        y = _dot(dft_ref[pl.ds(r0, tr), :], ab_ref[...])
        o_ref[0, pl.ds(r0, tr), :] = (y * scale).astype(o_ref.dtype)
        return carry

    lax.fori_loop(0, n // tr, body, 0)


def _fourier(u, cs, dft):
    b, n, w = u.shape
    tr = min(512, n)
    half = n // 2
    idx = jnp.arange(half, dtype=jnp.int32)
    rev = (idx[:, None] + idx[None, :] == half).astype(BF16)
    return pl.pallas_call(
        functools.partial(_fourier_kernel, tr=tr),
        grid=(b,),
        in_specs=[pl.BlockSpec((1, n, w), lambda bi: (bi, 0, 0)),
                  _const_spec((half, half)),
                  _const_spec(cs.shape),
                  _const_spec(dft.shape)],
        out_specs=pl.BlockSpec((1, n, w), lambda bi: (bi, 0, 0)),
        out_shape=jax.ShapeDtypeStruct((b, n, w), BF16),
        scratch_shapes=[pltpu.VMEM((n, w), BF16)],
        compiler_params=_cparams(("parallel",)),
        name="fourier",
    )(u, rev, cs, dft)


ML_ROWS = 2


def _split3(x):
    hi = x.astype(BF16).astype(F32)
    r1 = x - hi
    mid = r1.astype(BF16).astype(F32)
    lo = (r1 - mid).astype(BF16).astype(F32)
    return hi, mid, lo


def _mlstm_kernel(*refs, ctx_out):
    (qs_l, ks_l, vt_l, ot_l, gt_l, qs_c, ks_c, vt_c, ot_c, gt_c, bgt_ref) = refs[:11]
    if ctx_out:
        y_l, y_c = refs[11:13]
        scratch = refs[13:]
    else:
        y_l = refs[11]
        y_c = None
        scratch = refs[12:]
    nrows = qs_l.shape[0]
    ns = 2 * nrows
    hacc_l, hacc_c = scratch[0:ns], scratch[ns:2 * ns]
    cst = scratch[2 * ns:2 * ns + ns * ML_HEADS]
    mst = scratch[2 * ns + ns * ML_HEADS:]
    L = ML_CHUNK
    row = lax.broadcasted_iota(jnp.int32, (L, L), 0)
    col = lax.broadcasted_iota(jnp.int32, (L, L), 1)
    triu_b = (row <= col).astype(BF16)
    krow = lax.broadcasted_iota(jnp.int32, (32, L), 0)
    zeros8 = jnp.zeros((8, L), F32)

    for ref in list(cst) + list(mst):
        ref[...] = jnp.zeros_like(ref)

    chains = [(si, hh) for si in range(ns) for hh in range(ML_HEADS)]
    own_l = [(col < ML_HEAD_DIM) if hh % 2 == 0 else (col >= ML_HEAD_DIM) for hh in range(ML_HEADS)]
    own_r = [(row < ML_HEAD_DIM) if hh % 2 == 0 else (row >= ML_HEAD_DIM) for hh in range(ML_HEADS)]
    mask = [row <= col, row >= col]
    sel = [jnp.where((krow & 7) == hh, 1.0, 0.0).astype(BF16) for hh in range(ML_HEADS)]

    def chunk_pair(qs, ks, vt, gt, hacc, cs, need_h):
        r0 = [pl.multiple_of(c * L, L) for c in cs]
        cum, tot, b, m_st, wk, decay, lhs = [], [], [], [], [], [], []
        for d in range(ns):
            rr, dd = d // 2, d % 2
            g = gt[rr, cs[dd]] + bgt_ref[0]
            li = g[16 * dd:16 * dd + 8]
            gf = g[16 * dd + 8:16 * dd + 16]
            ls = jnp.minimum(gf, 0.0) - jnp.log(1.0 + jnp.exp(-jnp.abs(gf)))
            cu = sum(_dot(piece.astype(BF16), triu_b) for piece in _split3(ls))
            tt = cu[:, L - 1:L]
            if dd == 1:
                cu = tt - cu + ls
            bb = li - cu
            ms = mst[d][...]
            lw = tt + bb
            m_new = jnp.maximum(tt + ms, jnp.max(lw, axis=1, keepdims=True))
            wk.append(jnp.exp(lw - m_new))
            decay.append(jnp.exp(tt + ms - m_new))
            mst[d][...] = m_new
            cum.append(cu)
            tot.append(tt)
            b.append(bb)
            m_st.append(ms)
            if need_h:
                lhs.append(jnp.concatenate(list(_split3(bb)) + [zeros8], axis=0).astype(BF16))
        kh, v_aug, c_aug, qh, kq, cq, bm = [], [], [], [], [], [], []
        for d, hh in chains:
            rr, dd = d // 2, d % 2
            psl = slice(LANES * (hh // 2), LANES * (hh // 2 + 1))
            kh.append(ks[rr, pl.ds(r0[dd], L), psl])
            vb = vt[rr, cs[dd], psl, :]
            v_aug.append(jnp.where(own_r[hh], vb, jnp.ones_like(vb)))
            c_aug.append(cst[d * ML_HEADS + hh][...])
            if need_h:
                qb = qs[rr, pl.ds(r0[dd], L), psl]
                qh.append(jnp.where(own_l[hh], qb, jnp.zeros_like(qb)))
        if need_h:
            for i, (d, hh) in enumerate(chains):
                bm.append(jnp.where(mask[d % 2], _dot_tn(lhs[d], sel[hh]), NEG))
            for i in range(len(chains)):
                kq.append(_dot_nt(kh[i], qh[i]))
            for i in range(len(chains)):
                cq.append(_dot_nt(c_aug[i].astype(BF16), qh[i]))
            mu, st = [], []
            for i, (d, hh) in enumerate(chains):
                mu.append(jnp.maximum(m_st[d][hh:hh + 1, :], jnp.max(bm[i], axis=0, keepdims=True)))
            for i in range(len(chains)):
                st.append((kq[i] * jnp.exp(bm[i] - mu[i])).astype(BF16))
            r = []
            for i, (d, hh) in enumerate(chains):
                r.append(_dot(v_aug[i], st[i]) + cq[i] * jnp.exp(m_st[d][hh:hh + 1, :] - mu[i]))
            hs = []
            for i, (d, hh) in enumerate(chains):
                e_negm = jnp.exp(-(cum[d][hh:hh + 1, :] + mu[i]))
                den = pltpu.roll(r[i], ML_HEAD_DIM, axis=0)
                hs.append(r[i] / jnp.maximum(jnp.abs(den), e_negm))
            for d in range(ns):
                for p in range(2):
                    i = d * ML_HEADS + 2 * p
                    hacc[d][cs[d % 2], LANES * p:LANES * (p + 1), :] = jnp.where(row < ML_HEAD_DIM, hs[i], hs[i + 1])
        dc = []
        for i, (d, hh) in enumerate(chains):
            vw = (v_aug[i].astype(F32) * wk[d][hh:hh + 1, :]).astype(BF16)
            dc.append(_dot(vw, kh[i]))
        for i, (d, hh) in enumerate(chains):
            cst[i][...] = decay[d][hh:hh + 1, :] * c_aug[i] + dc[i]

    def scan(qs, ks, vt, gt, hacc, need_h):
        nc = gt.shape[1]

        def body(j, carry):
            chunk_pair(qs, ks, vt, gt, hacc, (j, nc - 1 - j), need_h)
            return carry

        lax.fori_loop(0, nc, body, 0, unroll=4)

    scan(qs_c, ks_c, vt_c, gt_c, hacc_c, ctx_out)
    scan(qs_l, ks_l, vt_l, gt_l, hacc_l, True)

    def finish(ot, hacc, y):
        def body(c, carry):
            for rr in range(nrows):
                gate = _sigmoid(ot[rr, c].astype(F32))
                y[rr, c] = (gate * (hacc[2 * rr][c] + hacc[2 * rr + 1][c])).astype(y.dtype)
            return carry

        lax.fori_loop(0, ot.shape[1], body, 0)

    finish(ot_l, hacc_l, y_l)
    if ctx_out:
        finish(ot_c, hacc_c, y_c)


def _mlstm(lat, ctx, pw, l, ctx_out):
    b, n, _ = lat[0].shape
    nc = ctx[0].shape[1]
    nr = ML_ROWS if b % ML_ROWS == 0 else 1
    ns = 2 * nr
    whole = lambda a: pl.BlockSpec((nr,) + a.shape[1:], lambda bi: (bi,) + (0,) * (a.ndim - 1))
    ins = list(lat) + list(ctx)
    tshape = lambda m: (b, m // LANES, 256, LANES)
    out_shape = [jax.ShapeDtypeStruct(tshape(n), BF16)]
    out_specs = [pl.BlockSpec((nr,) + tshape(n)[1:], lambda bi: (bi, 0, 0, 0))]
    if ctx_out:
        out_shape.append(jax.ShapeDtypeStruct(tshape(nc), BF16))
        out_specs.append(pl.BlockSpec((nr,) + tshape(nc)[1:], lambda bi: (bi, 0, 0, 0)))
    res = pl.pallas_call(
        functools.partial(_mlstm_kernel, ctx_out=ctx_out),
        grid=(b // nr,),
        in_specs=[whole(a) for a in ins] + [_layer_spec(pw["bgt"], l)],
        out_specs=out_specs,
        out_shape=out_shape,
        scratch_shapes=[pltpu.VMEM((n // LANES, 256, LANES), F32)] * ns
                       + [pltpu.VMEM((nc // LANES, 256, LANES), F32)] * ns
                       + [pltpu.VMEM((LANES, LANES), F32)] * (ns * ML_HEADS)
                       + [pltpu.VMEM((8, LANES), F32)] * ns,
        compiler_params=_cparams(("parallel",)),
        name="mlstm",
    )(*ins, pw["bgt"])
    return (res[0], res[1]) if ctx_out else (res[0], None)


MERGE_GROUPS = 2


def _merge_kernel(x_ref, mod_ref, ya_ref, yb_ref, yc_ref, yd_ref, gpre_ref, gpost_ref, gffn_ref,
                  wg_ref, bgate_ref, wb_ref, wo_ref, x1_ref, h2_ref):
    m = mod_ref[0, 0]
    tm = x_ref.shape[1]
    groups = min(MERGE_GROUPS, tm // LANES)
    th = tm // groups
    rows = [slice(a * th, (a + 1) * th) for a in range(groups)]
    xs = [x_ref[0, r, :] for r in rows]
    hbs = [(_rms(x, gpre_ref[0]) * (1.0 + m[1:2]) + m[0:1]).astype(BF16) for x in xs]
    cpg = th // LANES
    accs = [None] * groups
    for s, y_ref in enumerate((ya_ref, yb_ref, yc_ref, yd_ref)):
        for a, r in enumerate(rows):
            gate = _sigmoid(_dot(hbs[a], wg_ref[0, s]) + bgate_ref[0, s:s + 1, :])
            if s == 1:
                yb_t = jnp.concatenate([yb_ref[0, a * cpg + j] for j in range(cpg)], axis=1)
                branch = _dot_tn(yb_t, wb_ref[0, s])
            else:
                branch = _dot(y_ref[0, r, :], wb_ref[0, s])
            term = gate * branch
            accs[a] = term if accs[a] is None else accs[a] + term
    ys = [_dot(acc.astype(BF16), wo_ref[0]) for acc in accs]
    for a, r in enumerate(rows):
        x1 = xs[a] + m[2:3] * _rms(ys[a], gpost_ref[0])
        x1_ref[0, r, :] = x1
        h2_ref[0, r, :] = (_rms(x1, gffn_ref[0]) * (1.0 + m[4:5]) + m[3:4]).astype(BF16)


def _merge(x, mod, ctx_row, ys, pw, l, tm):
    b, n, d = x.shape
    tm = min(tm, n)
    row = lambda bi, i: (bi, i, 0)
    names = ("g_pre_mix", "g_post_mix", "g_pre_ffn", "wg", "b_gate", "wb", "wo")
    return pl.pallas_call(
        _merge_kernel,
        grid=(b, n // tm),
        in_specs=[pl.BlockSpec((1, tm, d), row), _mod_spec(l, ctx_row)]
                 + [pl.BlockSpec((1, tm, BRANCH_W), row),
                    pl.BlockSpec((1, tm // LANES, BRANCH_W, LANES), lambda bi, i: (bi, i, 0, 0)),
                    pl.BlockSpec((1, tm, BRANCH_W), row), pl.BlockSpec((1, tm, BRANCH_W), row)]
                 + [_layer_spec(pw[k], l) for k in names],
        out_specs=[pl.BlockSpec((1, tm, d), row), pl.BlockSpec((1, tm, d), row)],
        out_shape=[jax.ShapeDtypeStruct((b, n, d), F32), jax.ShapeDtypeStruct((b, n, d), BF16)],
        compiler_params=_cparams(("parallel", "parallel")),
        name="merge",
    )(x, mod, *ys, *[pw[k] for k in names])


FFN_HALO = 16
FFN_COL_CHUNKS = ((0, D_FF),)


def _ffn_kernel(h_ref, hp_ref, hn_ref, x1_ref, mod_ref, gpost_ref, wup_ref, wconv_ref, bconv_ref, wdown_ref,
                o_ref):
    i = pl.program_id(1)
    last = pl.num_programs(1) - 1
    tm = h_ref.shape[1]
    hm = h_ref[0]
    hp = jnp.where(i > 0, hp_ref[0], jnp.zeros_like(hp_ref[0]))
    hn = jnp.where(i < last, hn_ref[0], jnp.zeros_like(hn_ref[0]))
    hext = jnp.concatenate([hp, hm, hn], axis=0)
    ne = tm + 2 * FFN_HALO
    acc = None
    for lo, hi in FFN_COL_CHUNKS:
        a = _dot(hext, wup_ref[0, :, lo:hi])
        ap = pltpu.roll(a, 1, axis=0)[FFN_HALO:FFN_HALO + tm]
        an = pltpu.roll(a, ne - 1, axis=0)[FFN_HALO:FFN_HALO + tm]
        ac = a[FFN_HALO:FFN_HALO + tm]
        a = (ap * wconv_ref[0, 0:1, lo:hi] + ac * wconv_ref[0, 1:2, lo:hi] + an * wconv_ref[0, 2:3, lo:hi]
             + bconv_ref[0, :, lo:hi])
        v = _dot(hm, wup_ref[0, :, D_FF + lo:D_FF + hi])
        act = (a * _sigmoid(a) * v).astype(BF16)
        term = _dot(act, wdown_ref[0, lo:hi, :])
        acc = term if acc is None else acc + term
    m = mod_ref[0, 0]
    o_ref[0] = x1_ref[0] + m[5:6] * _rms(acc, gpost_ref[0])


def _ffn(h2, x1, mod, ctx_row, pw, l, tm):
    b, n, d = x1.shape
    tm = min(tm, n)
    row = lambda bi, i: (bi, i, 0)
    r = tm // FFN_HALO
    nblk = n // FFN_HALO
    names = ("g_post_ffn", "wup", "w_ffn_conv", "b_ffn_conv", "wdown")
    return pl.pallas_call(
        _ffn_kernel,
        grid=(b, n // tm),
        in_specs=[pl.BlockSpec((1, tm, d), row),
                  pl.BlockSpec((1, FFN_HALO, d), lambda bi, i: (bi, jnp.maximum(i * r - 1, 0), 0)),
                  pl.BlockSpec((1, FFN_HALO, d), lambda bi, i: (bi, jnp.minimum((i + 1) * r, nblk - 1), 0)),
                  pl.BlockSpec((1, tm, d), row),
                  _mod_spec(l, ctx_row)]
                 + [_layer_spec(pw[k], l) for k in names],
        out_specs=pl.BlockSpec((1, tm, d), row),
        out_shape=jax.ShapeDtypeStruct((b, n, d), F32),
        compiler_params=_cparams(("parallel", "parallel")),
        name="conv_ffn",
    )(h2, h2, h2, x1, mod, *[pw[k] for k in names])


def _rope_cos_sin(n, d, identity):
    if identity:
        return jnp.ones((n, d), F32), jnp.zeros((n, d), F32)
    half, nf = d // 2, d // 4
    t = jnp.arange(n, dtype=jnp.int32)
    row = (t // GRID_W).astype(F32)[:, None]
    colp = (t % GRID_W).astype(F32)[:, None]
    inv = ROPE_BASE ** (-jnp.arange(nf, dtype=F32) / nf)
    ang = jnp.concatenate([row * inv, row * inv, colp * inv, colp * inv], axis=-1)
    sign = jnp.asarray(np.where(np.arange(d) % half < nf, -1.0, 1.0), F32)
    return jnp.cos(ang), jnp.sin(ang) * sign


def _rope_table(n, identity):
    ca, sa = _rope_cos_sin(n, MLA_ROPE, identity)
    cw, sw = _rope_cos_sin(n, HEAD_DIM, identity)
    scale_a = (MLA_NOPE + MLA_ROPE) ** -0.5 * LOG2E
    scale_w = HEAD_DIM ** -0.5 * LOG2E
    z32 = jnp.zeros((n, LANES - MLA_NOPE - MLA_ROPE), F32)
    one64 = jnp.ones((n, MLA_NOPE), F32)
    zero64 = jnp.zeros((n, MLA_NOPE), F32)
    qa_cos = jnp.concatenate([one64, ca, z32], -1) * scale_a
    qa_sin = jnp.concatenate([zero64, sa, z32], -1) * scale_a
    ka_cos = jnp.concatenate([zero64, ca, z32], -1)
    ka_sin = jnp.concatenate([zero64, sa, z32], -1)
    cw2 = jnp.concatenate([cw, cw], -1)
    sw2 = jnp.concatenate([sw, sw], -1)
    return jnp.concatenate([qa_cos, qa_sin, ka_cos, ka_sin, cw2 * scale_w, sw2 * scale_w, cw2, sw2], axis=-1)


def _dft_tables(n):
    n1 = n // FN_GROUP_W
    k = jnp.arange(n, dtype=jnp.int32)[:, None]
    ang1 = ((k * jnp.arange(n1, dtype=jnp.int32)[None, :]) % n1).astype(F32) * (2.0 * math.pi / n1)
    ang0 = ((k * jnp.arange(FN_GROUP_W, dtype=jnp.int32)[None, :]) % n).astype(F32) * (2.0 * math.pi / n)
    c1, s1 = jnp.cos(ang1)[:, :, None], jnp.sin(ang1)[:, :, None]
    c0, s0 = jnp.cos(ang0)[:, None, :], jnp.sin(ang0)[:, None, :]
    cos_kt = (c1 * c0 - s1 * s0).reshape(n, n)
    sin_kt = (s1 * c0 + c1 * s0).reshape(n, n)
    half = n // 2
    col = jnp.arange(half, dtype=jnp.int32)[None, :]
    dft = jnp.concatenate([cos_kt[:, :half], jnp.where(col == 0, cos_kt[:, half:half + 1], -sin_kt[:, :half])],
                          axis=-1).astype(BF16)
    j = np.arange(FN_GROUP_W)
    a64 = (np.outer(j, j) % FN_GROUP_W) * (2.0 * np.pi / FN_GROUP_W)
    eye = np.eye(FN_GROUPS)
    cs = np.concatenate([np.kron(eye, np.cos(a64)), np.kron(eye, np.sin(a64))], axis=-1)
    return dft, jnp.asarray(cs, F32).astype(BF16)


def _pad_last(a, width):
    return jnp.pad(a, [(0, 0)] * (a.ndim - 1) + [(0, width - a.shape[-1])])


def _prep_weights(p):
    w_in = p["w_in"]
    depth, d, _ = w_in.shape
    vec = lambda a: a.reshape(depth, 1, a.shape[-1])
    off = [int(o) for o in np.concatenate([[0], np.cumsum(IN_SPLITS)])]
    seg = lambda i, j=None: w_in[:, :, off[i]:off[i + 1 if j is None else j]]
    dup = lambda t: jnp.broadcast_to(t.reshape(depth, d, WG_KV_HEADS, 1, HEAD_DIM),
                                     (depth, d, WG_KV_HEADS, 2, HEAD_DIM)).reshape(depth, d, 4 * HEAD_DIM)
    zeros = lambda w: jnp.zeros((depth, d, w), F32)
    w1 = jnp.concatenate([seg(0, 2), zeros(MLA_NOPE), seg(2), zeros(LANES - MLA_NOPE - MLA_ROPE),
                          seg(8), dup(seg(9)), dup(seg(10)), seg(11), seg(3, 5)], axis=-1)
    assert w1.shape[-1] == C_END
    gates = _pad_last(seg(7).reshape(depth, d, 4, ML_HEADS), 8).reshape(depth, d, GT_ROWS)
    w1t = jnp.swapaxes(jnp.concatenate([seg(5, 7), gates], axis=-1), 1, 2)
    assert w1t.shape[1] == R_END
    bgt = _pad_last(p["b_ml_gates"].reshape(depth, 4, ML_HEADS), 8).reshape(depth, GT_ROWS)
    bgt = jnp.broadcast_to(bgt[:, :, None], (depth, GT_ROWS, LANES))
    rq, rkv = p["w_uq"].shape[1], p["w_ukv"].shape[1]
    wq = _pad_last(p["w_uq"].reshape(depth, rq, MLA_HEADS, MLA_NOPE + MLA_ROPE), LANES).reshape(depth, rq, -1)
    kv = p["w_ukv"].reshape(depth, rkv, MLA_HEADS, MLA_NOPE + MLA_V)
    wkv = jnp.concatenate([_pad_last(kv[..., :MLA_NOPE], LANES).reshape(depth, rkv, -1),
                           kv[..., MLA_NOPE:].reshape(depth, rkv, -1)], axis=-1)
    return {
        "w1": w1.astype(BF16),
        "w1t": w1t.astype(BF16),
        "wq": wq.astype(BF16),
        "wkv": wkv.astype(BF16),
        "g_qa": vec(p["g_qa"]), "g_kva": vec(p["g_kva"]),
        "g_pre_mix": vec(p["g_pre_mix"]), "g_post_mix": vec(p["g_post_mix"]),
        "g_pre_ffn": vec(p["g_pre_ffn"]), "g_post_ffn": vec(p["g_post_ffn"]),
        "w_ml_conv": p["w_ml_conv"], "bgt": bgt, "wg_sink": p["wg_sink"],
        "wg": p["w_gate"].astype(BF16), "b_gate": p["b_gate"],
        "wb": p["w_branch"].astype(BF16), "wo": p["w_out"].astype(BF16),
        "wup": p["w_up"].astype(BF16), "w_ffn_conv": p["w_ffn_conv"],
        "b_ffn_conv": vec(p["b_ffn_conv"]), "wdown": p["w_down"].astype(BF16),
    }


TM = 512


def kernel(x, c, ctx, c_ctx, w_mod, b_mod, g_pre_mix, g_post_mix, g_pre_ffn, g_post_ffn, w_in, g_qa, w_uq, g_kva,
           w_ukv, w_ml_conv, b_ml_gates, wg_sink, w_gate, b_gate, w_branch, w_out, w_up, w_ffn_conv, b_ffn_conv,
           w_down):
    p = dict(g_pre_mix=g_pre_mix, g_post_mix=g_post_mix, g_pre_ffn=g_pre_ffn, g_post_ffn=g_post_ffn, w_in=w_in,
             g_qa=g_qa, w_uq=w_uq, g_kva=g_kva, w_ukv=w_ukv, w_ml_conv=w_ml_conv, b_ml_gates=b_ml_gates,
             wg_sink=wg_sink, w_gate=w_gate, b_gate=b_gate, w_branch=w_branch, w_out=w_out, w_up=w_up,
             w_ffn_conv=w_ffn_conv, b_ffn_conv=b_ffn_conv, w_down=w_down)
    bsz, n, d = x.shape
    n_ctx = ctx.shape[1]
    depth = w_mod.shape[0]
    rows = -(-(bsz + 1) // 8) * 8
    c_all = jnp.concatenate([c, c_ctx[None, :], jnp.zeros((rows - bsz - 1, d), F32)], axis=0)
    mod = _modulation(c_all, w_mod, b_mod).reshape(depth, rows, 6, d)
    pw = _prep_weights(p)
    tab_l = _rope_table(n, identity=False)
    tab_c = _rope_table(n_ctx, identity=True)
    dft_l, cs64 = _dft_tables(n)
    dft_c, _ = _dft_tables(n_ctx)
    xc = ctx
    for l in range(depth):
        ctx_out = l < depth - 1
        qa, ka, va, uq, uk, vt, ot, gt, qw, kw, vw, uf = _inproj(x, mod, None, pw, l, tab_l, TM)
        qa_c, ka_c, va_c, uq_c, uk_c, vt_c, ot_c, gt_c, qw_c, kw_c, vw_c, uf_c = _inproj(
            xc, mod, bsz, pw, l, tab_c, TM)
        ya = _mla_attend(qa, ka_c, va_c, ka, va)
        yb, yb_c = _mlstm((uq, uk, vt, ot, gt), (uq_c, uk_c, vt_c, ot_c, gt_c), pw, l, ctx_out)
        yc = _win_attend(pw["wg_sink"], l, qw, kw_c, vw_c, kw, vw)
        yd = _fourier(uf, cs64, dft_l)
        x1, h2 = _merge(x, mod, None, (ya, yb, yc, yd), pw, l, TM)
        x = _ffn(h2, x1, mod, None, pw, l, TM)
        if ctx_out:
            ya_c = _mla_attend(qa_c, ka_c, va_c)
            yc_c = _win_attend(pw["wg_sink"], l, qw_c, kw_c, vw_c)
            yd_c = _fourier(uf_c, cs64, dft_c)
            xc1, hc2 = _merge(xc, mod, bsz, (ya_c, yb_c, yc_c, yd_c), pw, l, TM)
            xc = _ffn(hc2, xc1, mod, bsz, pw, l, TM)
    return x
```
